```python
import jax
import jax.numpy as jnp
from jax import lax
import numpy as np

D_MODEL = 2048
BATCH = 2
SEQ = 4096
DEPTH = 4
DEC_BATCH = 32
DEC_SEQ = 1
PAST_LEN = 16384
PAGE_SIZE = 128

N_MIXERS = 3
ATTN_HEADS = 32
ATTN_KV_HEADS = 8
HEAD_DIM = 64
WINDOW = 128
ATTN_BLOCK = 128
SCONV_WIDTH = 3
SSM_EXPAND = 2
D_INNER = SSM_EXPAND * D_MODEL
SSM_HEAD_DIM = 64
SSM_HEADS = D_INNER // SSM_HEAD_DIM
SSM_GROUPS = 8
D_STATE = 128
SSM_CONV_WIDTH = 4
SSM_CHUNK = 128
SSM_CONV_DIM = D_INNER + 2 * SSM_GROUPS * D_STATE
D_FF = 7 * D_MODEL // 2
N_EXPERTS = 8
TOP_K = 2
PLE_DIM = 256
NORM_EPS = 1e-6

N_ATTN_LAYERS = (DEPTH + N_MIXERS - 1) // N_MIXERS
N_CONV_LAYERS = (DEPTH + N_MIXERS - 2) // N_MIXERS
N_SSM_LAYERS = DEPTH // N_MIXERS
N_DENSE_LAYERS = (DEPTH + 1) // 2
N_MOE_LAYERS = DEPTH // 2

kernel_name = "hybrid_swa_shortconv_ssd_moe_step"


def rms_norm(x, w):
    x32 = x.astype(jnp.float32)
    y = x32 * lax.rsqrt(jnp.mean(x32 * x32, axis=-1, keepdims=True) + NORM_EPS)
    return (y * w.astype(jnp.float32)).astype(x.dtype)


def alibi_slopes(n_heads):
    return jnp.exp2(-8.0 * jnp.arange(1, n_heads + 1, dtype=jnp.float32) / n_heads)


def causal_depthwise_conv(u_ext, w):
    width = w.shape[0]
    length = u_ext.shape[1] - width + 1
    return sum(w[k] * u_ext[:, k:k + length] for k in range(width))


def window_attention_mixer(hn, k_past, v_past, start_pos, w_qkv, q_gain, k_gain, sinks, w_o):
    b, L, _ = hn.shape
    rep = ATTN_HEADS // ATTN_KV_HEADS
    q, k, v = jnp.split(hn @ w_qkv, [ATTN_HEADS * HEAD_DIM, (ATTN_HEADS + ATTN_KV_HEADS) * HEAD_DIM], axis=-1)
    q = rms_norm(q.reshape(b, L, ATTN_HEADS, HEAD_DIM), q_gain)
    k = rms_norm(k.reshape(b, L, ATTN_KV_HEADS, HEAD_DIM), k_gain)
    v = v.reshape(b, L, ATTN_KV_HEADS, HEAD_DIM)
    k_ext = jnp.concatenate([k_past.astype(k.dtype), k], axis=1)
    v_ext = jnp.concatenate([v_past.astype(v.dtype), v], axis=1)
    new_k, new_v = k_ext[:, -WINDOW:], v_ext[:, -WINDOW:]
    qb = min(ATTN_BLOCK, L)
    nb = -(-L // qb)
    pad = nb * qb - L
    q = jnp.pad(q, ((0, 0), (0, pad), (0, 0), (0, 0)))
    k_ext = jnp.pad(k_ext, ((0, 0), (0, pad), (0, 0), (0, 0)))
    v_ext = jnp.pad(v_ext, ((0, 0), (0, pad), (0, 0), (0, 0)))
    kb = WINDOW + qb
    idx = jnp.arange(nb)[:, None] * qb + jnp.arange(kb)[None, :]
    k_band = jnp.take(k_ext, idx, axis=1)
    v_band = jnp.take(v_ext, idx, axis=1)
    q_blk = q.reshape(b, nb, qb, ATTN_KV_HEADS, rep, HEAD_DIM)
    scores = jnp.einsum("bnqgrd,bnkgd->bngrqk", q_blk, k_band).astype(jnp.float32) * (HEAD_DIM ** -0.5)
    delta = WINDOW + jnp.arange(qb)[:, None] - jnp.arange(kb)[None, :]
    key_pos = start_pos - WINDOW + idx
    mask = (delta >= 0)[None] & (delta <= WINDOW)[None] & (key_pos >= 0)[:, None, :]
    slopes = alibi_slopes(ATTN_HEADS).reshape(ATTN_KV_HEADS, rep)
    scores = scores - slopes[None, None, :, :, None, None] * delta.astype(jnp.float32)
    scores = jnp.where(mask[None, :, None, None], scores, -jnp.inf)
    sink = sinks.astype(jnp.float32).reshape(ATTN_KV_HEADS, rep)[None, None, :, :, None, None]
    m = jnp.maximum(jnp.max(scores, axis=-1, keepdims=True), sink)
    e = jnp.exp(scores - m)
    probs = e / (jnp.sum(e, axis=-1, keepdims=True) + jnp.exp(sink - m))
    o = jnp.einsum("bngrqk,bnkgd->bnqgrd", probs.astype(v_band.dtype), v_band)
    o = o.reshape(b, nb * qb, ATTN_HEADS * HEAD_DIM)[:, :L]
    return o @ w_o, new_k, new_v


def short_conv_mixer(hn, conv_past, w_in, conv_w, w_out):
    gate_b, gate_c, u = jnp.split(hn @ w_in, 3, axis=-1)
    u_ext = jnp.concatenate([conv_past.astype(u.dtype), gate_c * u], axis=1)
    y = gate_b * causal_depthwise_conv(u_ext, conv_w)
    return y @ w_out, u_ext[:, -(SCONV_WIDTH - 1):]


def ssd_chunked_scan(x, dt, a, bm, cm, h0):
    f32 = jnp.float32
    b, L, H, P = x.shape
    G, N = bm.shape[2], bm.shape[3]
    rep = H // G
    q = min(SSM_CHUNK, L)
    nc = -(-L // q)
    pad = nc * q - L
    x = jnp.pad(x.astype(f32), ((0, 0), (0, pad), (0, 0), (0, 0)))
    dt = jnp.pad(dt, ((0, 0), (0, pad), (0, 0)))
    bm = jnp.pad(bm.astype(f32), ((0, 0), (0, pad), (0, 0), (0, 0))).reshape(b, nc, q, G, N)
    cm = jnp.pad(cm.astype(f32), ((0, 0), (0, pad), (0, 0), (0, 0))).reshape(b, nc, q, G, N)
    xdt = (x * dt[..., None]).reshape(b, nc, q, G, rep, P)
    da = (dt * a).reshape(b, nc, q, G, rep).transpose(0, 3, 4, 1, 2)
    a_cs = jnp.cumsum(da, axis=-1)
    seg = a_cs[..., :, None] - a_cs[..., None, :]
    tri = jnp.tril(jnp.ones((q, q), dtype=bool))
    decay_in = jnp.where(tri, jnp.exp(jnp.where(tri, seg, 0.0)), 0.0)
    cb = jnp.einsum("bclgn,bcsgn->bcgls", cm, bm)
    y_diag = jnp.einsum("bcgls,bgrcls,bcsgrp->bclgrp", cb, decay_in, xdt)
    decay_out = jnp.exp(a_cs[..., -1:] - a_cs)
    chunk_states = jnp.einsum("bclgn,bgrcl,bclgrp->cbgrpn", bm, decay_out, xdt)
    chunk_decay = jnp.exp(a_cs[..., -1]).transpose(3, 0, 1, 2)

    def carry_step(h, inp):
        st, dec = inp
        return dec[..., None, None] * h + st, h

    h_final, h_enter = lax.scan(carry_step, h0.astype(f32).reshape(b, G, rep, P, N), (chunk_states, chunk_decay))
    y_off = jnp.einsum("bclgn,cbgrpn,bgrcl->bclgrp", cm, h_enter, jnp.exp(a_cs))
    y = (y_diag + y_off).reshape(b, nc * q, H, P)[:, :L]
    return y, h_final.reshape(b, H, P, N)


def gated_group_rms_norm(y, z, w):
    g = y.astype(jnp.float32) * jax.nn.silu(z.astype(jnp.float32))
    gg = g.reshape(*g.shape[:-1], SSM_GROUPS, D_INNER // SSM_GROUPS)
    gg = gg * lax.rsqrt(jnp.mean(gg * gg, axis=-1, keepdims=True) + NORM_EPS)
    return gg.reshape(g.shape) * w.astype(jnp.float32)


def ssd_mixer(hn, conv_past, h0, w_in, conv_w, conv_b, dt_bias, a_log, d_skip, norm_w, w_out):
    b, L, _ = hn.shape
    z, xbc, dt_raw = jnp.split(hn @ w_in, [D_INNER, D_INNER + SSM_CONV_DIM], axis=-1)
    xbc_ext = jnp.concatenate([conv_past.astype(xbc.dtype), xbc], axis=1)
    xbc = jax.nn.silu(causal_depthwise_conv(xbc_ext, conv_w) + conv_b)
    xs, bm, cm = jnp.split(xbc, [D_INNER, D_INNER + SSM_GROUPS * D_STATE], axis=-1)
    dt = jax.nn.softplus(dt_raw.astype(jnp.float32) + dt_bias.astype(jnp.float32))
    a = -jnp.exp(a_log.astype(jnp.float32))
    xs = xs.reshape(b, L, SSM_HEADS, SSM_HEAD_DIM)
    y, h_new = ssd_chunked_scan(xs, dt, a, bm.reshape(b, L, SSM_GROUPS, D_STATE),
                                cm.reshape(b, L, SSM_GROUPS, D_STATE), h0)
    y = y + d_skip.astype(jnp.float32)[:, None] * xs.astype(jnp.float32)
    y = gated_group_rms_norm(y.reshape(b, L, D_INNER), z, norm_w).astype(hn.dtype)
    return y @ w_out, h_new.astype(h0.dtype), xbc_ext[:, -(SSM_CONV_WIDTH - 1):]


def swiglu(x, w_gate, w_up, w_down):
    return (jax.nn.silu(x @ w_gate) * (x @ w_up)) @ w_down


def moe_swiglu(xn, w_router, w_gate, w_up, w_down):
    probs = jax.nn.softmax((xn @ w_router).astype(jnp.float32), axis=-1)
    top_p, top_i = lax.top_k(probs, TOP_K)
    top_p = top_p / jnp.sum(top_p, axis=-1, keepdims=True)
    gates = jnp.sum(jax.nn.one_hot(top_i, N_EXPERTS, dtype=jnp.float32) * top_p[..., None], axis=-2)
    y = jnp.zeros_like(xn)
    for e in range(N_EXPERTS):
        y = y + gates[..., e:e + 1].astype(xn.dtype) * swiglu(xn, w_gate[e], w_up[e], w_down[e])
    return y


def per_layer_gated_input(h, p_i, norm_w, w_gate, w_proj):
    return jax.nn.sigmoid(rms_norm(h, norm_w) @ w_gate) * (p_i @ w_proj)


def setup_inputs(seed: int = 0) -> dict:
    key = jax.random.key(seed)
    ks = iter(jax.random.split(key, 48))
    f32 = jnp.float32

    def normal(shape, scale=1.0):
        return jax.random.normal(next(ks), shape, f32) * scale

    def gain(shape):
        return 1.0 + 0.05 * normal(shape)

    dt0 = jnp.exp(jax.random.uniform(next(ks), (N_SSM_LAYERS, SSM_HEADS), f32,
                                     minval=float(np.log(1e-3)), maxval=float(np.log(1e-1))))
    qkv_cols = (ATTN_HEADS + 2 * ATTN_KV_HEADS) * HEAD_DIM
    ssm_in_cols = 2 * D_INNER + 2 * SSM_GROUPS * D_STATE + SSM_HEADS
    return {
        "x_prompt": normal((BATCH, SEQ, D_MODEL)),
        "x_sample": normal((DEC_BATCH, DEC_SEQ, D_MODEL)),
        "cache_k_win": normal((N_ATTN_LAYERS, DEC_BATCH, WINDOW, ATTN_KV_HEADS, HEAD_DIM)),
        "cache_v_win": normal((N_ATTN_LAYERS, DEC_BATCH, WINDOW, ATTN_KV_HEADS, HEAD_DIM)),
        "state_short_conv": normal((N_CONV_LAYERS, DEC_BATCH, SCONV_WIDTH - 1, D_MODEL)),
        "state_ssm": normal((N_SSM_LAYERS, DEC_BATCH, SSM_HEADS, SSM_HEAD_DIM, D_STATE), 0.5),
        "state_ssm_conv": normal((N_SSM_LAYERS, DEC_BATCH, SSM_CONV_WIDTH - 1, SSM_CONV_DIM)),
        "p_prompt": normal((DEPTH, BATCH, SEQ, PLE_DIM)),
        "p_sample": normal((DEPTH, DEC_BATCH, DEC_SEQ, PLE_DIM)),
        "norm_mixer": gain((DEPTH, D_MODEL)),
        "norm_ffn": gain((DEPTH, D_MODEL)),
        "attn_w_qkv": normal((N_ATTN_LAYERS, D_MODEL, qkv_cols), D_MODEL ** -0.5),
        "attn_q_norm": gain((N_ATTN_LAYERS, HEAD_DIM)),
        "attn_k_norm": gain((N_ATTN_LAYERS, HEAD_DIM)),
        "attn_sinks": normal((N_ATTN_LAYERS, ATTN_HEADS)),
        "attn_w_out": normal((N_ATTN_LAYERS, ATTN_HEADS * HEAD_DIM, D_MODEL), (ATTN_HEADS * HEAD_DIM) ** -0.5),
        "sconv_w_in": normal((N_CONV_LAYERS, D_MODEL, 3 * D_MODEL), D_MODEL ** -0.5),
        "sconv_conv_w": normal((N_CONV_LAYERS, SCONV_WIDTH, D_MODEL), SCONV_WIDTH ** -0.5),
        "sconv_w_out": normal((N_CONV_LAYERS, D_MODEL, D_MODEL), D_MODEL ** -0.5),
        "ssm_w_in": normal((N_SSM_LAYERS, D_MODEL, ssm_in_cols), D_MODEL ** -0.5),
        "ssm_conv_w": normal((N_SSM_LAYERS, SSM_CONV_WIDTH, SSM_CONV_DIM), SSM_CONV_WIDTH ** -0.5),
        "ssm_conv_b": normal((N_SSM_LAYERS, SSM_CONV_DIM), 0.02),
        "ssm_dt_bias": dt0 + jnp.log(-jnp.expm1(-dt0)),
        "ssm_a_log": jnp.log(jax.random.uniform(next(ks), (N_SSM_LAYERS, SSM_HEADS), f32, minval=1.0, maxval=16.0)),
        "ssm_d_skip": 1.0 + 0.1 * normal((N_SSM_LAYERS, SSM_HEADS)),
        "ssm_norm": gain((N_SSM_LAYERS, D_INNER)),
        "ssm_w_out": normal((N_SSM_LAYERS, D_INNER, D_MODEL), D_INNER ** -0.5),
        "ffn_w_gate": normal((N_DENSE_LAYERS, D_MODEL, D_FF), D_MODEL ** -0.5),
        "ffn_w_up": normal((N_DENSE_LAYERS, D_MODEL, D_FF), D_MODEL ** -0.5),
        "ffn_w_down": normal((N_DENSE_LAYERS, D_FF, D_MODEL), D_FF ** -0.5),
        "moe_w_router": normal((N_MOE_LAYERS, D_MODEL, N_EXPERTS), D_MODEL ** -0.5),
        "moe_w_gate": normal((N_MOE_LAYERS, N_EXPERTS, D_MODEL, D_FF), D_MODEL ** -0.5),
        "moe_w_up": normal((N_MOE_LAYERS, N_EXPERTS, D_MODEL, D_FF), D_MODEL ** -0.5),
        "moe_w_down": normal((N_MOE_LAYERS, N_EXPERTS, D_FF, D_MODEL), D_FF ** -0.5),
        "ple_norm": gain((DEPTH, D_MODEL)),
        "ple_w_gate": normal((DEPTH, D_MODEL, D_MODEL), D_MODEL ** -0.5),
        "ple_w_proj": normal((DEPTH, PLE_DIM, D_MODEL), PLE_DIM ** -0.5),
    }


def reference(x_prompt, x_sample, cache_k_win, cache_v_win, state_short_conv, state_ssm, state_ssm_conv,
              p_prompt, p_sample, norm_mixer, norm_ffn, attn_w_qkv, attn_q_norm, attn_k_norm, attn_sinks,
              attn_w_out, sconv_w_in, sconv_conv_w, sconv_w_out, ssm_w_in, ssm_conv_w, ssm_conv_b,
              ssm_dt_bias, ssm_a_log, ssm_d_skip, ssm_norm, ssm_w_out, ffn_w_gate, ffn_w_up, ffn_w_down,
              moe_w_router, moe_w_gate, moe_w_up, moe_w_down, ple_norm, ple_w_gate, ple_w_proj):
    dtype = x_prompt.dtype
    prompt_past = {
        0: (jnp.zeros((BATCH, WINDOW, ATTN_KV_HEADS, HEAD_DIM), dtype),
            jnp.zeros((BATCH, WINDOW, ATTN_KV_HEADS, HEAD_DIM), dtype)),
        1: (jnp.zeros((BATCH, SCONV_WIDTH - 1, D_MODEL), dtype),),
        2: (jnp.zeros((BATCH, SSM_CONV_WIDTH - 1, SSM_CONV_DIM), dtype),
            jnp.zeros((BATCH, SSM_HEADS, SSM_HEAD_DIM, D_STATE), dtype)),
    }

    def layer(i, h, p_i, past, start_pos):
        kind, j = i % N_MIXERS, i // N_MIXERS
        hn = rms_norm(h, norm_mixer[i])
        if kind == 0:
            out, *new = window_attention_mixer(hn, past[0], past[1], start_pos, attn_w_qkv[j], attn_q_norm[j],
                                               attn_k_norm[j], attn_sinks[j], attn_w_out[j])
        elif kind == 1:
            out, *new = short_conv_mixer(hn, past[0], sconv_w_in[j], sconv_conv_w[j], sconv_w_out[j])
        else:
            out, *new = ssd_mixer(hn, past[0], past[1], ssm_w_in[j], ssm_conv_w[j], ssm_conv_b[j], ssm_dt_bias[j],
                                  ssm_a_log[j], ssm_d_skip[j], ssm_norm[j], ssm_w_out[j])
        h = h + out
        hn = rms_norm(h, norm_ffn[i])
        c = i // 2
        if i % 2 == 0:
            h = h + swiglu(hn, ffn_w_gate[c], ffn_w_up[c], ffn_w_down[c])
        else:
            h = h + moe_swiglu(hn, moe_w_router[c], moe_w_gate[c], moe_w_up[c], moe_w_down[c])
        h = h + per_layer_gated_input(h, p_i, ple_norm[i], ple_w_gate[i], ple_w_proj[i])
        return h, new

    k_win_p, v_win_p, k_win_s, v_win_s = [], [], [], []
    sconv_p, sconv_s = [], []
    ssm_p, ssm_s, ssm_conv_p, ssm_conv_s = [], [], [], []
    h_p, h_s = x_prompt, x_sample
    for i in range(DEPTH):
        kind, j = i % N_MIXERS, i // N_MIXERS
        if kind == 0:
            past_s = (cache_k_win[j], cache_v_win[j])
        elif kind == 1:
            past_s = (state_short_conv[j],)
        else:
            past_s = (state_ssm_conv[j], state_ssm[j])
        h_p, new_p = layer(i, h_p, p_prompt[i], prompt_past[kind], 0)
        h_s, new_s = layer(i, h_s, p_sample[i], past_s, PAST_LEN)
        if kind == 0:
            k_win_p.append(new_p[0]); v_win_p.append(new_p[1])
            k_win_s.append(new_s[0]); v_win_s.append(new_s[1])
        elif kind == 1:
            sconv_p.append(new_p[0]); sconv_s.append(new_s[0])
        else:
            ssm_p.append(new_p[0]); ssm_conv_p.append(new_p[1])
            ssm_s.append(new_s[0]); ssm_conv_s.append(new_s[1])
    return (h_p, h_s,
            jnp.stack(k_win_p), jnp.stack(v_win_p), jnp.stack(k_win_s), jnp.stack(v_win_s),
            jnp.stack(sconv_p), jnp.stack(sconv_s),
            jnp.stack(ssm_p), jnp.stack(ssm_s),
            jnp.stack(ssm_conv_p), jnp.stack(ssm_conv_s))
```

```python
import functools

import jax
import jax.numpy as jnp
from jax import lax
from jax.experimental import pallas as pl
from jax.experimental.pallas import tpu as pltpu

F32 = jnp.float32
BF16 = jnp.bfloat16

D_MODEL = 2048
BATCH = 2
SEQ = 4096
DEPTH = 4
DEC_BATCH = 32
PAST_LEN = 16384
ATTN_HEADS = 32
ATTN_KV_HEADS = 8
HEAD_DIM = 64
WINDOW = 128
SCONV_WIDTH = 3
D_INNER = 2 * D_MODEL
SSM_HEAD_DIM = 64
SSM_HEADS = D_INNER // SSM_HEAD_DIM
SSM_GROUPS = 8
D_STATE = 128
SSM_CONV_WIDTH = 4
SSM_CHUNK = 128
SSM_CONV_DIM = D_INNER + 2 * SSM_GROUPS * D_STATE
D_FF = 7 * D_MODEL // 2
N_EXPERTS = 8
PLE_DIM = 256
NORM_EPS = 1e-6
NQ = ATTN_HEADS * HEAD_DIM
NKV = ATTN_KV_HEADS * HEAD_DIM
assert PAST_LEN >= WINDOW

LANES = 128
SUBLANES = 8
T = BATCH * SEQ
NS = DEC_BATCH
TM_BIG = 1024
TM_MID = 512
TM_MOE = 1024
MOE_TILES = (2 * T + N_EXPERTS * (TM_MOE - 1)) // TM_MOE
R_PAD = MOE_TILES * TM_MOE
ROW_TILES = D_MODEL // LANES
VMEM_LIMIT = 56 * 1024 * 1024


def _params(*sem):
    return pltpu.CompilerParams(dimension_semantics=sem, vmem_limit_bytes=VMEM_LIMIT)


def _rms(x, g):
    return x * lax.rsqrt(jnp.mean(x * x, axis=-1, keepdims=True) + NORM_EPS) * g


def _dot(a, b):
    return jnp.dot(a, b, preferred_element_type=F32)


def _dot_nt(a, b):
    return lax.dot_general(a, b, (((1,), (1,)), ((), ())), preferred_element_type=F32)


def _dot_tn(a, b):
    return lax.dot_general(a, b, (((0,), (0,)), ((), ())), preferred_element_type=F32)


def _softplus(x):
    return jnp.maximum(x, 0.0) + jnp.log1p(jnp.exp(-jnp.abs(x)))


def _alibi_slopes():
    return jnp.exp2(-8.0 * jnp.arange(1, ATTN_HEADS + 1, dtype=F32) / ATTN_HEADS)


def _norm_kernel(x_ref, g_ref, o_ref):
    o_ref[...] = _rms(x_ref[...], g_ref[...]).astype(o_ref.dtype)


def norm_rows(x, gain, dtype, tm=TM_BIG):
    t, d = x.shape
    return pl.pallas_call(
        _norm_kernel,
        grid=(t // tm,),
        in_specs=[pl.BlockSpec((tm, d), lambda i: (i, 0)),
                  pl.BlockSpec((1, d), lambda i: (0, 0))],
        out_specs=pl.BlockSpec((tm, d), lambda i: (i, 0)),
        out_shape=jax.ShapeDtypeStruct((t, d), dtype),
        compiler_params=_params("arbitrary"),
        name="norm_rows",
    )(x, gain.reshape(1, d))


def _mm_kernel(x_ref, w_ref, o_ref, wb_ref):
    @pl.when(pl.program_id(1) == 0)
    def _():
        wb_ref[...] = w_ref[...].astype(BF16)

    o_ref[...] = _dot(x_ref[...], wb_ref[...])


def matmul_cols(x, w, layer, n_out, col_off=0, tn=512, tm=TM_BIG):
    t, k = x.shape
    tn = min(tn, n_out)
    off = col_off // tn
    assert col_off % tn == 0 and n_out % tn == 0
    return pl.pallas_call(
        _mm_kernel,
        grid=(n_out // tn, t // tm),
        in_specs=[pl.BlockSpec((tm, k), lambda n, m: (m, 0)),
                  pl.BlockSpec((None, k, tn), lambda n, m: (layer, 0, n + off))],
        out_specs=pl.BlockSpec((tm, tn), lambda n, m: (m, n)),
        out_shape=jax.ShapeDtypeStruct((t, n_out), F32),
        scratch_shapes=[pltpu.VMEM((k, tn), BF16)],
        compiler_params=_params("arbitrary", "arbitrary"),
        name="matmul_cols",
    )(x, w)


def _proj_kernel(*refs, nk, gated, with_norm):
    x_ref, w_ref, res_ref = refs[:3]
    refs = refs[3:]
    if gated:
        p_ref, wp_ref = refs[:2]
        refs = refs[2:]
    if with_norm:
        g_ref, h_ref, hn_ref = refs
    else:
        (h_ref,) = refs
    k = pl.program_id(1)
    part = _dot(x_ref[...], w_ref[...].astype(BF16))

    @pl.when(k == 0)
    def _():
        h_ref[...] = part

    @pl.when(k > 0)
    def _():
        h_ref[...] += part

    @pl.when(k == nk - 1)
    def _():
        upd = h_ref[...]
        if gated:
            upd = jax.nn.sigmoid(upd) * _dot(p_ref[...].astype(BF16), wp_ref[...].astype(BF16))
        h_new = res_ref[...] + upd
        h_ref[...] = h_new
        if with_norm:
            hn_ref[...] = _rms(h_new, g_ref[...]).astype(BF16)


def proj_residual(x, w, layer, res, gain=None, p=None, wp=None, tm=TM_MID, tk=512):
    t, kdim = x.shape
    d = w.shape[2]
    nk = kdim // tk
    gated = p is not None
    with_norm = gain is not None
    in_specs = [pl.BlockSpec((tm, tk), lambda m, k: (m, k)),
                pl.BlockSpec((None, tk, d), lambda m, k: (layer, k, 0)),
                pl.BlockSpec((tm, d), lambda m, k: (m, 0))]
    args = [x, w, res]
    if gated:
        in_specs += [pl.BlockSpec((tm, p.shape[1]), lambda m, k: (m, 0)),
                     pl.BlockSpec((None,) + wp.shape[1:], lambda m, k: (layer, 0, 0))]
        args += [p, wp]
    out_specs = [pl.BlockSpec((tm, d), lambda m, k: (m, 0))]
    out_shape = [jax.ShapeDtypeStruct((t, d), F32)]
    if with_norm:
        in_specs.append(pl.BlockSpec((1, d), lambda m, k: (0, 0)))
        args.append(gain.reshape(1, d))
        out_specs.append(pl.BlockSpec((tm, d), lambda m, k: (m, 0)))
        out_shape.append(jax.ShapeDtypeStruct((t, d), BF16))
    out = pl.pallas_call(
        functools.partial(_proj_kernel, nk=nk, gated=gated, with_norm=with_norm),
        grid=(t // tm, nk),
        in_specs=in_specs,
        out_specs=out_specs,
        out_shape=out_shape,
        compiler_params=_params("arbitrary", "arbitrary"),
        name="proj_gated" if gated else "proj_residual",
    )(*args)
    return (out[0], out[1]) if with_norm else (out[0], None)


FFN_DOWN_COLS = 512


def _tile_slab(tiled_ref, s):
    return tiled_ref[pl.ds(s, tiled_ref.shape[0] // ROW_TILES, stride=ROW_TILES), :]


def _rows_from_tiles(tiled_ref, rows_ref):
    for s in range(ROW_TILES):
        rows_ref[:, s * LANES:(s + 1) * LANES] = _tile_slab(tiled_ref, s).astype(rows_ref.dtype)


def _ffn_kernel(te_ref, na_ref, x_ref, wg_ref, wu_ref, wd_ref, *refs, nf, routed):
    if routed:
        gate_ref, out_ref, xb_ref = refs
    else:
        res_ref, g_ref, out_ref, hn_ref = refs
    i = pl.program_id(0)
    f = pl.program_id(1)

    @pl.when(i < na_ref[0])
    def _():
        if routed:
            @pl.when(f == 0)
            def _():
                _rows_from_tiles(x_ref, xb_ref)
            x = xb_ref[...]
        else:
            x = x_ref[...]
        g = _dot(x, wg_ref[...].astype(BF16))
        u = _dot(x, wu_ref[...].astype(BF16))
        a = (jax.nn.silu(g) * u).astype(BF16)
        for n in range(out_ref.shape[1] // FFN_DOWN_COLS):
            sl = slice(n * FFN_DOWN_COLS, (n + 1) * FFN_DOWN_COLS)
            part = _dot(a, wd_ref[:, sl].astype(BF16))

            @pl.when(f == 0)
            def _():
                out_ref[:, sl] = part

            @pl.when(f > 0)
            def _():
                out_ref[:, sl] += part

        @pl.when(f == nf - 1)
        def _():
            if routed:
                out_ref[...] = gate_ref[...] * out_ref[...]
            else:
                h_new = res_ref[...] + out_ref[...]
                out_ref[...] = h_new
                hn_ref[...] = _rms(h_new, g_ref[...]).astype(BF16)


def swiglu_tiles(x, wg, wu, wd, tile_expert, n_active, *, tm, tf, res=None, gain=None, row_gate=None):
    routed = row_gate is not None
    r = x.shape[0] // ROW_TILES if routed else x.shape[0]
    d = wd.shape[2]
    ff = wg.shape[2]
    nf = ff // tf

    def row(i, f, te, na):
        return (jnp.minimum(i, na[0] - 1), 0)

    def fcol(i, f, na):
        return jnp.where(i < na[0], f, nf - 1)

    single = pl.Buffered(1)
    w_specs = [pl.BlockSpec((None, d, tf), lambda i, f, te, na: (te[i], 0, fcol(i, f, na))),
               pl.BlockSpec((None, d, tf), lambda i, f, te, na: (te[i], 0, fcol(i, f, na))),
               pl.BlockSpec((None, tf, d), lambda i, f, te, na: (te[i], fcol(i, f, na), 0))]
    if routed:
        in_specs = [pl.BlockSpec((tm * ROW_TILES, LANES), row)]
        in_specs += w_specs + [pl.BlockSpec((tm, 1), row)]
        args = [x, wg, wu, wd, row_gate]
        out_specs = pl.BlockSpec((tm, d), row, pipeline_mode=single)
        out_shape = jax.ShapeDtypeStruct((r, d), F32)
        scratch = [pltpu.VMEM((tm, d), BF16)]
    else:
        in_specs = [pl.BlockSpec((tm, d), row)] + w_specs
        in_specs += [pl.BlockSpec((tm, d), row, pipeline_mode=single),
                     pl.BlockSpec((1, d), lambda i, f, te, na: (0, 0))]
        args = [x, wg, wu, wd, res, gain.reshape(1, d)]
        out_specs = [pl.BlockSpec((tm, d), row, pipeline_mode=single),
                     pl.BlockSpec((tm, d), row, pipeline_mode=single)]
        out_shape = [jax.ShapeDtypeStruct((r, d), F32), jax.ShapeDtypeStruct((r, d), BF16)]
        scratch = []
    return pl.pallas_call(
        functools.partial(_ffn_kernel, nf=nf, routed=routed),
        grid_spec=pltpu.PrefetchScalarGridSpec(
            num_scalar_prefetch=2,
            grid=(r // tm, nf),
            in_specs=in_specs,
            out_specs=out_specs,
            scratch_shapes=scratch),
        out_shape=out_shape,
        compiler_params=_params("arbitrary", "arbitrary"),
        name="swiglu_routed" if routed else "swiglu_dense",
    )(tile_expert, n_active, *args)


def _router_kernel(h_ref, g_ref, w_ref, o_ref, dense_ref):
    xn = _rms(h_ref[...], g_ref[...])
    logits = jnp.dot(xn, w_ref[...], preferred_element_type=F32, precision=lax.Precision.HIGHEST)
    lane = lax.broadcasted_iota(jnp.int32, logits.shape, 1)
    logits = jnp.where(lane < N_EXPERTS, logits, -jnp.inf)
    e = jnp.exp(logits - jnp.max(logits, axis=-1, keepdims=True))
    probs = e / jnp.sum(e, axis=-1, keepdims=True)
    p1 = jnp.max(probs, axis=-1, keepdims=True)
    e1 = jnp.min(jnp.where(probs == p1, lane, LANES), axis=-1, keepdims=True)
    rest = jnp.where(lane == e1, -1.0, probs)
    p2 = jnp.max(rest, axis=-1, keepdims=True)
    e2 = jnp.min(jnp.where(rest == p2, lane, LANES), axis=-1, keepdims=True)
    tot = p1 + p2
    g1 = p1 / tot
    g2 = p2 / tot
    o_ref[...] = jnp.where(lane == 0, e1.astype(F32),
                           jnp.where(lane == 1, e2.astype(F32), jnp.where(lane == 2, g1, g2)))
    dense_ref[...] = jnp.where(lane == e1, g1, jnp.where(lane == e2, g2, 0.0))


def route_top2(h, gain, w_router, tm):
    t, d = h.shape
    wpad = jnp.pad(w_router, ((0, 0), (0, LANES - N_EXPERTS)))
    return pl.pallas_call(
        _router_kernel,
        grid=(t // tm,),
        in_specs=[pl.BlockSpec((tm, d), lambda i: (i, 0)),
                  pl.BlockSpec((1, d), lambda i: (0, 0)),
                  pl.BlockSpec((d, LANES), lambda i: (0, 0))],
        out_specs=[pl.BlockSpec((tm, LANES), lambda i: (i, 0)),
                   pl.BlockSpec((tm, LANES), lambda i: (i, 0))],
        out_shape=[jax.ShapeDtypeStruct((t, LANES), F32), jax.ShapeDtypeStruct((t, LANES), F32)],
        compiler_params=_params("arbitrary"),
        name="route_top2",
    )(h, gain.reshape(1, d), wpad)


GATHER_CHUNK = 128
GATHER_DEPTH = 4


def _gather_kernel(idx_ref, n_ref, src_ref, dst_ref, sem):
    nchunk = n_ref[0] // GATHER_CHUNK

    def wait_chunk():
        pltpu.make_async_copy(src_ref.at[pl.ds(0, GATHER_CHUNK)], dst_ref.at[pl.ds(0, GATHER_CHUNK)], sem).wait()

    def chunk(c, carry):
        @pl.when(c >= GATHER_DEPTH)
        def _():
            wait_chunk()

        def row(j, carry2):
            r = c * GATHER_CHUNK + j
            pltpu.make_async_copy(src_ref.at[idx_ref[r]], dst_ref.at[r], sem).start()
            return carry2

        lax.fori_loop(0, GATHER_CHUNK, row, 0, unroll=8)
        return carry

    lax.fori_loop(0, nchunk, chunk, 0)

    def drain(c, carry):
        wait_chunk()
        return carry

    lax.fori_loop(0, jnp.minimum(nchunk, GATHER_DEPTH), drain, 0)


def gather_rows(src, idx, n):
    r = idx.shape[0]
    return pl.pallas_call(
        _gather_kernel,
        grid_spec=pltpu.PrefetchScalarGridSpec(
            num_scalar_prefetch=2,
            grid=(1,),
            in_specs=[pl.BlockSpec(memory_space=pl.ANY)],
            out_specs=pl.BlockSpec(memory_space=pl.ANY),
            scratch_shapes=[pltpu.SemaphoreType.DMA(())]),
        out_shape=jax.ShapeDtypeStruct((r,) + src.shape[1:], src.dtype),
        compiler_params=_params("arbitrary"),
        name="gather_rows",
    )(idx, n, src)


def _combine_kernel(h_ref, o1_ref, o2_ref, g_ref, h2_ref, hn_ref):
    for s in range(ROW_TILES):
        sl = slice(s * LANES, (s + 1) * LANES)
        h2_ref[:, sl] = h_ref[:, sl] + (_tile_slab(o1_ref, s) + _tile_slab(o2_ref, s))
    hn_ref[...] = _rms(h2_ref[...], g_ref[...]).astype(BF16)


def combine_rows(h, og, gain, tm=TM_MID):
    t, d = h.shape
    nt = t // tm
    return pl.pallas_call(
        _combine_kernel,
        grid=(nt,),
        in_specs=[pl.BlockSpec((tm, d), lambda i: (i, 0)),
                  pl.BlockSpec((tm * ROW_TILES, LANES), lambda i: (i, 0)),
                  pl.BlockSpec((tm * ROW_TILES, LANES), lambda i: (i + nt, 0)),
                  pl.BlockSpec((1, d), lambda i: (0, 0))],
        out_specs=[pl.BlockSpec((tm, d), lambda i: (i, 0)),
                   pl.BlockSpec((tm, d), lambda i: (i, 0))],
        out_shape=[jax.ShapeDtypeStruct((t, d), F32), jax.ShapeDtypeStruct((t, d), BF16)],
        compiler_params=_params("arbitrary"),
        name="combine_rows",
    )(h, og, og, gain.reshape(1, d))


def moe_prompt(h, ffn_gain, w_router, wg, wu, wd, layer, ple_gain):
    xn = norm_rows(h, ffn_gain, F32)
    r, _ = route_top2(h, ffn_gain, w_router, TM_BIG)
    experts = jnp.concatenate([r[:, 0], r[:, 1]]).astype(jnp.int32)
    gates = jnp.concatenate([r[:, 2], r[:, 3]])
    onehot = (experts[:, None] == jnp.arange(N_EXPERTS)[None, :]).astype(jnp.int32)
    rank = jnp.sum(jnp.cumsum(onehot, axis=0) * onehot, axis=1) - 1
    counts = jnp.sum(onehot, axis=0)
    tiles = (counts + TM_MOE - 1) // TM_MOE
    tile_end = jnp.cumsum(tiles)
    tile_start = tile_end - tiles
    dest = jnp.sum(onehot * tile_start[None, :], axis=1) * TM_MOE + rank
    token = jnp.tile(jnp.arange(T, dtype=jnp.int32), 2)
    src_token = jnp.zeros((R_PAD,), jnp.int32).at[dest].set(token)
    row_gate = jnp.zeros((R_PAD,), F32).at[dest].set(gates).reshape(R_PAD, 1)
    n_active = tile_end[-1:].astype(jnp.int32)
    tile_ids = jnp.minimum(jnp.arange(MOE_TILES, dtype=jnp.int32), n_active[0] - 1)
    tile_expert = jnp.sum((tile_ids[:, None] >= tile_end[None, :]).astype(jnp.int32), axis=1)
    xs = gather_rows(xn.reshape(T, ROW_TILES, LANES), src_token, n_active * TM_MOE)
    ne = N_EXPERTS
    o = swiglu_tiles(xs.reshape(R_PAD * ROW_TILES, LANES), wg.reshape((-1,) + wg.shape[2:]),
                     wu.reshape((-1,) + wu.shape[2:]), wd.reshape((-1,) + wd.shape[2:]),
                     tile_expert + layer * ne, n_active, tm=TM_MOE, tf=256, row_gate=row_gate)
    og = gather_rows(o.reshape(R_PAD, ROW_TILES, LANES), dest, jnp.full((1,), 2 * T, jnp.int32))
    return combine_rows(h, og.reshape(2 * T * ROW_TILES, LANES), ple_gain)


def _headnorm_kernel(q_ref, k_ref, qg_ref, kg_ref, qn_ref, kn_ref):
    def norm_pairs(x_ref, g_ref, o_ref, scale):
        lo = lax.broadcasted_iota(jnp.int32, (x_ref.shape[0], LANES), 1) < HEAD_DIM
        for j in range(x_ref.shape[1] // LANES):
            x = x_ref[:, j * LANES:(j + 1) * LANES]
            s = x * x
            s_lo = jnp.sum(jnp.where(lo, s, 0.0), axis=-1, keepdims=True)
            s_hi = jnp.sum(jnp.where(lo, 0.0, s), axis=-1, keepdims=True)
            ms = jnp.where(lo, s_lo, s_hi) * (1.0 / HEAD_DIM)
            y = x * lax.rsqrt(ms + NORM_EPS) * g_ref[...]
            o_ref[:, j * LANES:(j + 1) * LANES] = (y * scale).astype(o_ref.dtype)

    norm_pairs(q_ref, qg_ref, qn_ref, HEAD_DIM ** -0.5)
    norm_pairs(k_ref, kg_ref, kn_ref, 1.0)


def head_norm(qkv, q_gain, k_gain, q_dtype, tm):
    t = qkv.shape[0]
    return pl.pallas_call(
        _headnorm_kernel,
        grid=(t // tm,),
        in_specs=[pl.BlockSpec((tm, NQ), lambda i: (i, 0)),
                  pl.BlockSpec((tm, NKV), lambda i: (i, NQ // NKV)),
                  pl.BlockSpec((1, LANES), lambda i: (0, 0)),
                  pl.BlockSpec((1, LANES), lambda i: (0, 0))],
        out_specs=[pl.BlockSpec((tm, NQ), lambda i: (i, 0)),
                   pl.BlockSpec((tm, NKV), lambda i: (i, 0))],
        out_shape=[jax.ShapeDtypeStruct((t, NQ), q_dtype), jax.ShapeDtypeStruct((t, NKV), F32)],
        compiler_params=_params("arbitrary"),
        name="head_norm",
    )(qkv, qkv, jnp.tile(q_gain, 2).reshape(1, LANES), jnp.tile(k_gain, 2).reshape(1, LANES))


def _attn_kernel(slope_ref, sink_ref, q_ref, kp_ref, kc_ref, vp_ref, vc_ref, o_ref):
    rows = WINDOW
    nkeys = 2 * WINDOW
    rep = ATTN_HEADS // ATTN_KV_HEADS
    qi = lax.broadcasted_iota(jnp.int32, (rows, nkeys), 0)
    kj = lax.broadcasted_iota(jnp.int32, (rows, nkeys), 1)
    delta = WINDOW + qi - kj
    mask = (delta >= 0) & (delta <= WINDOW) & ((kj >= WINDOW) | (pl.program_id(1) > 0))
    deltaf = delta.astype(F32)
    lo_q = lax.broadcasted_iota(jnp.int32, (rows, LANES), 1) < HEAD_DIM

    for jp in range(ATTN_KV_HEADS // 2):
        ksl = slice(jp * LANES, (jp + 1) * LANES)
        k2 = jnp.concatenate([kp_ref[:, ksl], kc_ref[:, ksl]], axis=0).astype(BF16)
        v2 = jnp.concatenate([vp_ref[:, ksl], vc_ref[:, ksl]], axis=0)
        v2r = pltpu.roll(v2, HEAD_DIM, 1).astype(BF16)
        v2 = v2.astype(BF16)
        for qp in range(rep):
            col = (jp * rep + qp) * LANES
            q2 = q_ref[:, col:col + LANES]
            q2r = pltpu.roll(q2, HEAD_DIM, 1)
            g_half = qp // (rep // 2)
            outs = []
            for hh in range(2):
                h = (jp * rep + qp) * 2 + hh
                qsel = q2 if hh == g_half else q2r
                qm = jnp.where(lo_q if g_half == 0 else ~lo_q, qsel, jnp.zeros_like(qsel))
                s = _dot_nt(qm, k2) - slope_ref[h] * deltaf
                s = jnp.where(mask, s, -jnp.inf)
                sink = sink_ref[h]
                m = jnp.maximum(jnp.max(s, axis=-1, keepdims=True), sink)
                e = jnp.exp(s - m)
                probs = e / (jnp.sum(e, axis=-1, keepdims=True) + jnp.exp(sink - m))
                outs.append(_dot(probs.astype(BF16), v2 if hh == g_half else v2r))
            o_ref[:, col:col + LANES] = jnp.where(lo_q, outs[0], outs[1]).astype(o_ref.dtype)


def attention_prompt(hn, h, w_qkv, layer, q_gain, k_gain, sinks, w_o, next_gain):
    qkv = matmul_cols(hn, w_qkv, layer, NQ + 2 * NKV)
    qn, kn = head_norm(qkv, q_gain, k_gain, BF16, TM_MID)
    nblk = SEQ // WINDOW
    vcol = NQ // NKV + 1
    cur = lambda c: (lambda b, n: (b * nblk + n, c))
    prev = lambda c: (lambda b, n: (jnp.maximum(b * nblk + n - 1, 0), c))
    smem = pl.BlockSpec(memory_space=pltpu.SMEM)
    o = pl.pallas_call(
        _attn_kernel,
        grid=(BATCH, nblk),
        in_specs=[smem, smem,
                  pl.BlockSpec((WINDOW, NQ), cur(0)),
                  pl.BlockSpec((WINDOW, NKV), prev(0)),
                  pl.BlockSpec((WINDOW, NKV), cur(0)),
                  pl.BlockSpec((WINDOW, NKV), prev(vcol)),
                  pl.BlockSpec((WINDOW, NKV), cur(vcol))],
        out_specs=pl.BlockSpec((WINDOW, NQ), cur(0)),
        out_shape=jax.ShapeDtypeStruct((T, NQ), BF16),
        compiler_params=_params("arbitrary", "arbitrary"),
        name="window_attention",
    )(_alibi_slopes(), sinks, qn, kn, kn, qkv, qkv)
    h2, hn2 = proj_residual(o, w_o, layer, h, next_gain)
    kv_shape = (BATCH, WINDOW, ATTN_KV_HEADS, HEAD_DIM)
    new_k = kn.reshape(BATCH, SEQ, NKV)[:, -WINDOW:].reshape(kv_shape)
    new_v = qkv[:, NQ + NKV:].reshape(BATCH, SEQ, NKV)[:, -WINDOW:].reshape(kv_shape)
    return h2, hn2, (new_k, new_v)


SCONV_TILE = 512
SCONV_TAIL = SUBLANES


def _sconv_kernel(gb_ref, gc_ref, u_ref, gcp_ref, up_ref, w_ref, y_ref, tail_ref):
    rows = gb_ref.shape[0]
    cu = gc_ref[...] * u_ref[...]
    prev = jnp.where(pl.program_id(1) > 0, gcp_ref[...] * up_ref[...], 0.0)
    row = lax.broadcasted_iota(jnp.int32, cu.shape, 0)
    s1 = jnp.where(row == 0, prev[SCONV_TAIL - 1:], pltpu.roll(cu, 1, 0))
    s2 = jnp.where(row == 0, prev[SCONV_TAIL - 2:SCONV_TAIL - 1],
                   jnp.where(row == 1, prev[SCONV_TAIL - 1:], pltpu.roll(cu, 2, 0)))
    conv = w_ref[0:1] * s2 + w_ref[1:2] * s1 + w_ref[2:3] * cu
    y_ref[...] = (gb_ref[...] * conv).astype(y_ref.dtype)
    tail_ref[...] = cu[rows - SCONV_TAIL:]


def sconv_prompt(hn, h, w_in, conv_w, layer, w_out, next_gain):
    d = D_MODEL
    proj = matmul_cols(hn, w_in, layer, 3 * d)
    nt = SEQ // SCONV_TILE
    tpb = SCONV_TILE // SCONV_TAIL
    cur = lambda c: (lambda b, i: (b * nt + i, c))
    prev = lambda c: (lambda b, i: (jnp.maximum((b * nt + i) * tpb - 1, 0), c))
    y, tails = pl.pallas_call(
        _sconv_kernel,
        grid=(BATCH, nt),
        in_specs=[pl.BlockSpec((SCONV_TILE, d), cur(0)),
                  pl.BlockSpec((SCONV_TILE, d), cur(1)),
                  pl.BlockSpec((SCONV_TILE, d), cur(2)),
                  pl.BlockSpec((SCONV_TAIL, d), prev(1)),
                  pl.BlockSpec((SCONV_TAIL, d), prev(2)),
                  pl.BlockSpec((None, SCONV_WIDTH, d), lambda b, i: (layer, 0, 0))],
        out_specs=[pl.BlockSpec((SCONV_TILE, d), cur(0)),
                   pl.BlockSpec((None, SCONV_TAIL, d), lambda b, i: (b * nt + i, 0, 0))],
        out_shape=[jax.ShapeDtypeStruct((T, d), BF16),
                   jax.ShapeDtypeStruct((BATCH * nt, SCONV_TAIL, d), F32)],
        compiler_params=_params("arbitrary", "arbitrary"),
        name="sconv_prompt",
    )(proj, proj, proj, proj, proj, conv_w)
    h2, hn2 = proj_residual(y, w_out, layer, h, next_gain)
    new_state = tails.reshape(BATCH, nt, SCONV_TAIL, d)[:, -1, -(SCONV_WIDTH - 1):]
    return h2, hn2, new_state


XBC_CHUNK = 512
GROUP_W = D_INNER // SSM_GROUPS
HEADS_PER_GROUP = SSM_HEADS // SSM_GROUPS


def _ssd_kernel(z_ref, xc_ref, xp_ref, dt_ref, cw_ref, cb_ref, dtb_ref, alog_ref, dsk_ref, nw_ref,
                y_ref, hfin_ref, state_ref, xact_ref, acst_ref, yg_ref):
    c = pl.program_id(1)
    q = SSM_CHUNK

    @pl.when(c == 0)
    def _():
        state_ref[...] = jnp.zeros_like(state_ref)

    has_prev = c > 0
    row = lax.broadcasted_iota(jnp.int32, (q, XBC_CHUNK), 0)
    for j in range(SSM_CONV_DIM // XBC_CHUNK):
        sl = slice(j * XBC_CHUNK, (j + 1) * XBC_CHUNK)
        cur = xc_ref[:, sl]
        prv = jnp.where(has_prev, xp_ref[:, sl], 0.0)
        acc = None
        for k in range(SSM_CONV_WIDTH):
            sh = SSM_CONV_WIDTH - 1 - k
            if sh == 0:
                term = cur
            else:
                term = jnp.where(row < sh, pltpu.roll(prv, sh, 0), pltpu.roll(cur, sh, 0))
            term = cw_ref[k:k + 1, sl] * term
            acc = term if acc is None else acc + term
        xact_ref[:, sl] = jax.nn.silu(acc + cb_ref[:, sl])

    dt = _softplus(dt_ref[...] + dtb_ref[...])
    da = dt * (-jnp.exp(alog_ref[...]))
    tri = (lax.broadcasted_iota(jnp.int32, (q, q), 0) >= lax.broadcasted_iota(jnp.int32, (q, q), 1))
    a_cs = jnp.dot(tri.astype(F32), da, preferred_element_type=F32, precision=lax.Precision.HIGHEST)
    acst_ref[...] = a_cs.T
    lane = lax.broadcasted_iota(jnp.int32, (q, LANES), 1)

    def group(g, carry):
        goff = pl.multiple_of(g * GROUP_W, GROUP_W)
        xs_g = xact_ref[:, pl.ds(goff, GROUP_W)]
        bm = xact_ref[:, pl.ds(pl.multiple_of(D_INNER + g * D_STATE, D_STATE), D_STATE)]
        cm = xact_ref[:, pl.ds(pl.multiple_of(D_INNER + SSM_GROUPS * D_STATE + g * D_STATE, D_STATE), D_STATE)]
        cm_b = cm.astype(BF16)
        cb = _dot_nt(cm_b, bm.astype(BF16))
        for hh in range(HEADS_PER_GROUP):
            h = g * HEADS_PER_GROUP + hh
            sel = lane == h
            acs_col = jnp.sum(jnp.where(sel, a_cs, 0.0), axis=1, keepdims=True)
            dt_col = jnp.sum(jnp.where(sel, dt, 0.0), axis=1, keepdims=True)
            acs_row = acst_ref[pl.ds(h, 1), :]
            acs_last = acs_col[q - 1:q]
            xs = xs_g[:, hh * SSM_HEAD_DIM:(hh + 1) * SSM_HEAD_DIM]
            xdt = (xs * dt_col).astype(BF16)
            seg = acs_col - acs_row
            decay_in = jnp.where(tri, jnp.exp(jnp.where(tri, seg, 0.0)), 0.0)
            y = _dot((cb * decay_in).astype(BF16), xdt)
            h_in = state_ref[h]
            y = y + _dot_nt(cm_b, h_in.astype(BF16)) * jnp.exp(acs_col)
            decay_out = jnp.exp(acs_last - acs_col)
            st = _dot_tn(xdt, (bm * decay_out).astype(BF16))
            state_ref[h] = jnp.exp(acs_last) * h_in + st
            yg_ref[:, hh * SSM_HEAD_DIM:(hh + 1) * SSM_HEAD_DIM] = y
        y_g = yg_ref[...] + dsk_ref[:, pl.ds(goff, GROUP_W)] * xs_g
        gg = y_g * jax.nn.silu(z_ref[:, pl.ds(goff, GROUP_W)])
        gg = gg * lax.rsqrt(jnp.mean(gg * gg, axis=-1, keepdims=True) + NORM_EPS)
        y_ref[:, pl.ds(goff, GROUP_W)] = (gg * nw_ref[:, pl.ds(goff, GROUP_W)]).astype(y_ref.dtype)
        return carry

    lax.fori_loop(0, SSM_GROUPS, group, 0)

    @pl.when(c == pl.num_programs(1) - 1)
    def _():
        hfin_ref[...] = state_ref[...]


def _pad_heads(v):
    return jnp.pad(v, (0, LANES - SSM_HEADS)).reshape(1, LANES)


def ssd_prompt(hn, h, w_in, layer, conv_w, conv_b, dt_bias, a_log, d_skip, norm_w, w_out, next_gain):
    cd = SSM_CONV_DIM
    z = matmul_cols(hn, w_in, layer, D_INNER, col_off=0)
    xbc = matmul_cols(hn, w_in, layer, cd, col_off=D_INNER)
    w_dt = jnp.pad(w_in[layer, :, D_INNER + cd:], ((0, 0), (0, LANES - SSM_HEADS)))[None]
    dt_raw = matmul_cols(hn, w_dt, 0, LANES)
    dsk_full = jnp.repeat(d_skip, SSM_HEAD_DIM).reshape(1, D_INNER)
    nc = SEQ // SSM_CHUNK
    cur = lambda b, c: (b * nc + c, 0)
    whole = lambda b, c: (0, 0)
    y, h_fin = pl.pallas_call(
        _ssd_kernel,
        grid=(BATCH, nc),
        in_specs=[pl.BlockSpec((SSM_CHUNK, D_INNER), cur),
                  pl.BlockSpec((SSM_CHUNK, cd), cur),
                  pl.BlockSpec((SSM_CHUNK, cd), lambda b, c: (jnp.maximum(b * nc + c - 1, 0), 0)),
                  pl.BlockSpec((SSM_CHUNK, LANES), cur),
                  pl.BlockSpec((SSM_CONV_WIDTH, cd), whole),
                  pl.BlockSpec((1, cd), whole),
                  pl.BlockSpec((1, LANES), whole),
                  pl.BlockSpec((1, LANES), whole),
                  pl.BlockSpec((1, D_INNER), whole),
                  pl.BlockSpec((1, D_INNER), whole)],
        out_specs=[pl.BlockSpec((SSM_CHUNK, D_INNER), cur),
                   pl.BlockSpec((None, SSM_HEADS, SSM_HEAD_DIM, D_STATE), lambda b, c: (b, 0, 0, 0))],
        out_shape=[jax.ShapeDtypeStruct((T, D_INNER), BF16),
                   jax.ShapeDtypeStruct((BATCH, SSM_HEADS, SSM_HEAD_DIM, D_STATE), F32)],
        scratch_shapes=[pltpu.VMEM((SSM_HEADS, SSM_HEAD_DIM, D_STATE), F32),
                        pltpu.VMEM((SSM_CHUNK, cd), F32),
                        pltpu.VMEM((LANES, SSM_CHUNK), F32),
                        pltpu.VMEM((SSM_CHUNK, GROUP_W), F32)],
        compiler_params=_params("arbitrary", "arbitrary"),
        name="ssd_prompt",
    )(z, xbc, xbc, dt_raw, conv_w, conv_b.reshape(1, cd), _pad_heads(dt_bias), _pad_heads(a_log), dsk_full,
      norm_w.reshape(1, D_INNER))
    h2, hn2 = proj_residual(y, w_out, layer, h, next_gain)
    new_conv = xbc.reshape(BATCH, SEQ, cd)[:, -(SSM_CONV_WIDTH - 1):]
    return h2, hn2, (h_fin, new_conv)


def _split(x):
    hi = x.astype(BF16)
    lo = (x - hi.astype(F32)).astype(BF16)
    return hi, lo


def _hp_kernel(*refs, nk, x_mode, epi):
    refs = list(refs)
    if x_mode == "norm":
        x_ref, g_ref = refs[:2]
        refs = refs[2:]
        x = _rms(x_ref[...], g_ref[...])
    elif x_mode == "swiglu":
        g_ref, u_ref = refs[:2]
        refs = refs[2:]
        x = jax.nn.silu(g_ref[...]) * u_ref[...]
    else:
        x = refs[0][...]
        refs = refs[1:]
    w_ref = refs[0]
    refs = refs[1:]
    if epi == "gated":
        res_ref, pp_ref, o_ref = refs
    elif epi == "res":
        res_ref, o_ref = refs
    else:
        (o_ref,) = refs
    k = pl.program_id(2)
    rows = x.shape[0]
    x_hi, x_lo = _split(x)
    w_hi, w_lo = _split(w_ref[...])
    both = _dot(jnp.concatenate([x_hi, x_lo], axis=0), w_hi)
    part = both[:rows] + both[rows:] + _dot(x_hi, w_lo)

    @pl.when(k == 0)
    def _():
        o_ref[...] = part

    @pl.when(k > 0)
    def _():
        o_ref[...] += part

    if epi is not None:
        @pl.when(k == nk - 1)
        def _():
            acc = o_ref[...]
            if epi == "gated":
                acc = jax.nn.sigmoid(acc) * pp_ref[...]
            o_ref[...] = res_ref[...] + acc


def hp_linear(w, widx, n_out, *, x=None, norm=None, swiglu=None, col_off=0, n_e=1, tk, tn, res=None, pp=None):
    kdim = w.shape[1]
    nk = kdim // tk
    off = col_off // tn
    assert kdim % tk == 0 and n_out % tn == 0 and col_off % tn == 0
    if norm is not None:
        assert nk == 1
        x_mode = "norm"
        args = [norm[0], norm[1].reshape(1, kdim)]
        in_specs = [pl.BlockSpec((NS, tk), lambda e, n, k: (0, k)),
                    pl.BlockSpec((1, tk), lambda e, n, k: (0, k))]
    elif swiglu is not None:
        x_mode = "swiglu"
        args = list(swiglu)
        in_specs = [pl.BlockSpec((None, NS, tk), lambda e, n, k: (e, 0, k))] * 2
    else:
        x_mode = "plain"
        args = [x]
        in_specs = [pl.BlockSpec((NS, tk), lambda e, n, k: (0, k))]
    args.append(w)
    in_specs.append(pl.BlockSpec((None, tk, tn), lambda e, n, k: (widx + e, k, n + off)))
    epi = None
    if res is not None:
        epi = "res"
        args.append(res)
        in_specs.append(pl.BlockSpec((NS, tn), lambda e, n, k: (0, n)))
        if pp is not None:
            epi = "gated"
            args.append(pp)
            in_specs.append(pl.BlockSpec((NS, tn), lambda e, n, k: (0, n)))
    return pl.pallas_call(
        functools.partial(_hp_kernel, nk=nk, x_mode=x_mode, epi=epi),
        grid=(n_e, n_out // tn, nk),
        in_specs=in_specs,
        out_specs=pl.BlockSpec((None, NS, tn), lambda e, n, k: (e, 0, n)),
        out_shape=jax.ShapeDtypeStruct((n_e, NS, n_out), F32),
        compiler_params=_params("arbitrary", "arbitrary", "arbitrary"),
        name="hp_linear_" + x_mode,
    )(*args)


def _attn_step_kernel(slope_ref, sink_ref, q_ref, kc_ref, vc_ref, kn_ref, vn_ref, o_ref):
    rep = ATTN_HEADS // ATTN_KV_HEADS
    lo = lax.broadcasted_iota(jnp.int32, (1, LANES), 1) < HEAD_DIM
    delta = (WINDOW - lax.broadcasted_iota(jnp.int32, (WINDOW, 1), 0)).astype(F32)
    for jp in range(ATTN_KV_HEADS // 2):
        ksl = slice(jp * LANES, (jp + 1) * LANES)
        k2 = kc_ref[:, ksl]
        v2 = vc_ref[:, ksl]
        kn2 = kn_ref[:, ksl]
        vn2 = vn_ref[:, ksl]
        for qp in range(rep):
            col = (jp * rep + qp) * LANES
            q2 = q_ref[:, col:col + LANES]
            q2r = pltpu.roll(q2, HEAD_DIM, 1)
            g_half = qp // (rep // 2)
            kv_lanes = lo if g_half == 0 else ~lo
            out = jnp.zeros((1, LANES), F32)
            for hh in range(2):
                h = (jp * rep + qp) * 2 + hh
                qm = jnp.where(kv_lanes, q2 if hh == g_half else q2r, 0.0)
                s = jnp.sum(k2 * qm, axis=1, keepdims=True) - slope_ref[h] * delta
                s_new = jnp.sum(kn2 * qm, axis=1, keepdims=True)
                sink = sink_ref[h]
                m = jnp.maximum(jnp.maximum(jnp.max(s, axis=0, keepdims=True), s_new), sink)
                e = jnp.exp(s - m)
                e_new = jnp.exp(s_new - m)
                denom = jnp.sum(e, axis=0, keepdims=True) + e_new + jnp.exp(sink - m)
                o = jnp.sum((e / denom) * v2, axis=0, keepdims=True) + (e_new / denom) * vn2
                if hh != g_half:
                    o = pltpu.roll(o, HEAD_DIM, 1)
                out = jnp.where(lo if hh == 0 else ~lo, o, out)
            o_ref[:, col:col + LANES] = out


def _sconv_step_kernel(proj_ref, p_ref, w_ref, y_ref, cu_ref):
    d = D_MODEL
    cu = proj_ref[:, d:2 * d] * proj_ref[:, 2 * d:]
    conv = w_ref[0:1] * p_ref[0] + w_ref[1:2] * p_ref[1] + w_ref[2:3] * cu
    y_ref[...] = proj_ref[:, :d] * conv
    cu_ref[...] = cu


def _ssd_step_pre_kernel(xc_ref, p_ref, dt_ref, cw_ref, cb_ref, dtb_ref, alog_ref, xact_ref, dt_out_ref, dec_ref):
    acc = cw_ref[0:1] * p_ref[0] + cw_ref[1:2] * p_ref[1] + cw_ref[2:3] * p_ref[2] + cw_ref[3:4] * xc_ref[...]
    xact_ref[...] = jax.nn.silu(acc + cb_ref[...])
    dt = _softplus(dt_ref[...] + dtb_ref[...])
    dt_out_ref[...] = dt
    dec_ref[...] = jnp.exp(dt * (-jnp.exp(alog_ref[...])))


def _ssd_step_kernel(dt_ref, dec_ref, dsk_ref, xt_ref, bm_ref, cm_ref, h0_ref, yt_ref, h1_ref):
    b = pl.program_id(0)
    xt = xt_ref[...]
    lane = lax.broadcasted_iota(jnp.int32, xt.shape, 1)

    def head(h, yt):
        g = h // HEADS_PER_GROUP
        sel = lane == h
        x_col = jnp.sum(jnp.where(sel, xt, 0.0), axis=1, keepdims=True)
        bm = bm_ref[pl.ds(g, 1), :]
        cm = cm_ref[pl.ds(g, 1), :]
        h1 = dec_ref[b, h] * h0_ref[h] + (dt_ref[b, h] * x_col) * bm
        h1_ref[h] = h1
        y_col = jnp.sum(h1 * cm, axis=1, keepdims=True) + dsk_ref[h] * x_col
        return jnp.where(sel, y_col, yt)

    yt_ref[...] = lax.fori_loop(0, SSM_HEADS, head, jnp.zeros(xt.shape, F32))


def _ssd_gate_kernel(y_ref, z_ref, nw_ref, o_ref):
    for g in range(SSM_GROUPS):
        sl = slice(g * GROUP_W, (g + 1) * GROUP_W)
        gg = y_ref[:, sl] * jax.nn.silu(z_ref[:, sl])
        gg = gg * lax.rsqrt(jnp.mean(gg * gg, axis=-1, keepdims=True) + NORM_EPS)
        o_ref[:, sl] = gg * nw_ref[:, sl]


def _moe_mix_kernel(h_ref, gates_ref, d_ref, o_ref):
    lane = lax.broadcasted_iota(jnp.int32, gates_ref.shape, 1)
    y = jnp.zeros(h_ref.shape, F32)
    for e in range(N_EXPERTS):
        gate = jnp.sum(jnp.where(lane == e, gates_ref[...], 0.0), axis=1, keepdims=True)
        y = y + gate * d_ref[e]
    o_ref[...] = h_ref[...] + y


def _whole(shape):
    return pl.BlockSpec(shape, lambda *_: (0,) * len(shape))


def _small_call(body, out_shapes, *args, name):
    multi = isinstance(out_shapes, (list, tuple))
    outs = list(out_shapes) if multi else [out_shapes]
    res = pl.pallas_call(
        body,
        grid=(1,),
        in_specs=[_whole(a.shape) for a in args],
        out_specs=[_whole(o.shape) for o in outs],
        out_shape=outs,
        compiler_params=_params("arbitrary"),
        name=name,
    )(*args)
    return res if multi else res[0]


def _sds(*shape):
    return jax.ShapeDtypeStruct(shape, F32)


def attention_sample(h, norm_gain, cache_k, cache_v, w_qkv, layer, q_gain, k_gain, sinks, w_o):
    qkv = hp_linear(w_qkv, layer, NQ + 2 * NKV, norm=(h, norm_gain), tk=D_MODEL, tn=512)[0]
    qn, kn = head_norm(qkv, q_gain, k_gain, F32, NS)
    v_new = qkv[:, NQ + NKV:]
    smem = pl.BlockSpec(memory_space=pltpu.SMEM)
    rowsq = pl.BlockSpec((None, 1, NQ), lambda b: (b, 0, 0))
    rowsk = pl.BlockSpec((None, 1, NKV), lambda b: (b, 0, 0))
    cache = pl.BlockSpec((None, WINDOW, NKV), lambda b: (b, 0, 0))
    o = pl.pallas_call(
        _attn_step_kernel,
        grid=(NS,),
        in_specs=[smem, smem, rowsq, cache, cache, rowsk, rowsk],
        out_specs=rowsq,
        out_shape=_sds(NS, 1, NQ),
        compiler_params=_params("arbitrary"),
        name="attention_sample",
    )(_alibi_slopes(), sinks, qn.reshape(NS, 1, NQ), cache_k.reshape(NS, WINDOW, NKV),
      cache_v.reshape(NS, WINDOW, NKV), kn.reshape(NS, 1, NKV), v_new.reshape(NS, 1, NKV))
    h2 = hp_linear(w_o, layer, D_MODEL, x=o.reshape(NS, NQ), tk=1024, tn=1024, res=h)[0]
    kv_row = (NS, 1, ATTN_KV_HEADS, HEAD_DIM)
    new_k = jnp.concatenate([cache_k[:, 1:], kn.reshape(kv_row)], axis=1)
    new_v = jnp.concatenate([cache_v[:, 1:], v_new.reshape(kv_row)], axis=1)
    return h2, (new_k, new_v)


def sconv_sample(h, norm_gain, past, w_in, conv_w, layer, w_out):
    d = D_MODEL
    proj = hp_linear(w_in, layer, 3 * d, norm=(h, norm_gain), tk=d, tn=512)[0]
    y, cu = _small_call(_sconv_step_kernel, [_sds(NS, d), _sds(NS, d)],
                        proj, past.transpose(1, 0, 2), conv_w[layer], name="sconv_sample")
    h2 = hp_linear(w_out, layer, d, x=y, tk=1024, tn=1024, res=h)[0]
    return h2, jnp.concatenate([past[:, 1:], cu[:, None]], axis=1)


def ssd_sample(h, norm_gain, conv_past, h0, w_in, layer, conv_w, conv_b, dt_bias, a_log, d_skip, norm_w, w_out):
    cd = SSM_CONV_DIM
    z = hp_linear(w_in, layer, D_INNER, norm=(h, norm_gain), tk=D_MODEL, tn=512)[0]
    xbc = hp_linear(w_in, layer, cd, norm=(h, norm_gain), col_off=D_INNER, tk=D_MODEL, tn=512)[0]
    w_dt = jnp.pad(w_in[layer, :, D_INNER + cd:], ((0, 0), (0, LANES - SSM_HEADS)))[None]
    dt_raw = hp_linear(w_dt, 0, LANES, norm=(h, norm_gain), tk=D_MODEL, tn=LANES)[0]
    xact, dt, dec = _small_call(
        _ssd_step_pre_kernel, [_sds(NS, cd), _sds(NS, LANES), _sds(NS, LANES)],
        xbc, conv_past.transpose(1, 0, 2), dt_raw, conv_w, conv_b.reshape(1, cd), _pad_heads(dt_bias),
        _pad_heads(a_log), name="ssd_sample_pre")
    xt = xact[:, :D_INNER].reshape(NS, SSM_HEADS, SSM_HEAD_DIM).transpose(0, 2, 1)
    bm = xact[:, D_INNER:D_INNER + SSM_GROUPS * D_STATE].reshape(NS, SSM_GROUPS, D_STATE)
    cm = xact[:, D_INNER + SSM_GROUPS * D_STATE:].reshape(NS, SSM_GROUPS, D_STATE)
    smem = pl.BlockSpec(memory_space=pltpu.SMEM)
    per_b = lambda *shape: pl.BlockSpec((None,) + shape, lambda b: (b,) + (0,) * len(shape))
    yt, h1 = pl.pallas_call(
        _ssd_step_kernel,
        grid=(NS,),
        in_specs=[smem, smem, smem,
                  per_b(SSM_HEAD_DIM, SSM_HEADS), per_b(SSM_GROUPS, D_STATE), per_b(SSM_GROUPS, D_STATE),
                  per_b(SSM_HEADS, SSM_HEAD_DIM, D_STATE)],
        out_specs=[per_b(SSM_HEAD_DIM, SSM_HEADS), per_b(SSM_HEADS, SSM_HEAD_DIM, D_STATE)],
        out_shape=[_sds(NS, SSM_HEAD_DIM, SSM_HEADS), _sds(NS, SSM_HEADS, SSM_HEAD_DIM, D_STATE)],
        compiler_params=_params("arbitrary"),
        name="ssd_sample_step",
    )(dt[:, :SSM_HEADS], dec[:, :SSM_HEADS], d_skip, xt, bm, cm, h0)
    y = yt.transpose(0, 2, 1).reshape(NS, D_INNER)
    yn = _small_call(_ssd_gate_kernel, _sds(NS, D_INNER), y, z, norm_w.reshape(1, D_INNER), name="ssd_sample_gate")
    h2 = hp_linear(w_out, layer, D_MODEL, x=yn, tk=1024, tn=1024, res=h)[0]
    return h2, (h1, jnp.concatenate([conv_past[:, 1:], xbc[:, None]], axis=1))


def ffn_sample(h, norm_gain, wg, wu, wd, widx, n_e):
    g = hp_linear(wg, widx, D_FF, norm=(h, norm_gain), n_e=n_e, tk=D_MODEL, tn=512)
    u = hp_linear(wu, widx, D_FF, norm=(h, norm_gain), n_e=n_e, tk=D_MODEL, tn=512)
    return g, u


def ple_sample(h, p, ple_gain, w_gate, w_proj, layer):
    pp = hp_linear(w_proj, layer, D_MODEL, x=p, tk=PLE_DIM, tn=D_MODEL)[0]
    return hp_linear(w_gate, layer, D_MODEL, norm=(h, ple_gain), tk=D_MODEL, tn=512, res=h, pp=pp)[0]


def kernel(x_prompt, x_sample, cache_k_win, cache_v_win, state_short_conv, state_ssm, state_ssm_conv, p_prompt, p_sample, norm_mixer, norm_ffn, attn_w_qkv, attn_q_norm, attn_k_norm, attn_sinks, attn_w_out, sconv_w_in, sconv_conv_w, sconv_w_out, ssm_w_in, ssm_conv_w, ssm_conv_b, ssm_dt_bias, ssm_a_log, ssm_d_skip, ssm_norm, ssm_w_out, ffn_w_gate, ffn_w_up, ffn_w_down, moe_w_router, moe_w_gate, moe_w_up, moe_w_down, ple_norm, ple_w_gate, ple_w_proj):
    flat_e = lambda w: w.reshape((-1,) + w.shape[2:])
    moe_wg, moe_wu, moe_wd = flat_e(moe_w_gate), flat_e(moe_w_up), flat_e(moe_w_down)
    dense_tiles = T // TM_BIG
    dense_active = jnp.full((1,), dense_tiles, jnp.int32)

    h = x_prompt.reshape(T, D_MODEL)
    hn = norm_rows(h, norm_mixer[0], BF16)
    k_p, v_p, sc_p, ssm_p, ssmc_p = [], [], [], [], []
    for i in range(DEPTH):
        kind, j = i % 3, i // 3
        if kind == 0:
            h, hn, new = attention_prompt(hn, h, attn_w_qkv, j, attn_q_norm[j], attn_k_norm[j], attn_sinks[j],
                                          attn_w_out, norm_ffn[i])
            k_p.append(new[0]); v_p.append(new[1])
        elif kind == 1:
            h, hn, new = sconv_prompt(hn, h, sconv_w_in, sconv_conv_w, j, sconv_w_out, norm_ffn[i])
            sc_p.append(new)
        else:
            h, hn, new = ssd_prompt(hn, h, ssm_w_in, j, ssm_conv_w[j], ssm_conv_b[j], ssm_dt_bias[j],
                                    ssm_a_log[j], ssm_d_skip[j], ssm_norm[j], ssm_w_out, norm_ffn[i])
            ssm_p.append(new[0]); ssmc_p.append(new[1])
        c = i // 2
        if i % 2 == 0:
            h, hn = swiglu_tiles(hn, ffn_w_gate, ffn_w_up, ffn_w_down, jnp.full((dense_tiles,), c, jnp.int32),
                                 dense_active, tm=TM_BIG, tf=256, res=h, gain=ple_norm[i])
        else:
            h, hn = moe_prompt(h, norm_ffn[i], moe_w_router[c], moe_w_gate, moe_w_up, moe_w_down, c, ple_norm[i])
        next_gain = norm_mixer[i + 1] if i + 1 < DEPTH else None
        h, hn = proj_residual(hn, ple_w_gate, i, h, next_gain, p=p_prompt[i].reshape(T, PLE_DIM), wp=ple_w_proj)
    y_p = h.reshape(BATCH, SEQ, D_MODEL)

    s = x_sample.reshape(NS, D_MODEL)
    k_s, v_s, sc_s, ssm_s, ssmc_s = [], [], [], [], []
    for i in range(DEPTH):
        kind, j = i % 3, i // 3
        if kind == 0:
            s, new = attention_sample(s, norm_mixer[i], cache_k_win[j], cache_v_win[j], attn_w_qkv, j,
                                      attn_q_norm[j], attn_k_norm[j], attn_sinks[j], attn_w_out)
            k_s.append(new[0]); v_s.append(new[1])
        elif kind == 1:
            s, new = sconv_sample(s, norm_mixer[i], state_short_conv[j], sconv_w_in, sconv_conv_w, j, sconv_w_out)
            sc_s.append(new)
        else:
            s, new = ssd_sample(s, norm_mixer[i], state_ssm_conv[j], state_ssm[j], ssm_w_in, j, ssm_conv_w[j],
                                ssm_conv_b[j], ssm_dt_bias[j], ssm_a_log[j], ssm_d_skip[j], ssm_norm[j], ssm_w_out)
            ssm_s.append(new[0]); ssmc_s.append(new[1])
        c = i // 2
        if i % 2 == 0:
            g, u = ffn_sample(s, norm_ffn[i], ffn_w_gate, ffn_w_up, ffn_w_down, c, 1)
            s = hp_linear(ffn_w_down, c, D_MODEL, swiglu=(g, u), tk=1024, tn=1024, res=s)[0]
        else:
            _, gates = route_top2(s, norm_ffn[i], moe_w_router[c], NS)
            g, u = ffn_sample(s, norm_ffn[i], moe_wg, moe_wu, moe_wd, c * N_EXPERTS, N_EXPERTS)
            d_e = hp_linear(moe_wd, c * N_EXPERTS, D_MODEL, swiglu=(g, u), n_e=N_EXPERTS, tk=1024, tn=1024)
            s = _small_call(_moe_mix_kernel, _sds(NS, D_MODEL), s, gates, d_e, name="moe_mix_sample")
        s = ple_sample(s, p_sample[i].reshape(NS, PLE_DIM), ple_norm[i], ple_w_gate, ple_w_proj, i)
    y_s = s.reshape(NS, 1, D_MODEL)

    return (y_p, y_s, jnp.stack(k_p), jnp.stack(v_p), jnp.stack(k_s), jnp.stack(v_s),
            jnp.stack(sc_p), jnp.stack(sc_s), jnp.stack(ssm_p), jnp.stack(ssm_s),
            jnp.stack(ssmc_p), jnp.stack(ssmc_s))
```

```python
import functools

import jax
import jax.numpy as jnp
from jax import lax
from jax.experimental import pallas as pl
from jax.experimental.pallas import tpu as pltpu

F32 = jnp.float32
BF16 = jnp.bfloat16

D_MODEL = 2048
BATCH = 2
SEQ = 4096
DEPTH = 4
DEC_BATCH = 32
PAST_LEN = 16384
ATTN_HEADS = 32
ATTN_KV_HEADS = 8
HEAD_DIM = 64
WINDOW = 128
SCONV_WIDTH = 3
D_INNER = 2 * D_MODEL
SSM_HEAD_DIM = 64
SSM_HEADS = D_INNER // SSM_HEAD_DIM
SSM_GROUPS = 8
D_STATE = 128
SSM_CONV_WIDTH = 4
SSM_CHUNK = 128
SSM_CONV_DIM = D_INNER + 2 * SSM_GROUPS * D_STATE
D_FF = 7 * D_MODEL // 2
N_EXPERTS = 8
PLE_DIM = 256
NORM_EPS = 1e-6
NQ = ATTN_HEADS * HEAD_DIM
NKV = ATTN_KV_HEADS * HEAD_DIM
assert PAST_LEN >= WINDOW

LANES = 128
SUBLANES = 8
T = BATCH * SEQ
NS = DEC_BATCH
TM_BIG = 1024
TM_MID = 512
TM_MOE = 1024
MOE_TILES = (2 * T + N_EXPERTS * (TM_MOE - 1)) // TM_MOE
R_PAD = MOE_TILES * TM_MOE
ROW_TILES = D_MODEL // LANES
VMEM_LIMIT = 56 * 1024 * 1024


def _params(*sem):
    return pltpu.CompilerParams(dimension_semantics=sem, vmem_limit_bytes=VMEM_LIMIT)


def _rms(x, g):
    return x * lax.rsqrt(jnp.mean(x * x, axis=-1, keepdims=True) + NORM_EPS) * g


def _dot(a, b):
    return jnp.dot(a, b, preferred_element_type=F32)


def _dot_nt(a, b):
    return lax.dot_general(a, b, (((1,), (1,)), ((), ())), preferred_element_type=F32)


def _dot_tn(a, b):
    return lax.dot_general(a, b, (((0,), (0,)), ((), ())), preferred_element_type=F32)


def _softplus(x):
    return jnp.maximum(x, 0.0) + jnp.log1p(jnp.exp(-jnp.abs(x)))


def _alibi_slopes():
    return jnp.exp2(-8.0 * jnp.arange(1, ATTN_HEADS + 1, dtype=F32) / ATTN_HEADS)


def _slab(tiled_ref, s):
    return (pl.ds(s, tiled_ref.shape[0] // ROW_TILES, stride=ROW_TILES), slice(None))


def _rows_from_tiles(tiled_ref, rows_ref):
    for s in range(ROW_TILES):
        rows_ref[:, s * LANES:(s + 1) * LANES] = tiled_ref[_slab(tiled_ref, s)].astype(rows_ref.dtype)


def _norm_kernel(x_ref, g_ref, o_ref, *, tiled):
    y = _rms(x_ref[...], g_ref[...]).astype(o_ref.dtype)
    if tiled:
        for s in range(ROW_TILES):
            o_ref[_slab(o_ref, s)] = y[:, s * LANES:(s + 1) * LANES]
    else:
        o_ref[...] = y


def norm_rows(x, gain, dtype, tiled=False, tm=TM_BIG):
    t, d = x.shape
    out_block, out_full = ((tm * ROW_TILES, LANES), (t * ROW_TILES, LANES)) if tiled else ((tm, d), (t, d))
    return pl.pallas_call(
        functools.partial(_norm_kernel, tiled=tiled),
        grid=(t // tm,),
        in_specs=[pl.BlockSpec((tm, d), lambda i: (i, 0)),
                  pl.BlockSpec((1, d), lambda i: (0, 0))],
        out_specs=pl.BlockSpec(out_block, lambda i: (i, 0)),
        out_shape=jax.ShapeDtypeStruct(out_full, dtype),
        compiler_params=_params("arbitrary"),
        name="norm_rows",
    )(x, gain.reshape(1, d))


def _mm_kernel(x_ref, w_ref, o_ref, wb_ref):
    @pl.when(pl.program_id(1) == 0)
    def _():
        wb_ref[...] = w_ref[...].astype(BF16)

    o_ref[...] = _dot(x_ref[...], wb_ref[...])


def matmul_cols(x, w, layer, n_out, col_off=0, tn=512, tm=TM_BIG):
    t, k = x.shape
    tn = min(tn, n_out)
    off = col_off // tn
    assert col_off % tn == 0 and n_out % tn == 0
    return pl.pallas_call(
        _mm_kernel,
        grid=(n_out // tn, t // tm),
        in_specs=[pl.BlockSpec((tm, k), lambda n, m: (m, 0)),
                  pl.BlockSpec((None, k, tn), lambda n, m: (layer, 0, n + off))],
        out_specs=pl.BlockSpec((tm, tn), lambda n, m: (m, n)),
        out_shape=jax.ShapeDtypeStruct((t, n_out), F32),
        scratch_shapes=[pltpu.VMEM((k, tn), BF16)],
        compiler_params=_params("arbitrary", "arbitrary"),
        name="matmul_cols",
    )(x, w)


def _proj_kernel(*refs, nk, gated, with_norm):
    x_ref, w_ref, res_ref = refs[:3]
    refs = refs[3:]
    if gated:
        p_ref, wp_ref = refs[:2]
        refs = refs[2:]
    if with_norm:
        g_ref, h_ref, hn_ref = refs
    else:
        (h_ref,) = refs
    k = pl.program_id(1)
    part = _dot(x_ref[...], w_ref[...].astype(BF16))

    @pl.when(k == 0)
    def _():
        h_ref[...] = part

    @pl.when(k > 0)
    def _():
        h_ref[...] += part

    @pl.when(k == nk - 1)
    def _():
        upd = h_ref[...]
        if gated:
            upd = jax.nn.sigmoid(upd) * _dot(p_ref[...].astype(BF16), wp_ref[...].astype(BF16))
        h_new = res_ref[...] + upd
        h_ref[...] = h_new
        if with_norm:
            hn_ref[...] = _rms(h_new, g_ref[...]).astype(BF16)


def proj_residual(x, w, layer, res, gain=None, p=None, wp=None, tm=TM_MID, tk=512):
    t, kdim = x.shape
    d = w.shape[2]
    nk = kdim // tk
    gated = p is not None
    with_norm = gain is not None
    in_specs = [pl.BlockSpec((tm, tk), lambda m, k: (m, k)),
                pl.BlockSpec((None, tk, d), lambda m, k: (layer, k, 0)),
                pl.BlockSpec((tm, d), lambda m, k: (m, 0))]
    args = [x, w, res]
    if gated:
        in_specs += [pl.BlockSpec((tm, p.shape[1]), lambda m, k: (m, 0)),
                     pl.BlockSpec((None,) + wp.shape[1:], lambda m, k: (layer, 0, 0))]
        args += [p, wp]
    out_specs = [pl.BlockSpec((tm, d), lambda m, k: (m, 0))]
    out_shape = [jax.ShapeDtypeStruct((t, d), F32)]
    if with_norm:
        in_specs.append(pl.BlockSpec((1, d), lambda m, k: (0, 0)))
        args.append(gain.reshape(1, d))
        out_specs.append(pl.BlockSpec((tm, d), lambda m, k: (m, 0)))
        out_shape.append(jax.ShapeDtypeStruct((t, d), BF16))
    out = pl.pallas_call(
        functools.partial(_proj_kernel, nk=nk, gated=gated, with_norm=with_norm),
        grid=(t // tm, nk),
        in_specs=in_specs,
        out_specs=out_specs,
        out_shape=out_shape,
        compiler_params=_params("arbitrary", "arbitrary"),
        name="proj_gated" if gated else "proj_residual",
    )(*args)
    return (out[0], out[1]) if with_norm else (out[0], None)


FFN_DOWN_COLS = 512


def _ffn_kernel(te_ref, na_ref, x_ref, wg_ref, wu_ref, wd_ref, *refs, nf, routed):
    if routed:
        gate_ref, out_ref, xb_ref = refs
    else:
        res_ref, g_ref, out_ref, hn_ref = refs
    i = pl.program_id(0)
    f = pl.program_id(1)

    @pl.when(i < na_ref[0])
    def _():
        if routed:
            @pl.when(f == 0)
            def _():
                _rows_from_tiles(x_ref, xb_ref)
            x = xb_ref[...]
        else:
            x = x_ref[...]
        g = _dot(x, wg_ref[...].astype(BF16))
        u = _dot(x, wu_ref[...].astype(BF16))
        a = (jax.nn.silu(g) * u).astype(BF16)
        for n in range(wd_ref.shape[1] // FFN_DOWN_COLS):
            sl = slice(n * FFN_DOWN_COLS, (n + 1) * FFN_DOWN_COLS)
            part = _dot(a, wd_ref[:, sl].astype(BF16))
            if routed:
                for j in range(FFN_DOWN_COLS // LANES):
                    slab = _slab(out_ref, n * (FFN_DOWN_COLS // LANES) + j)
                    piece = part[:, j * LANES:(j + 1) * LANES]

                    @pl.when(f == 0)
                    def _():
                        out_ref[slab] = piece

                    @pl.when((f > 0) & (f < nf - 1))
                    def _():
                        out_ref[slab] += piece

                    @pl.when(f == nf - 1)
                    def _():
                        out_ref[slab] = gate_ref[...] * (out_ref[slab] + piece)
            else:
                @pl.when(f == 0)
                def _():
                    out_ref[:, sl] = part

                @pl.when(f > 0)
                def _():
                    out_ref[:, sl] += part

        if not routed:
            @pl.when(f == nf - 1)
            def _():
                h_new = res_ref[...] + out_ref[...]
                out_ref[...] = h_new
                hn_ref[...] = _rms(h_new, g_ref[...]).astype(BF16)


def swiglu_tiles(x, wg, wu, wd, tile_expert, n_active, *, tm, tf, res=None, gain=None, row_gate=None):
    routed = row_gate is not None
    r = x.shape[0] // ROW_TILES if routed else x.shape[0]
    d = wd.shape[2]
    ff = wg.shape[2]
    nf = ff // tf

    def row(i, f, te, na):
        return (jnp.minimum(i, na[0] - 1), 0)

    def fcol(i, f, na):
        return jnp.where(i < na[0], f, nf - 1)

    single = pl.Buffered(1)
    w_specs = [pl.BlockSpec((None, d, tf), lambda i, f, te, na: (te[i], 0, fcol(i, f, na))),
               pl.BlockSpec((None, d, tf), lambda i, f, te, na: (te[i], 0, fcol(i, f, na))),
               pl.BlockSpec((None, tf, d), lambda i, f, te, na: (te[i], fcol(i, f, na), 0))]
    if routed:
        in_specs = [pl.BlockSpec((tm * ROW_TILES, LANES), row)]
        in_specs += w_specs + [pl.BlockSpec((tm, 1), row)]
        args = [x, wg, wu, wd, row_gate]
        assert nf > 2
        out_specs = pl.BlockSpec((tm * ROW_TILES, LANES), row, pipeline_mode=single)
        out_shape = jax.ShapeDtypeStruct((r * ROW_TILES, LANES), F32)
        scratch = [pltpu.VMEM((tm, d), BF16)]
    else:
        in_specs = [pl.BlockSpec((tm, d), row)] + w_specs
        in_specs += [pl.BlockSpec((tm, d), row, pipeline_mode=single),
                     pl.BlockSpec((1, d), lambda i, f, te, na: (0, 0))]
        args = [x, wg, wu, wd, res, gain.reshape(1, d)]
        out_specs = [pl.BlockSpec((tm, d), row, pipeline_mode=single),
                     pl.BlockSpec((tm, d), row, pipeline_mode=single)]
        out_shape = [jax.ShapeDtypeStruct((r, d), F32), jax.ShapeDtypeStruct((r, d), BF16)]
        scratch = []
    return pl.pallas_call(
        functools.partial(_ffn_kernel, nf=nf, routed=routed),
        grid_spec=pltpu.PrefetchScalarGridSpec(
            num_scalar_prefetch=2,
            grid=(r // tm, nf),
            in_specs=in_specs,
            out_specs=out_specs,
            scratch_shapes=scratch),
        out_shape=out_shape,
        compiler_params=_params("arbitrary", "arbitrary"),
        name="swiglu_routed" if routed else "swiglu_dense",
    )(tile_expert, n_active, *args)


def _router_kernel(h_ref, g_ref, w_ref, o_ref, dense_ref):
    xn = _rms(h_ref[...], g_ref[...])
    logits = jnp.dot(xn, w_ref[...], preferred_element_type=F32, precision=lax.Precision.HIGHEST)
    lane = lax.broadcasted_iota(jnp.int32, logits.shape, 1)
    logits = jnp.where(lane < N_EXPERTS, logits, -jnp.inf)
    e = jnp.exp(logits - jnp.max(logits, axis=-1, keepdims=True))
    probs = e / jnp.sum(e, axis=-1, keepdims=True)
    p1 = jnp.max(probs, axis=-1, keepdims=True)
    e1 = jnp.min(jnp.where(probs == p1, lane, LANES), axis=-1, keepdims=True)
    rest = jnp.where(lane == e1, -1.0, probs)
    p2 = jnp.max(rest, axis=-1, keepdims=True)
    e2 = jnp.min(jnp.where(rest == p2, lane, LANES), axis=-1, keepdims=True)
    tot = p1 + p2
    g1 = p1 / tot
    g2 = p2 / tot
    o_ref[...] = jnp.where(lane == 0, e1.astype(F32),
                           jnp.where(lane == 1, e2.astype(F32), jnp.where(lane == 2, g1, g2)))
    dense_ref[...] = jnp.where(lane == e1, g1, jnp.where(lane == e2, g2, 0.0))


def route_top2(h, gain, w_router, tm):
    t, d = h.shape
    wpad = jnp.pad(w_router, ((0, 0), (0, LANES - N_EXPERTS)))
    return pl.pallas_call(
        _router_kernel,
        grid=(t // tm,),
        in_specs=[pl.BlockSpec((tm, d), lambda i: (i, 0)),
                  pl.BlockSpec((1, d), lambda i: (0, 0)),
                  pl.BlockSpec((d, LANES), lambda i: (0, 0))],
        out_specs=[pl.BlockSpec((tm, LANES), lambda i: (i, 0)),
                   pl.BlockSpec((tm, LANES), lambda i: (i, 0))],
        out_shape=[jax.ShapeDtypeStruct((t, LANES), F32), jax.ShapeDtypeStruct((t, LANES), F32)],
        compiler_params=_params("arbitrary"),
        name="route_top2",
    )(h, gain.reshape(1, d), wpad)


GATHER_TILE = 1024


def _gather_kernel(idx_ref, n_ref, src_ref, o_ref, sem):
    i = pl.program_id(0)

    @pl.when(i * GATHER_TILE < n_ref[0])
    def _():
        def issue(j, carry):
            t = idx_ref[i * GATHER_TILE + j]
            pltpu.make_async_copy(
                src_ref.at[pl.ds(pl.multiple_of(t * ROW_TILES, ROW_TILES), ROW_TILES), :],
                o_ref.at[pl.ds(pl.multiple_of(j * ROW_TILES, ROW_TILES), ROW_TILES), :], sem).start()
            return carry

        lax.fori_loop(0, GATHER_TILE, issue, 0, unroll=8)
        pltpu.make_async_copy(src_ref.at[pl.ds(0, GATHER_TILE * ROW_TILES), :], o_ref, sem).wait()


def gather_rows(src, idx, n):
    r = idx.shape[0]
    last = lambda i, idx_ref, n_ref: (jnp.minimum(i, (n_ref[0] - 1) // GATHER_TILE), 0)
    return pl.pallas_call(
        _gather_kernel,
        grid_spec=pltpu.PrefetchScalarGridSpec(
            num_scalar_prefetch=2,
            grid=(r // GATHER_TILE,),
            in_specs=[pl.BlockSpec(memory_space=pl.ANY)],
            out_specs=pl.BlockSpec((GATHER_TILE * ROW_TILES, LANES), last),
            scratch_shapes=[pltpu.SemaphoreType.DMA(())]),
        out_shape=jax.ShapeDtypeStruct((r * ROW_TILES, LANES), src.dtype),
        compiler_params=_params("arbitrary"),
        name="gather_rows",
    )(idx, n, src)


def _combine_kernel(h_ref, o1_ref, o2_ref, g_ref, h2_ref, hn_ref):
    for s in range(ROW_TILES):
        sl = slice(s * LANES, (s + 1) * LANES)
        h2_ref[:, sl] = h_ref[:, sl] + (o1_ref[_slab(o1_ref, s)] + o2_ref[_slab(o2_ref, s)])
    hn_ref[...] = _rms(h2_ref[...], g_ref[...]).astype(BF16)


def combine_rows(h, og, gain, tm=TM_MID):
    t, d = h.shape
    nt = t // tm
    return pl.pallas_call(
        _combine_kernel,
        grid=(nt,),
        in_specs=[pl.BlockSpec((tm, d), lambda i: (i, 0)),
                  pl.BlockSpec((tm * ROW_TILES, LANES), lambda i: (i, 0)),
                  pl.BlockSpec((tm * ROW_TILES, LANES), lambda i: (i + nt, 0)),
                  pl.BlockSpec((1, d), lambda i: (0, 0))],
        out_specs=[pl.BlockSpec((tm, d), lambda i: (i, 0)),
                   pl.BlockSpec((tm, d), lambda i: (i, 0))],
        out_shape=[jax.ShapeDtypeStruct((t, d), F32), jax.ShapeDtypeStruct((t, d), BF16)],
        compiler_params=_params("arbitrary"),
        name="combine_rows",
    )(h, og, og, gain.reshape(1, d))


def moe_prompt(h, ffn_gain, w_router, wg, wu, wd, layer, ple_gain):
    xn = norm_rows(h, ffn_gain, F32, tiled=True)
    r, _ = route_top2(h, ffn_gain, w_router, TM_BIG)
    experts = jnp.concatenate([r[:, 0], r[:, 1]]).astype(jnp.int32)
    gates = jnp.concatenate([r[:, 2], r[:, 3]])
    onehot = (experts[:, None] == jnp.arange(N_EXPERTS)[None, :]).astype(jnp.int32)
    rank = jnp.sum(jnp.cumsum(onehot, axis=0) * onehot, axis=1) - 1
    counts = jnp.sum(onehot, axis=0)
    tiles = (counts + TM_MOE - 1) // TM_MOE
    tile_end = jnp.cumsum(tiles)
    tile_start = tile_end - tiles
    dest = jnp.sum(onehot * tile_start[None, :], axis=1) * TM_MOE + rank
    token = jnp.tile(jnp.arange(T, dtype=jnp.int32), 2)
    src_token = jnp.zeros((R_PAD,), jnp.int32).at[dest].set(token)
    row_gate = jnp.zeros((R_PAD,), F32).at[dest].set(gates).reshape(R_PAD, 1)
    n_active = tile_end[-1:].astype(jnp.int32)
    tile_ids = jnp.minimum(jnp.arange(MOE_TILES, dtype=jnp.int32), n_active[0] - 1)
    tile_expert = jnp.sum((tile_ids[:, None] >= tile_end[None, :]).astype(jnp.int32), axis=1)
    xs = gather_rows(xn, src_token, n_active * TM_MOE)
    o = swiglu_tiles(xs, wg.reshape((-1,) + wg.shape[2:]), wu.reshape((-1,) + wu.shape[2:]),
                     wd.reshape((-1,) + wd.shape[2:]), tile_expert + layer * N_EXPERTS, n_active,
                     tm=TM_MOE, tf=256, row_gate=row_gate)
    og = gather_rows(o, dest, jnp.full((1,), 2 * T, jnp.int32))
    return combine_rows(h, og, ple_gain)


def _headnorm_kernel(q_ref, k_ref, qg_ref, kg_ref, qn_ref, kn_ref):
    def norm_pairs(x_ref, g_ref, o_ref, scale):
        lo = lax.broadcasted_iota(jnp.int32, (x_ref.shape[0], LANES), 1) < HEAD_DIM
        for j in range(x_ref.shape[1] // LANES):
            x = x_ref[:, j * LANES:(j + 1) * LANES]
            s = x * x
            s_lo = jnp.sum(jnp.where(lo, s, 0.0), axis=-1, keepdims=True)
            s_hi = jnp.sum(jnp.where(lo, 0.0, s), axis=-1, keepdims=True)
            ms = jnp.where(lo, s_lo, s_hi) * (1.0 / HEAD_DIM)
            y = x * lax.rsqrt(ms + NORM_EPS) * g_ref[...]
            o_ref[:, j * LANES:(j + 1) * LANES] = (y * scale).astype(o_ref.dtype)

    norm_pairs(q_ref, qg_ref, qn_ref, HEAD_DIM ** -0.5)
    norm_pairs(k_ref, kg_ref, kn_ref, 1.0)


def head_norm(qkv, q_gain, k_gain, q_dtype, tm):
    t = qkv.shape[0]
    return pl.pallas_call(
        _headnorm_kernel,
        grid=(t // tm,),
        in_specs=[pl.BlockSpec((tm, NQ), lambda i: (i, 0)),
                  pl.BlockSpec((tm, NKV), lambda i: (i, NQ // NKV)),
                  pl.BlockSpec((1, LANES), lambda i: (0, 0)),
                  pl.BlockSpec((1, LANES), lambda i: (0, 0))],
        out_specs=[pl.BlockSpec((tm, NQ), lambda i: (i, 0)),
                   pl.BlockSpec((tm, NKV), lambda i: (i, 0))],
        out_shape=[jax.ShapeDtypeStruct((t, NQ), q_dtype), jax.ShapeDtypeStruct((t, NKV), F32)],
        compiler_params=_params("arbitrary"),
        name="head_norm",
    )(qkv, qkv, jnp.tile(q_gain, 2).reshape(1, LANES), jnp.tile(k_gain, 2).reshape(1, LANES))


def _attn_kernel(slope_ref, sink_ref, q_ref, kp_ref, kc_ref, vp_ref, vc_ref, o_ref):
    rows = WINDOW
    nkeys = 2 * WINDOW
    rep = ATTN_HEADS // ATTN_KV_HEADS
    qi = lax.broadcasted_iota(jnp.int32, (rows, nkeys), 0)
    kj = lax.broadcasted_iota(jnp.int32, (rows, nkeys), 1)
    delta = WINDOW + qi - kj
    mask = (delta >= 0) & (delta <= WINDOW) & ((kj >= WINDOW) | (pl.program_id(1) > 0))
    deltaf = delta.astype(F32)
    lo_q = lax.broadcasted_iota(jnp.int32, (rows, LANES), 1) < HEAD_DIM

    for jp in range(ATTN_KV_HEADS // 2):
        ksl = slice(jp * LANES, (jp + 1) * LANES)
        k2 = jnp.concatenate([kp_ref[:, ksl], kc_ref[:, ksl]], axis=0).astype(BF16)
        v2 = jnp.concatenate([vp_ref[:, ksl], vc_ref[:, ksl]], axis=0)
        v2r = pltpu.roll(v2, HEAD_DIM, 1).astype(BF16)
        v2 = v2.astype(BF16)
        for qp in range(rep):
            col = (jp * rep + qp) * LANES
            q2 = q_ref[:, col:col + LANES]
            q2r = pltpu.roll(q2, HEAD_DIM, 1)
            g_half = qp // (rep // 2)
            outs = []
            for hh in range(2):
                h = (jp * rep + qp) * 2 + hh
                qsel = q2 if hh == g_half else q2r
                qm = jnp.where(lo_q if g_half == 0 else ~lo_q, qsel, jnp.zeros_like(qsel))
                s = _dot_nt(qm, k2) - slope_ref[h] * deltaf
                s = jnp.where(mask, s, -jnp.inf)
                sink = sink_ref[h]
                m = jnp.maximum(jnp.max(s, axis=-1, keepdims=True), sink)
                e = jnp.exp(s - m)
                probs = e / (jnp.sum(e, axis=-1, keepdims=True) + jnp.exp(sink - m))
                outs.append(_dot(probs.astype(BF16), v2 if hh == g_half else v2r))
            o_ref[:, col:col + LANES] = jnp.where(lo_q, outs[0], outs[1]).astype(o_ref.dtype)


def attention_prompt(hn, h, w_qkv, layer, q_gain, k_gain, sinks, w_o, next_gain):
    qkv = matmul_cols(hn, w_qkv, layer, NQ + 2 * NKV)
    qn, kn = head_norm(qkv, q_gain, k_gain, BF16, TM_MID)
    nblk = SEQ // WINDOW
    vcol = NQ // NKV + 1
    cur = lambda c: (lambda b, n: (b * nblk + n, c))
    prev = lambda c: (lambda b, n: (jnp.maximum(b * nblk + n - 1, 0), c))
    smem = pl.BlockSpec(memory_space=pltpu.SMEM)
    o = pl.pallas_call(
        _attn_kernel,
        grid=(BATCH, nblk),
        in_specs=[smem, smem,
                  pl.BlockSpec((WINDOW, NQ), cur(0)),
                  pl.BlockSpec((WINDOW, NKV), prev(0)),
                  pl.BlockSpec((WINDOW, NKV), cur(0)),
                  pl.BlockSpec((WINDOW, NKV), prev(vcol)),
                  pl.BlockSpec((WINDOW, NKV), cur(vcol))],
        out_specs=pl.BlockSpec((WINDOW, NQ), cur(0)),
        out_shape=jax.ShapeDtypeStruct((T, NQ), BF16),
        compiler_params=_params("arbitrary", "arbitrary"),
        name="window_attention",
    )(_alibi_slopes(), sinks, qn, kn, kn, qkv, qkv)
    h2, hn2 = proj_residual(o, w_o, layer, h, next_gain)
    kv_shape = (BATCH, WINDOW, ATTN_KV_HEADS, HEAD_DIM)
    new_k = kn.reshape(BATCH, SEQ, NKV)[:, -WINDOW:].reshape(kv_shape)
    new_v = qkv[:, NQ + NKV:].reshape(BATCH, SEQ, NKV)[:, -WINDOW:].reshape(kv_shape)
    return h2, hn2, (new_k, new_v)


SCONV_TILE = 512
SCONV_TAIL = SUBLANES


def _sconv_kernel(gb_ref, gc_ref, u_ref, gcp_ref, up_ref, w_ref, y_ref, tail_ref):
    rows = gb_ref.shape[0]
    cu = gc_ref[...] * u_ref[...]
    prev = jnp.where(pl.program_id(1) > 0, gcp_ref[...] * up_ref[...], 0.0)
    row = lax.broadcasted_iota(jnp.int32, cu.shape, 0)
    s1 = jnp.where(row == 0, prev[SCONV_TAIL - 1:], pltpu.roll(cu, 1, 0))
    s2 = jnp.where(row == 0, prev[SCONV_TAIL - 2:SCONV_TAIL - 1],
                   jnp.where(row == 1, prev[SCONV_TAIL - 1:], pltpu.roll(cu, 2, 0)))
    conv = w_ref[0:1] * s2 + w_ref[1:2] * s1 + w_ref[2:3] * cu
    y_ref[...] = (gb_ref[...] * conv).astype(y_ref.dtype)
    tail_ref[...] = cu[rows - SCONV_TAIL:]


def sconv_prompt(hn, h, w_in, conv_w, layer, w_out, next_gain):
    d = D_MODEL
    proj = matmul_cols(hn, w_in, layer, 3 * d)
    nt = SEQ // SCONV_TILE
    tpb = SCONV_TILE // SCONV_TAIL
    cur = lambda c: (lambda b, i: (b * nt + i, c))
    prev = lambda c: (lambda b, i: (jnp.maximum((b * nt + i) * tpb - 1, 0), c))
    y, tails = pl.pallas_call(
        _sconv_kernel,
        grid=(BATCH, nt),
        in_specs=[pl.BlockSpec((SCONV_TILE, d), cur(0)),
                  pl.BlockSpec((SCONV_TILE, d), cur(1)),
                  pl.BlockSpec((SCONV_TILE, d), cur(2)),
                  pl.BlockSpec((SCONV_TAIL, d), prev(1)),
                  pl.BlockSpec((SCONV_TAIL, d), prev(2)),
                  pl.BlockSpec((None, SCONV_WIDTH, d), lambda b, i: (layer, 0, 0))],
        out_specs=[pl.BlockSpec((SCONV_TILE, d), cur(0)),
                   pl.BlockSpec((None, SCONV_TAIL, d), lambda b, i: (b * nt + i, 0, 0))],
        out_shape=[jax.ShapeDtypeStruct((T, d), BF16),
                   jax.ShapeDtypeStruct((BATCH * nt, SCONV_TAIL, d), F32)],
        compiler_params=_params("arbitrary", "arbitrary"),
        name="sconv_prompt",
    )(proj, proj, proj, proj, proj, conv_w)
    h2, hn2 = proj_residual(y, w_out, layer, h, next_gain)
    new_state = tails.reshape(BATCH, nt, SCONV_TAIL, d)[:, -1, -(SCONV_WIDTH - 1):]
    return h2, hn2, new_state


XBC_CHUNK = 512
GROUP_W = D_INNER // SSM_GROUPS
HEADS_PER_GROUP = SSM_HEADS // SSM_GROUPS


def _ssd_kernel(z_ref, xc_ref, xp_ref, dt_ref, cw_ref, cb_ref, dtb_ref, alog_ref, dsk_ref, nw_ref,
                y_ref, hfin_ref, state_ref, xact_ref, acst_ref, yg_ref):
    c = pl.program_id(1)
    q = SSM_CHUNK

    @pl.when(c == 0)
    def _():
        state_ref[...] = jnp.zeros_like(state_ref)

    has_prev = c > 0
    row = lax.broadcasted_iota(jnp.int32, (q, XBC_CHUNK), 0)
    for j in range(SSM_CONV_DIM // XBC_CHUNK):
        sl = slice(j * XBC_CHUNK, (j + 1) * XBC_CHUNK)
        cur = xc_ref[:, sl]
        prv = jnp.where(has_prev, xp_ref[:, sl], 0.0)
        acc = None
        for k in range(SSM_CONV_WIDTH):
            sh = SSM_CONV_WIDTH - 1 - k
            if sh == 0:
                term = cur
            else:
                term = jnp.where(row < sh, pltpu.roll(prv, sh, 0), pltpu.roll(cur, sh, 0))
            term = cw_ref[k:k + 1, sl] * term
            acc = term if acc is None else acc + term
        xact_ref[:, sl] = jax.nn.silu(acc + cb_ref[:, sl])

    dt = _softplus(dt_ref[...] + dtb_ref[...])
    da = dt * (-jnp.exp(alog_ref[...]))
    tri = (lax.broadcasted_iota(jnp.int32, (q, q), 0) >= lax.broadcasted_iota(jnp.int32, (q, q), 1))
    a_cs = jnp.dot(tri.astype(F32), da, preferred_element_type=F32, precision=lax.Precision.HIGHEST)
    acst_ref[...] = a_cs.T
    lane = lax.broadcasted_iota(jnp.int32, (q, LANES), 1)

    def group(g, carry):
        goff = pl.multiple_of(g * GROUP_W, GROUP_W)
        xs_g = xact_ref[:, pl.ds(goff, GROUP_W)]
        bm = xact_ref[:, pl.ds(pl.multiple_of(D_INNER + g * D_STATE, D_STATE), D_STATE)]
        cm = xact_ref[:, pl.ds(pl.multiple_of(D_INNER + SSM_GROUPS * D_STATE + g * D_STATE, D_STATE), D_STATE)]
        cm_b = cm.astype(BF16)
        cb = _dot_nt(cm_b, bm.astype(BF16))
        for hh in range(HEADS_PER_GROUP):
            h = g * HEADS_PER_GROUP + hh
            sel = lane == h
            acs_col = jnp.sum(jnp.where(sel, a_cs, 0.0), axis=1, keepdims=True)
            dt_col = jnp.sum(jnp.where(sel, dt, 0.0), axis=1, keepdims=True)
            acs_row = acst_ref[pl.ds(h, 1), :]
            acs_last = acs_col[q - 1:q]
            xs = xs_g[:, hh * SSM_HEAD_DIM:(hh + 1) * SSM_HEAD_DIM]
            xdt = (xs * dt_col).astype(BF16)
            seg = acs_col - acs_row
            decay_in = jnp.where(tri, jnp.exp(jnp.where(tri, seg, 0.0)), 0.0)
            y = _dot((cb * decay_in).astype(BF16), xdt)
            h_in = state_ref[h]
            y = y + _dot_nt(cm_b, h_in.astype(BF16)) * jnp.exp(acs_col)
            decay_out = jnp.exp(acs_last - acs_col)
            st = _dot_tn(xdt, (bm * decay_out).astype(BF16))
            state_ref[h] = jnp.exp(acs_last) * h_in + st
            yg_ref[:, hh * SSM_HEAD_DIM:(hh + 1) * SSM_HEAD_DIM] = y
        y_g = yg_ref[...] + dsk_ref[:, pl.ds(goff, GROUP_W)] * xs_g
        gg = y_g * jax.nn.silu(z_ref[:, pl.ds(goff, GROUP_W)])
        gg = gg * lax.rsqrt(jnp.mean(gg * gg, axis=-1, keepdims=True) + NORM_EPS)
        y_ref[:, pl.ds(goff, GROUP_W)] = (gg * nw_ref[:, pl.ds(goff, GROUP_W)]).astype(y_ref.dtype)
        return carry

    lax.fori_loop(0, SSM_GROUPS, group, 0)

    @pl.when(c == pl.num_programs(1) - 1)
    def _():
        hfin_ref[...] = state_ref[...]


def _pad_heads(v):
    return jnp.pad(v, (0, LANES - SSM_HEADS)).reshape(1, LANES)


def ssd_prompt(hn, h, w_in, layer, conv_w, conv_b, dt_bias, a_log, d_skip, norm_w, w_out, next_gain):
    cd = SSM_CONV_DIM
    z = matmul_cols(hn, w_in, layer, D_INNER, col_off=0)
    xbc = matmul_cols(hn, w_in, layer, cd, col_off=D_INNER)
    w_dt = jnp.pad(w_in[layer, :, D_INNER + cd:], ((0, 0), (0, LANES - SSM_HEADS)))[None]
    dt_raw = matmul_cols(hn, w_dt, 0, LANES)
    dsk_full = jnp.repeat(d_skip, SSM_HEAD_DIM).reshape(1, D_INNER)
    nc = SEQ // SSM_CHUNK
    cur = lambda b, c: (b * nc + c, 0)
    whole = lambda b, c: (0, 0)
    y, h_fin = pl.pallas_call(
        _ssd_kernel,
        grid=(BATCH, nc),
        in_specs=[pl.BlockSpec((SSM_CHUNK, D_INNER), cur),
                  pl.BlockSpec((SSM_CHUNK, cd), cur),
                  pl.BlockSpec((SSM_CHUNK, cd), lambda b, c: (jnp.maximum(b * nc + c - 1, 0), 0)),
                  pl.BlockSpec((SSM_CHUNK, LANES), cur),
                  pl.BlockSpec((SSM_CONV_WIDTH, cd), whole),
                  pl.BlockSpec((1, cd), whole),
                  pl.BlockSpec((1, LANES), whole),
                  pl.BlockSpec((1, LANES), whole),
                  pl.BlockSpec((1, D_INNER), whole),
                  pl.BlockSpec((1, D_INNER), whole)],
        out_specs=[pl.BlockSpec((SSM_CHUNK, D_INNER), cur),
                   pl.BlockSpec((None, SSM_HEADS, SSM_HEAD_DIM, D_STATE), lambda b, c: (b, 0, 0, 0))],
        out_shape=[jax.ShapeDtypeStruct((T, D_INNER), BF16),
                   jax.ShapeDtypeStruct((BATCH, SSM_HEADS, SSM_HEAD_DIM, D_STATE), F32)],
        scratch_shapes=[pltpu.VMEM((SSM_HEADS, SSM_HEAD_DIM, D_STATE), F32),
                        pltpu.VMEM((SSM_CHUNK, cd), F32),
                        pltpu.VMEM((LANES, SSM_CHUNK), F32),
                        pltpu.VMEM((SSM_CHUNK, GROUP_W), F32)],
        compiler_params=_params("arbitrary", "arbitrary"),
        name="ssd_prompt",
    )(z, xbc, xbc, dt_raw, conv_w, conv_b.reshape(1, cd), _pad_heads(dt_bias), _pad_heads(a_log), dsk_full,
      norm_w.reshape(1, D_INNER))
    h2, hn2 = proj_residual(y, w_out, layer, h, next_gain)
    new_conv = xbc.reshape(BATCH, SEQ, cd)[:, -(SSM_CONV_WIDTH - 1):]
    return h2, hn2, (h_fin, new_conv)


def _split(x):
    hi = x.astype(BF16)
    lo = (x - hi.astype(F32)).astype(BF16)
    return hi, lo


def _hp_kernel(*refs, nk, x_mode, epi):
    refs = list(refs)
    if x_mode == "norm":
        x_ref, g_ref = refs[:2]
        refs = refs[2:]
        x = _rms(x_ref[...], g_ref[...])
    elif x_mode == "swiglu":
        g_ref, u_ref = refs[:2]
        refs = refs[2:]
        x = jax.nn.silu(g_ref[...]) * u_ref[...]
    else:
        x = refs[0][...]
        refs = refs[1:]
    w_ref = refs[0]
    refs = refs[1:]
    if epi == "gated":
        res_ref, pp_ref, o_ref = refs
    elif epi == "res":
        res_ref, o_ref = refs
    else:
        (o_ref,) = refs
    k = pl.program_id(2)
    rows = x.shape[0]
    x_hi, x_lo = _split(x)
    w_hi, w_lo = _split(w_ref[...])
    both = _dot(jnp.concatenate([x_hi, x_lo], axis=0), w_hi)
    part = both[:rows] + both[rows:] + _dot(x_hi, w_lo)

    @pl.when(k == 0)
    def _():
        o_ref[...] = part

    @pl.when(k > 0)
    def _():
        o_ref[...] += part

    if epi is not None:
        @pl.when(k == nk - 1)
        def _():
            acc = o_ref[...]
            if epi == "gated":
                acc = jax.nn.sigmoid(acc) * pp_ref[...]
            o_ref[...] = res_ref[...] + acc


def hp_linear(w, widx, n_out, *, x=None, norm=None, swiglu=None, col_off=0, n_e=1, tk, tn, res=None, pp=None):
    kdim = w.shape[1]
    nk = kdim // tk
    off = col_off // tn
    assert kdim % tk == 0 and n_out % tn == 0 and col_off % tn == 0
    if norm is not None:
        assert nk == 1
        x_mode = "norm"
        args = [norm[0], norm[1].reshape(1, kdim)]
        in_specs = [pl.BlockSpec((NS, tk), lambda e, n, k: (0, k)),
                    pl.BlockSpec((1, tk), lambda e, n, k: (0, k))]
    elif swiglu is not None:
        x_mode = "swiglu"
        args = list(swiglu)
        in_specs = [pl.BlockSpec((None, NS, tk), lambda e, n, k: (e, 0, k))] * 2
    else:
        x_mode = "plain"
        args = [x]
        in_specs = [pl.BlockSpec((NS, tk), lambda e, n, k: (0, k))]
    args.append(w)
    in_specs.append(pl.BlockSpec((None, tk, tn), lambda e, n, k: (widx + e, k, n + off)))
    epi = None
    if res is not None:
        epi = "res"
        args.append(res)
        in_specs.append(pl.BlockSpec((NS, tn), lambda e, n, k: (0, n)))
        if pp is not None:
            epi = "gated"
            args.append(pp)
            in_specs.append(pl.BlockSpec((NS, tn), lambda e, n, k: (0, n)))
    return pl.pallas_call(
        functools.partial(_hp_kernel, nk=nk, x_mode=x_mode, epi=epi),
        grid=(n_e, n_out // tn, nk),
        in_specs=in_specs,
        out_specs=pl.BlockSpec((None, NS, tn), lambda e, n, k: (e, 0, n)),
        out_shape=jax.ShapeDtypeStruct((n_e, NS, n_out), F32),
        compiler_params=_params("arbitrary", "arbitrary", "arbitrary"),
        name="hp_linear_" + x_mode,
    )(*args)


def _attn_step_kernel(slope_ref, sink_ref, q_ref, kc_ref, vc_ref, kn_ref, vn_ref, o_ref):
    rep = ATTN_HEADS // ATTN_KV_HEADS
    lo = lax.broadcasted_iota(jnp.int32, (1, LANES), 1) < HEAD_DIM
    delta = (WINDOW - lax.broadcasted_iota(jnp.int32, (WINDOW, 1), 0)).astype(F32)
    for jp in range(ATTN_KV_HEADS // 2):
        ksl = slice(jp * LANES, (jp + 1) * LANES)
        k2 = kc_ref[:, ksl]
        v2 = vc_ref[:, ksl]
        kn2 = kn_ref[:, ksl]
        vn2 = vn_ref[:, ksl]
        for qp in range(rep):
            col = (jp * rep + qp) * LANES
            q2 = q_ref[:, col:col + LANES]
            q2r = pltpu.roll(q2, HEAD_DIM, 1)
            g_half = qp // (rep // 2)
            kv_lanes = lo if g_half == 0 else ~lo
            out = jnp.zeros((1, LANES), F32)
            for hh in range(2):
                h = (jp * rep + qp) * 2 + hh
                qm = jnp.where(kv_lanes, q2 if hh == g_half else q2r, 0.0)
                s = jnp.sum(k2 * qm, axis=1, keepdims=True) - slope_ref[h] * delta
                s_new = jnp.sum(kn2 * qm, axis=1, keepdims=True)
                sink = sink_ref[h]
                m = jnp.maximum(jnp.maximum(jnp.max(s, axis=0, keepdims=True), s_new), sink)
                e = jnp.exp(s - m)
                e_new = jnp.exp(s_new - m)
                denom = jnp.sum(e, axis=0, keepdims=True) + e_new + jnp.exp(sink - m)
                o = jnp.sum((e / denom) * v2, axis=0, keepdims=True) + (e_new / denom) * vn2
                if hh != g_half:
                    o = pltpu.roll(o, HEAD_DIM, 1)
                out = jnp.where(lo if hh == 0 else ~lo, o, out)
            o_ref[:, col:col + LANES] = out


def _sconv_step_kernel(proj_ref, p_ref, w_ref, y_ref, cu_ref):
    d = D_MODEL
    cu = proj_ref[:, d:2 * d] * proj_ref[:, 2 * d:]
    conv = w_ref[0:1] * p_ref[0] + w_ref[1:2] * p_ref[1] + w_ref[2:3] * cu
    y_ref[...] = proj_ref[:, :d] * conv
    cu_ref[...] = cu


def _ssd_step_pre_kernel(xc_ref, p_ref, dt_ref, cw_ref, cb_ref, dtb_ref, alog_ref, xact_ref, dt_out_ref, dec_ref):
    acc = cw_ref[0:1] * p_ref[0] + cw_ref[1:2] * p_ref[1] + cw_ref[2:3] * p_ref[2] + cw_ref[3:4] * xc_ref[...]
    xact_ref[...] = jax.nn.silu(acc + cb_ref[...])
    dt = _softplus(dt_ref[...] + dtb_ref[...])
    dt_out_ref[...] = dt
    dec_ref[...] = jnp.exp(dt * (-jnp.exp(alog_ref[...])))


def _ssd_step_kernel(dt_ref, dec_ref, dsk_ref, xt_ref, bm_ref, cm_ref, h0_ref, yt_ref, h1_ref):
    b = pl.program_id(0)
    xt = xt_ref[...]
    lane = lax.broadcasted_iota(jnp.int32, xt.shape, 1)

    def head(h, yt):
        g = h // HEADS_PER_GROUP
        sel = lane == h
        x_col = jnp.sum(jnp.where(sel, xt, 0.0), axis=1, keepdims=True)
        bm = bm_ref[pl.ds(g, 1), :]
        cm = cm_ref[pl.ds(g, 1), :]
        h1 = dec_ref[b, h] * h0_ref[h] + (dt_ref[b, h] * x_col) * bm
        h1_ref[h] = h1
        y_col = jnp.sum(h1 * cm, axis=1, keepdims=True) + dsk_ref[h] * x_col
        return jnp.where(sel, y_col, yt)

    yt_ref[...] = lax.fori_loop(0, SSM_HEADS, head, jnp.zeros(xt.shape, F32))


def _ssd_gate_kernel(y_ref, z_ref, nw_ref, o_ref):
    for g in range(SSM_GROUPS):
        sl = slice(g * GROUP_W, (g + 1) * GROUP_W)
        gg = y_ref[:, sl] * jax.nn.silu(z_ref[:, sl])
        gg = gg * lax.rsqrt(jnp.mean(gg * gg, axis=-1, keepdims=True) + NORM_EPS)
        o_ref[:, sl] = gg * nw_ref[:, sl]


def _moe_mix_kernel(h_ref, gates_ref, d_ref, o_ref):
    lane = lax.broadcasted_iota(jnp.int32, gates_ref.shape, 1)
    y = jnp.zeros(h_ref.shape, F32)
    for e in range(N_EXPERTS):
        gate = jnp.sum(jnp.where(lane == e, gates_ref[...], 0.0), axis=1, keepdims=True)
        y = y + gate * d_ref[e]
    o_ref[...] = h_ref[...] + y


def _whole(shape):
    return pl.BlockSpec(shape, lambda *_: (0,) * len(shape))


def _small_call(body, out_shapes, *args, name):
    multi = isinstance(out_shapes, (list, tuple))
    outs = list(out_shapes) if multi else [out_shapes]
    res = pl.pallas_call(
        body,
        grid=(1,),
        in_specs=[_whole(a.shape) for a in args],
        out_specs=[_whole(o.shape) for o in outs],
        out_shape=outs,
        compiler_params=_params("arbitrary"),
        name=name,
    )(*args)
    return res if multi else res[0]


def _sds(*shape):
    return jax.ShapeDtypeStruct(shape, F32)


def attention_sample(h, norm_gain, cache_k, cache_v, w_qkv, layer, q_gain, k_gain, sinks, w_o):
    qkv = hp_linear(w_qkv, layer, NQ + 2 * NKV, norm=(h, norm_gain), tk=D_MODEL, tn=512)[0]
    qn, kn = head_norm(qkv, q_gain, k_gain, F32, NS)
    v_new = qkv[:, NQ + NKV:]
    smem = pl.BlockSpec(memory_space=pltpu.SMEM)
    rowsq = pl.BlockSpec((None, 1, NQ), lambda b: (b, 0, 0))
    rowsk = pl.BlockSpec((None, 1, NKV), lambda b: (b, 0, 0))
    cache = pl.BlockSpec((None, WINDOW, NKV), lambda b: (b, 0, 0))
    o = pl.pallas_call(
        _attn_step_kernel,
        grid=(NS,),
        in_specs=[smem, smem, rowsq, cache, cache, rowsk, rowsk],
        out_specs=rowsq,
        out_shape=_sds(NS, 1, NQ),
        compiler_params=_params("arbitrary"),
        name="attention_sample",
    )(_alibi_slopes(), sinks, qn.reshape(NS, 1, NQ), cache_k.reshape(NS, WINDOW, NKV),
      cache_v.reshape(NS, WINDOW, NKV), kn.reshape(NS, 1, NKV), v_new.reshape(NS, 1, NKV))
    h2 = hp_linear(w_o, layer, D_MODEL, x=o.reshape(NS, NQ), tk=1024, tn=1024, res=h)[0]
    kv_row = (NS, 1, ATTN_KV_HEADS, HEAD_DIM)
    new_k = jnp.concatenate([cache_k[:, 1:], kn.reshape(kv_row)], axis=1)
    new_v = jnp.concatenate([cache_v[:, 1:], v_new.reshape(kv_row)], axis=1)
    return h2, (new_k, new_v)


def sconv_sample(h, norm_gain, past, w_in, conv_w, layer, w_out):
    d = D_MODEL
    proj = hp_linear(w_in, layer, 3 * d, norm=(h, norm_gain), tk=d, tn=512)[0]
    y, cu = _small_call(_sconv_step_kernel, [_sds(NS, d), _sds(NS, d)],
                        proj, past.transpose(1, 0, 2), conv_w[layer], name="sconv_sample")
    h2 = hp_linear(w_out, layer, d, x=y, tk=1024, tn=1024, res=h)[0]
    return h2, jnp.concatenate([past[:, 1:], cu[:, None]], axis=1)


def ssd_sample(h, norm_gain, conv_past, h0, w_in, layer, conv_w, conv_b, dt_bias, a_log, d_skip, norm_w, w_out):
    cd = SSM_CONV_DIM
    z = hp_linear(w_in, layer, D_INNER, norm=(h, norm_gain), tk=D_MODEL, tn=512)[0]
    xbc = hp_linear(w_in, layer, cd, norm=(h, norm_gain), col_off=D_INNER, tk=D_MODEL, tn=512)[0]
    w_dt = jnp.pad(w_in[layer, :, D_INNER + cd:], ((0, 0), (0, LANES - SSM_HEADS)))[None]
    dt_raw = hp_linear(w_dt, 0, LANES, norm=(h, norm_gain), tk=D_MODEL, tn=LANES)[0]
    xact, dt, dec = _small_call(
        _ssd_step_pre_kernel, [_sds(NS, cd), _sds(NS, LANES), _sds(NS, LANES)],
        xbc, conv_past.transpose(1, 0, 2), dt_raw, conv_w, conv_b.reshape(1, cd), _pad_heads(dt_bias),
        _pad_heads(a_log), name="ssd_sample_pre")
    xt = xact[:, :D_INNER].reshape(NS, SSM_HEADS, SSM_HEAD_DIM).transpose(0, 2, 1)
    bm = xact[:, D_INNER:D_INNER + SSM_GROUPS * D_STATE].reshape(NS, SSM_GROUPS, D_STATE)
    cm = xact[:, D_INNER + SSM_GROUPS * D_STATE:].reshape(NS, SSM_GROUPS, D_STATE)
    smem = pl.BlockSpec(memory_space=pltpu.SMEM)
    per_b = lambda *shape: pl.BlockSpec((None,) + shape, lambda b: (b,) + (0,) * len(shape))
    yt, h1 = pl.pallas_call(
        _ssd_step_kernel,
        grid=(NS,),
        in_specs=[smem, smem, smem,
                  per_b(SSM_HEAD_DIM, SSM_HEADS), per_b(SSM_GROUPS, D_STATE), per_b(SSM_GROUPS, D_STATE),
                  per_b(SSM_HEADS, SSM_HEAD_DIM, D_STATE)],
        out_specs=[per_b(SSM_HEAD_DIM, SSM_HEADS), per_b(SSM_HEADS, SSM_HEAD_DIM, D_STATE)],
        out_shape=[_sds(NS, SSM_HEAD_DIM, SSM_HEADS), _sds(NS, SSM_HEADS, SSM_HEAD_DIM, D_STATE)],
        compiler_params=_params("arbitrary"),
        name="ssd_sample_step",
    )(dt[:, :SSM_HEADS], dec[:, :SSM_HEADS], d_skip, xt, bm, cm, h0)
    y = yt.transpose(0, 2, 1).reshape(NS, D_INNER)
    yn = _small_call(_ssd_gate_kernel, _sds(NS, D_INNER), y, z, norm_w.reshape(1, D_INNER), name="ssd_sample_gate")
    h2 = hp_linear(w_out, layer, D_MODEL, x=yn, tk=1024, tn=1024, res=h)[0]
    return h2, (h1, jnp.concatenate([conv_past[:, 1:], xbc[:, None]], axis=1))


def ffn_sample(h, norm_gain, wg, wu, wd, widx, n_e):
    g = hp_linear(wg, widx, D_FF, norm=(h, norm_gain), n_e=n_e, tk=D_MODEL, tn=512)
    u = hp_linear(wu, widx, D_FF, norm=(h, norm_gain), n_e=n_e, tk=D_MODEL, tn=512)
    return g, u


def ple_sample(h, p, ple_gain, w_gate, w_proj, layer):
    pp = hp_linear(w_proj, layer, D_MODEL, x=p, tk=PLE_DIM, tn=D_MODEL)[0]
    return hp_linear(w_gate, layer, D_MODEL, norm=(h, ple_gain), tk=D_MODEL, tn=512, res=h, pp=pp)[0]


def kernel(x_prompt, x_sample, cache_k_win, cache_v_win, state_short_conv, state_ssm, state_ssm_conv, p_prompt, p_sample, norm_mixer, norm_ffn, attn_w_qkv, attn_q_norm, attn_k_norm, attn_sinks, attn_w_out, sconv_w_in, sconv_conv_w, sconv_w_out, ssm_w_in, ssm_conv_w, ssm_conv_b, ssm_dt_bias, ssm_a_log, ssm_d_skip, ssm_norm, ssm_w_out, ffn_w_gate, ffn_w_up, ffn_w_down, moe_w_router, moe_w_gate, moe_w_up, moe_w_down, ple_norm, ple_w_gate, ple_w_proj):
    flat_e = lambda w: w.reshape((-1,) + w.shape[2:])
    moe_wg, moe_wu, moe_wd = flat_e(moe_w_gate), flat_e(moe_w_up), flat_e(moe_w_down)
    dense_tiles = T // TM_BIG
    dense_active = jnp.full((1,), dense_tiles, jnp.int32)

    h = x_prompt.reshape(T, D_MODEL)
    hn = norm_rows(h, norm_mixer[0], BF16)
    k_p, v_p, sc_p, ssm_p, ssmc_p = [], [], [], [], []
    for i in range(DEPTH):
        kind, j = i % 3, i // 3
        if kind == 0:
            h, hn, new = attention_prompt(hn, h, attn_w_qkv, j, attn_q_norm[j], attn_k_norm[j], attn_sinks[j],
                                          attn_w_out, norm_ffn[i])
            k_p.append(new[0]); v_p.append(new[1])
        elif kind == 1:
            h, hn, new = sconv_prompt(hn, h, sconv_w_in, sconv_conv_w, j, sconv_w_out, norm_ffn[i])
            sc_p.append(new)
        else:
            h, hn, new = ssd_prompt(hn, h, ssm_w_in, j, ssm_conv_w[j], ssm_conv_b[j], ssm_dt_bias[j],
                                    ssm_a_log[j], ssm_d_skip[j], ssm_norm[j], ssm_w_out, norm_ffn[i])
            ssm_p.append(new[0]); ssmc_p.append(new[1])
        c = i // 2
        if i % 2 == 0:
            h, hn = swiglu_tiles(hn, ffn_w_gate, ffn_w_up, ffn_w_down, jnp.full((dense_tiles,), c, jnp.int32),
                                 dense_active, tm=TM_BIG, tf=256, res=h, gain=ple_norm[i])
        else:
            h, hn = moe_prompt(h, norm_ffn[i], moe_w_router[c], moe_w_gate, moe_w_up, moe_w_down, c, ple_norm[i])
        next_gain = norm_mixer[i + 1] if i + 1 < DEPTH else None
        h, hn = proj_residual(hn, ple_w_gate, i, h, next_gain, p=p_prompt[i].reshape(T, PLE_DIM), wp=ple_w_proj)
    y_p = h.reshape(BATCH, SEQ, D_MODEL)

    s = x_sample.reshape(NS, D_MODEL)
    k_s, v_s, sc_s, ssm_s, ssmc_s = [], [], [], [], []
    for i in range(DEPTH):
        kind, j = i % 3, i // 3
        if kind == 0:
            s, new = attention_sample(s, norm_mixer[i], cache_k_win[j], cache_v_win[j], attn_w_qkv, j,
                                      attn_q_norm[j], attn_k_norm[j], attn_sinks[j], attn_w_out)
            k_s.append(new[0]); v_s.append(new[1])
        elif kind == 1:
            s, new = sconv_sample(s, norm_mixer[i], state_short_conv[j], sconv_w_in, sconv_conv_w, j, sconv_w_out)
            sc_s.append(new)
        else:
            s, new = ssd_sample(s, norm_mixer[i], state_ssm_conv[j], state_ssm[j], ssm_w_in, j, ssm_conv_w[j],
                                ssm_conv_b[j], ssm_dt_bias[j], ssm_a_log[j], ssm_d_skip[j], ssm_norm[j], ssm_w_out)
            ssm_s.append(new[0]); ssmc_s.append(new[1])
        c = i // 2
        if i % 2 == 0:
            g, u = ffn_sample(s, norm_ffn[i], ffn_w_gate, ffn_w_up, ffn_w_down, c, 1)
            s = hp_linear(ffn_w_down, c, D_MODEL, swiglu=(g, u), tk=1024, tn=1024, res=s)[0]
        else:
            _, gates = route_top2(s, norm_ffn[i], moe_w_router[c], NS)
            g, u = ffn_sample(s, norm_ffn[i], moe_wg, moe_wu, moe_wd, c * N_EXPERTS, N_EXPERTS)
            d_e = hp_linear(moe_wd, c * N_EXPERTS, D_MODEL, swiglu=(g, u), n_e=N_EXPERTS, tk=1024, tn=1024)
            s = _small_call(_moe_mix_kernel, _sds(NS, D_MODEL), s, gates, d_e, name="moe_mix_sample")
        s = ple_sample(s, p_sample[i].reshape(NS, PLE_DIM), ple_norm[i], ple_w_gate, ple_w_proj, i)
    y_s = s.reshape(NS, 1, D_MODEL)

    return (y_p, y_s, jnp.stack(k_p), jnp.stack(v_p), jnp.stack(k_s), jnp.stack(v_s),
            jnp.stack(sc_p), jnp.stack(sc_s), jnp.stack(ssm_p), jnp.stack(ssm_s),
            jnp.stack(ssmc_p), jnp.stack(ssmc_s))
```

```python
import functools

import jax
import jax.numpy as jnp
from jax import lax
from jax.experimental import pallas as pl
from jax.experimental.pallas import tpu as pltpu

F32 = jnp.float32
BF16 = jnp.bfloat16

D_MODEL = 2048
BATCH = 2
SEQ = 4096
DEPTH = 4
DEC_BATCH = 32
PAST_LEN = 16384
ATTN_HEADS = 32
ATTN_KV_HEADS = 8
HEAD_DIM = 64
WINDOW = 128
SCONV_WIDTH = 3
D_INNER = 2 * D_MODEL
SSM_HEAD_DIM = 64
SSM_HEADS = D_INNER // SSM_HEAD_DIM
SSM_GROUPS = 8
D_STATE = 128
SSM_CONV_WIDTH = 4
SSM_CHUNK = 128
SSM_CONV_DIM = D_INNER + 2 * SSM_GROUPS * D_STATE
D_FF = 7 * D_MODEL // 2
N_EXPERTS = 8
PLE_DIM = 256
NORM_EPS = 1e-6
NQ = ATTN_HEADS * HEAD_DIM
NKV = ATTN_KV_HEADS * HEAD_DIM
assert PAST_LEN >= WINDOW

LANES = 128
SUBLANES = 8
T = BATCH * SEQ
NS = DEC_BATCH
TM_BIG = 1024
TM_MID = 512
TM_MOE = 512
MOE_TILES = (2 * T + N_EXPERTS * (TM_MOE - 1)) // TM_MOE
R_PAD = MOE_TILES * TM_MOE
ROW_TILES = D_MODEL // LANES
VMEM_LIMIT = 56 * 1024 * 1024


def _params(*sem):
    return pltpu.CompilerParams(dimension_semantics=sem, vmem_limit_bytes=VMEM_LIMIT)


def _rms(x, g):
    return x * lax.rsqrt(jnp.mean(x * x, axis=-1, keepdims=True) + NORM_EPS) * g


def _dot(a, b):
    return jnp.dot(a, b, preferred_element_type=F32)


def _dot_nt(a, b):
    return lax.dot_general(a, b, (((1,), (1,)), ((), ())), preferred_element_type=F32)


def _dot_tn(a, b):
    return lax.dot_general(a, b, (((0,), (0,)), ((), ())), preferred_element_type=F32)


def _softplus(x):
    return jnp.maximum(x, 0.0) + jnp.log1p(jnp.exp(-jnp.abs(x)))


def _alibi_slopes():
    return jnp.exp2(-8.0 * jnp.arange(1, ATTN_HEADS + 1, dtype=F32) / ATTN_HEADS)


def _slab(tiled_ref, s):
    return (pl.ds(s, tiled_ref.shape[0] // ROW_TILES, stride=ROW_TILES), slice(None))


def _rows_from_tiles(tiled_ref, rows_ref):
    for s in range(ROW_TILES):
        rows_ref[:, s * LANES:(s + 1) * LANES] = tiled_ref[_slab(tiled_ref, s)].astype(rows_ref.dtype)


def _norm_kernel(x_ref, g_ref, o_ref, *, tiled):
    y = _rms(x_ref[...], g_ref[...]).astype(o_ref.dtype)
    if tiled:
        for s in range(ROW_TILES):
            o_ref[_slab(o_ref, s)] = y[:, s * LANES:(s + 1) * LANES]
    else:
        o_ref[...] = y


def norm_rows(x, gain, dtype, tiled=False, tm=TM_BIG):
    t, d = x.shape
    out_block, out_full = ((tm * ROW_TILES, LANES), (t * ROW_TILES, LANES)) if tiled else ((tm, d), (t, d))
    return pl.pallas_call(
        functools.partial(_norm_kernel, tiled=tiled),
        grid=(t // tm,),
        in_specs=[pl.BlockSpec((tm, d), lambda i: (i, 0)),
                  pl.BlockSpec((1, d), lambda i: (0, 0))],
        out_specs=pl.BlockSpec(out_block, lambda i: (i, 0)),
        out_shape=jax.ShapeDtypeStruct(out_full, dtype),
        compiler_params=_params("arbitrary"),
        name="norm_rows",
    )(x, gain.reshape(1, d))


def _mm_kernel(x_ref, w_ref, o_ref, wb_ref):
    @pl.when(pl.program_id(1) == 0)
    def _():
        wb_ref[...] = w_ref[...].astype(BF16)

    o_ref[...] = _dot(x_ref[...], wb_ref[...])


def matmul_cols(x, w, layer, n_out, col_off=0, tn=512, tm=TM_BIG):
    t, k = x.shape
    tn = min(tn, n_out)
    off = col_off // tn
    assert col_off % tn == 0 and n_out % tn == 0
    return pl.pallas_call(
        _mm_kernel,
        grid=(n_out // tn, t // tm),
        in_specs=[pl.BlockSpec((tm, k), lambda n, m: (m, 0)),
                  pl.BlockSpec((None, k, tn), lambda n, m: (layer, 0, n + off))],
        out_specs=pl.BlockSpec((tm, tn), lambda n, m: (m, n)),
        out_shape=jax.ShapeDtypeStruct((t, n_out), F32),
        scratch_shapes=[pltpu.VMEM((k, tn), BF16)],
        compiler_params=_params("arbitrary", "arbitrary"),
        name="matmul_cols",
    )(x, w)


def _proj_kernel(*refs, tn, nn, gated, with_norm):
    x_ref, w_ref, res_ref = refs[:3]
    refs = refs[3:]
    if gated:
        p_ref, wp_ref = refs[:2]
        refs = refs[2:]
    if with_norm:
        g_ref, h_ref, hn_ref = refs
    else:
        (h_ref,) = refs
    n = pl.program_id(1)
    h_ref[:, pl.ds(pl.multiple_of(n * tn, tn), tn)] = _dot(x_ref[...], w_ref[...].astype(BF16))

    @pl.when(n == nn - 1)
    def _():
        upd = h_ref[...]
        if gated:
            upd = jax.nn.sigmoid(upd) * _dot(p_ref[...].astype(BF16), wp_ref[...].astype(BF16))
        h_new = res_ref[...] + upd
        h_ref[...] = h_new
        if with_norm:
            hn_ref[...] = _rms(h_new, g_ref[...]).astype(BF16)


def proj_residual(x, w, layer, res, gain=None, p=None, wp=None, tm=TM_MID, tn=512):
    t, kdim = x.shape
    d = w.shape[2]
    nn = d // tn
    gated = p is not None
    with_norm = gain is not None
    in_specs = [pl.BlockSpec((tm, kdim), lambda m, k: (m, 0)),
                pl.BlockSpec((None, kdim, tn), lambda m, k: (layer, 0, k)),
                pl.BlockSpec((tm, d), lambda m, k: (m, 0))]
    args = [x, w, res]
    if gated:
        in_specs += [pl.BlockSpec((tm, p.shape[1]), lambda m, k: (m, 0)),
                     pl.BlockSpec((None,) + wp.shape[1:], lambda m, k: (layer, 0, 0))]
        args += [p, wp]
    out_specs = [pl.BlockSpec((tm, d), lambda m, k: (m, 0))]
    out_shape = [jax.ShapeDtypeStruct((t, d), F32)]
    if with_norm:
        in_specs.append(pl.BlockSpec((1, d), lambda m, k: (0, 0)))
        args.append(gain.reshape(1, d))
        out_specs.append(pl.BlockSpec((tm, d), lambda m, k: (m, 0)))
        out_shape.append(jax.ShapeDtypeStruct((t, d), BF16))
    out = pl.pallas_call(
        functools.partial(_proj_kernel, tn=tn, nn=nn, gated=gated, with_norm=with_norm),
        grid=(t // tm, nn),
        in_specs=in_specs,
        out_specs=out_specs,
        out_shape=out_shape,
        compiler_params=_params("arbitrary", "arbitrary"),
        name="proj_gated" if gated else "proj_residual",
    )(*args)
    return (out[0], out[1]) if with_norm else (out[0], None)


FFN_DOWN_COLS = 512


def _ffn_kernel(te_ref, na_ref, x_ref, wg_ref, wu_ref, wd_ref, *refs, nf, routed):
    if routed:
        gate_ref, out_ref, xb_ref = refs
    else:
        res_ref, g_ref, out_ref, hn_ref = refs
    i = pl.program_id(0)
    f = pl.program_id(1)

    @pl.when(i < na_ref[0])
    def _():
        if routed:
            @pl.when(f == 0)
            def _():
                _rows_from_tiles(x_ref, xb_ref)
            x = xb_ref[...]
        else:
            x = x_ref[...]
        g = _dot(x, wg_ref[...].astype(BF16))
        u = _dot(x, wu_ref[...].astype(BF16))
        a = (jax.nn.silu(g) * u).astype(BF16)
        for n in range(wd_ref.shape[1] // FFN_DOWN_COLS):
            sl = slice(n * FFN_DOWN_COLS, (n + 1) * FFN_DOWN_COLS)
            part = _dot(a, wd_ref[:, sl].astype(BF16))
            if routed:
                for j in range(FFN_DOWN_COLS // LANES):
                    slab = _slab(out_ref, n * (FFN_DOWN_COLS // LANES) + j)
                    piece = part[:, j * LANES:(j + 1) * LANES]

                    @pl.when(f == 0)
                    def _():
                        out_ref[slab] = piece

                    @pl.when((f > 0) & (f < nf - 1))
                    def _():
                        out_ref[slab] += piece

                    @pl.when(f == nf - 1)
                    def _():
                        out_ref[slab] = gate_ref[...] * (out_ref[slab] + piece)
            else:
                @pl.when(f == 0)
                def _():
                    out_ref[:, sl] = part

                @pl.when(f > 0)
                def _():
                    out_ref[:, sl] += part

        if not routed:
            @pl.when(f == nf - 1)
            def _():
                h_new = res_ref[...] + out_ref[...]
                out_ref[...] = h_new
                hn_ref[...] = _rms(h_new, g_ref[...]).astype(BF16)


def swiglu_tiles(x, wg, wu, wd, tile_expert, n_active, *, tm, tf, res=None, gain=None, row_gate=None):
    routed = row_gate is not None
    r = x.shape[0] // ROW_TILES if routed else x.shape[0]
    d = wd.shape[2]
    ff = wg.shape[2]
    nf = ff // tf

    def row(i, f, te, na):
        return (jnp.minimum(i, na[0] - 1), 0)

    def fcol(i, f, na):
        return jnp.where(i < na[0], f, nf - 1)

    single = pl.Buffered(1)
    w_specs = [pl.BlockSpec((None, d, tf), lambda i, f, te, na: (te[i], 0, fcol(i, f, na))),
               pl.BlockSpec((None, d, tf), lambda i, f, te, na: (te[i], 0, fcol(i, f, na))),
               pl.BlockSpec((None, tf, d), lambda i, f, te, na: (te[i], fcol(i, f, na), 0))]
    if routed:
        in_specs = [pl.BlockSpec((tm * ROW_TILES, LANES), row)]
        in_specs += w_specs + [pl.BlockSpec((tm, 1), row)]
        args = [x, wg, wu, wd, row_gate]
        assert nf > 2
        out_specs = pl.BlockSpec((tm * ROW_TILES, LANES), row, pipeline_mode=single)
        out_shape = jax.ShapeDtypeStruct((r * ROW_TILES, LANES), F32)
        scratch = [pltpu.VMEM((tm, d), BF16)]
    else:
        in_specs = [pl.BlockSpec((tm, d), row)] + w_specs
        in_specs += [pl.BlockSpec((tm, d), row, pipeline_mode=single),
                     pl.BlockSpec((1, d), lambda i, f, te, na: (0, 0))]
        args = [x, wg, wu, wd, res, gain.reshape(1, d)]
        out_specs = [pl.BlockSpec((tm, d), row, pipeline_mode=single),
                     pl.BlockSpec((tm, d), row, pipeline_mode=single)]
        out_shape = [jax.ShapeDtypeStruct((r, d), F32), jax.ShapeDtypeStruct((r, d), BF16)]
        scratch = []
    return pl.pallas_call(
        functools.partial(_ffn_kernel, nf=nf, routed=routed),
        grid_spec=pltpu.PrefetchScalarGridSpec(
            num_scalar_prefetch=2,
            grid=(r // tm, nf),
            in_specs=in_specs,
            out_specs=out_specs,
            scratch_shapes=scratch),
        out_shape=out_shape,
        compiler_params=_params("arbitrary", "arbitrary"),
        name="swiglu_routed" if routed else "swiglu_dense",
    )(tile_expert, n_active, *args)


def _tile_row(m, n, te, na):
    return (jnp.minimum(m, na[0] - 1), 0)


def _ffn_up_kernel(te_ref, na_ref, x_ref, wg_ref, wu_ref, a_ref, wgu_ref, *, tf):
    m = pl.program_id(1)

    @pl.when(m < na_ref[0])
    def _():
        @pl.when((m == 0) | (te_ref[m] != te_ref[jnp.maximum(m - 1, 0)]))
        def _():
            wgu_ref[:, :tf] = wg_ref[...].astype(BF16)
            wgu_ref[:, tf:] = wu_ref[...].astype(BF16)

        gu = _dot(x_ref[...], wgu_ref[...])
        a_ref[...] = (jax.nn.silu(gu[:, :tf]) * gu[:, tf:]).astype(BF16)


def ffn_up(x, wg, wu, tile_expert, n_active, *, tm, tf=512):
    r, d = x.shape
    ff = wg.shape[2]
    w_spec = pl.BlockSpec((None, d, tf), lambda n, m, te, na: (te[m], 0, n))
    return pl.pallas_call(
        functools.partial(_ffn_up_kernel, tf=tf),
        grid_spec=pltpu.PrefetchScalarGridSpec(
            num_scalar_prefetch=2,
            grid=(ff // tf, r // tm),
            in_specs=[pl.BlockSpec((tm, d), lambda n, m, te, na: _tile_row(m, n, te, na)), w_spec, w_spec],
            out_specs=pl.BlockSpec((tm, tf), lambda n, m, te, na: (jnp.minimum(m, na[0] - 1), n)),
            scratch_shapes=[pltpu.VMEM((d, 2 * tf), BF16)]),
        out_shape=jax.ShapeDtypeStruct((r, ff), BF16),
        compiler_params=_params("arbitrary", "arbitrary"),
        name="ffn_up",
    )(tile_expert, n_active, x, wg, wu)


def _ffn_down_kernel(te_ref, na_ref, a_ref, wd_ref, *refs, tn, nn, routed):
    if routed:
        gate_ref, out_ref, acc_ref = refs
    else:
        res_ref, g_ref, out_ref, hn_ref = refs
    m = pl.program_id(0)
    n = pl.program_id(1)

    @pl.when(m < na_ref[0])
    def _():
        cols = pl.ds(pl.multiple_of(n * tn, tn), tn)
        part = _dot(a_ref[...], wd_ref[...].astype(BF16))
        if routed:
            acc_ref[:, cols] = gate_ref[...] * part

            @pl.when(n == nn - 1)
            def _():
                for s in range(ROW_TILES):
                    out_ref[_slab(out_ref, s)] = acc_ref[:, s * LANES:(s + 1) * LANES]
        else:
            out_ref[:, cols] = part

            @pl.when(n == nn - 1)
            def _():
                h_new = res_ref[...] + out_ref[...]
                out_ref[...] = h_new
                hn_ref[...] = _rms(h_new, g_ref[...]).astype(BF16)


def ffn_down(a, wd, tile_expert, n_active, *, tm, tn=256, res=None, gain=None, row_gate=None):
    r, ff = a.shape
    d = wd.shape[2]
    nn = d // tn
    routed = row_gate is not None

    def ncol(m, n, na):
        return jnp.where(m < na[0], n, nn - 1)

    in_specs = [pl.BlockSpec((tm, ff), _tile_row),
                pl.BlockSpec((None, ff, tn), lambda m, n, te, na: (te[m], 0, ncol(m, n, na)))]
    if routed:
        in_specs.append(pl.BlockSpec((tm, 1), _tile_row))
        args = [a, wd, row_gate]
        out_specs = pl.BlockSpec((tm * ROW_TILES, LANES), _tile_row)
        out_shape = jax.ShapeDtypeStruct((r * ROW_TILES, LANES), F32)
        scratch = [pltpu.VMEM((tm, d), F32)]
    else:
        in_specs += [pl.BlockSpec((tm, d), _tile_row), pl.BlockSpec((1, d), lambda m, n, te, na: (0, 0))]
        args = [a, wd, res, gain.reshape(1, d)]
        out_specs = [pl.BlockSpec((tm, d), _tile_row), pl.BlockSpec((tm, d), _tile_row)]
        out_shape = [jax.ShapeDtypeStruct((r, d), F32), jax.ShapeDtypeStruct((r, d), BF16)]
        scratch = []
    return pl.pallas_call(
        functools.partial(_ffn_down_kernel, tn=tn, nn=nn, routed=routed),
        grid_spec=pltpu.PrefetchScalarGridSpec(
            num_scalar_prefetch=2,
            grid=(r // tm, nn),
            in_specs=in_specs,
            out_specs=out_specs,
            scratch_shapes=scratch),
        out_shape=out_shape,
        compiler_params=_params("arbitrary", "arbitrary"),
        name="ffn_down_routed" if routed else "ffn_down_dense",
    )(tile_expert, n_active, *args)


def _router_kernel(h_ref, g_ref, w_ref, o_ref, dense_ref):
    xn = _rms(h_ref[...], g_ref[...])
    logits = jnp.dot(xn, w_ref[...], preferred_element_type=F32, precision=lax.Precision.HIGHEST)
    lane = lax.broadcasted_iota(jnp.int32, logits.shape, 1)
    logits = jnp.where(lane < N_EXPERTS, logits, -jnp.inf)
    e = jnp.exp(logits - jnp.max(logits, axis=-1, keepdims=True))
    probs = e / jnp.sum(e, axis=-1, keepdims=True)
    p1 = jnp.max(probs, axis=-1, keepdims=True)
    e1 = jnp.min(jnp.where(probs == p1, lane, LANES), axis=-1, keepdims=True)
    rest = jnp.where(lane == e1, -1.0, probs)
    p2 = jnp.max(rest, axis=-1, keepdims=True)
    e2 = jnp.min(jnp.where(rest == p2, lane, LANES), axis=-1, keepdims=True)
    tot = p1 + p2
    g1 = p1 / tot
    g2 = p2 / tot
    o_ref[...] = jnp.where(lane == 0, e1.astype(F32),
                           jnp.where(lane == 1, e2.astype(F32), jnp.where(lane == 2, g1, g2)))
    dense_ref[...] = jnp.where(lane == e1, g1, jnp.where(lane == e2, g2, 0.0))


def route_top2(h, gain, w_router, tm):
    t, d = h.shape
    wpad = jnp.pad(w_router, ((0, 0), (0, LANES - N_EXPERTS)))
    return pl.pallas_call(
        _router_kernel,
        grid=(t // tm,),
        in_specs=[pl.BlockSpec((tm, d), lambda i: (i, 0)),
                  pl.BlockSpec((1, d), lambda i: (0, 0)),
                  pl.BlockSpec((d, LANES), lambda i: (0, 0))],
        out_specs=[pl.BlockSpec((tm, LANES), lambda i: (i, 0)),
                   pl.BlockSpec((tm, LANES), lambda i: (i, 0))],
        out_shape=[jax.ShapeDtypeStruct((t, LANES), F32), jax.ShapeDtypeStruct((t, LANES), F32)],
        compiler_params=_params("arbitrary"),
        name="route_top2",
    )(h, gain.reshape(1, d), wpad)


GATHER_TILE = TM_MOE


def _start_row_copies(idx_ref, first, count, src_ref, dst_ref, sem):
    def issue(j, carry):
        t = idx_ref[first + j]
        pltpu.make_async_copy(
            src_ref.at[pl.ds(pl.multiple_of(t * ROW_TILES, ROW_TILES), ROW_TILES), :],
            dst_ref.at[pl.ds(pl.multiple_of(j * ROW_TILES, ROW_TILES), ROW_TILES), :], sem).start()
        return carry

    lax.fori_loop(0, count, issue, 0, unroll=8)


def _wait_row_copies(src_ref, dst_ref, sem):
    pltpu.make_async_copy(src_ref.at[pl.ds(0, dst_ref.shape[0]), :], dst_ref, sem).wait()


def _gather_kernel(idx_ref, n_ref, src_ref, o_ref, buf_ref, sem):
    i = pl.program_id(0)

    @pl.when(i * GATHER_TILE < n_ref[0])
    def _():
        _start_row_copies(idx_ref, i * GATHER_TILE, GATHER_TILE, src_ref, buf_ref, sem)
        _wait_row_copies(src_ref, buf_ref, sem)
        _rows_from_tiles(buf_ref, o_ref)


def gather_rows(src, idx, n):
    r = idx.shape[0]
    last = lambda i, idx_ref, n_ref: (jnp.minimum(i, (n_ref[0] - 1) // GATHER_TILE), 0)
    return pl.pallas_call(
        _gather_kernel,
        grid_spec=pltpu.PrefetchScalarGridSpec(
            num_scalar_prefetch=2,
            grid=(r // GATHER_TILE,),
            in_specs=[pl.BlockSpec(memory_space=pl.ANY)],
            out_specs=pl.BlockSpec((GATHER_TILE, D_MODEL), last),
            scratch_shapes=[pltpu.VMEM((GATHER_TILE * ROW_TILES, LANES), F32), pltpu.SemaphoreType.DMA(())]),
        out_shape=jax.ShapeDtypeStruct((r, D_MODEL), BF16),
        compiler_params=_params("arbitrary"),
        name="gather_rows",
    )(idx, n, src)


def _combine_kernel(pos_ref, h_ref, o_ref, g_ref, h2_ref, hn_ref, o1_ref, o2_ref, sem1, sem2, *, tm, t):
    i = pl.program_id(0)
    _start_row_copies(pos_ref, i * tm, tm, o_ref, o1_ref, sem1)
    _start_row_copies(pos_ref, t + i * tm, tm, o_ref, o2_ref, sem2)
    _wait_row_copies(o_ref, o1_ref, sem1)
    _wait_row_copies(o_ref, o2_ref, sem2)
    for s in range(ROW_TILES):
        sl = slice(s * LANES, (s + 1) * LANES)
        h2_ref[:, sl] = h_ref[:, sl] + (o1_ref[_slab(o1_ref, s)] + o2_ref[_slab(o2_ref, s)])
    hn_ref[...] = _rms(h2_ref[...], g_ref[...]).astype(BF16)


def combine_rows(h, o, pos, gain, tm=TM_MID):
    t, d = h.shape
    tiled = pltpu.VMEM((tm * ROW_TILES, LANES), F32)
    return pl.pallas_call(
        functools.partial(_combine_kernel, tm=tm, t=t),
        grid_spec=pltpu.PrefetchScalarGridSpec(
            num_scalar_prefetch=1,
            grid=(t // tm,),
            in_specs=[pl.BlockSpec((tm, d), lambda i, pos_ref: (i, 0)),
                      pl.BlockSpec(memory_space=pl.ANY),
                      pl.BlockSpec((1, d), lambda i, pos_ref: (0, 0))],
            out_specs=[pl.BlockSpec((tm, d), lambda i, pos_ref: (i, 0)),
                       pl.BlockSpec((tm, d), lambda i, pos_ref: (i, 0))],
            scratch_shapes=[tiled, tiled, pltpu.SemaphoreType.DMA(()), pltpu.SemaphoreType.DMA(())]),
        out_shape=[jax.ShapeDtypeStruct((t, d), F32), jax.ShapeDtypeStruct((t, d), BF16)],
        compiler_params=_params("arbitrary"),
        name="combine_rows",
    )(pos, h, o, gain.reshape(1, d))


def moe_prompt(h, ffn_gain, w_router, wg, wu, wd, layer, ple_gain):
    xn = norm_rows(h, ffn_gain, F32, tiled=True)
    r, _ = route_top2(h, ffn_gain, w_router, TM_BIG)
    experts = jnp.concatenate([r[:, 0], r[:, 1]]).astype(jnp.int32)
    gates = jnp.concatenate([r[:, 2], r[:, 3]])
    onehot = (experts[:, None] == jnp.arange(N_EXPERTS)[None, :]).astype(jnp.int32)
    rank = jnp.sum(jnp.cumsum(onehot, axis=0) * onehot, axis=1) - 1
    counts = jnp.sum(onehot, axis=0)
    tiles = (counts + TM_MOE - 1) // TM_MOE
    tile_end = jnp.cumsum(tiles)
    tile_start = tile_end - tiles
    dest = jnp.sum(onehot * tile_start[None, :], axis=1) * TM_MOE + rank
    token = jnp.tile(jnp.arange(T, dtype=jnp.int32), 2)
    src_token = jnp.zeros((R_PAD,), jnp.int32).at[dest].set(token)
    row_gate = jnp.zeros((R_PAD,), F32).at[dest].set(gates).reshape(R_PAD, 1)
    n_active = tile_end[-1:].astype(jnp.int32)
    tile_ids = jnp.minimum(jnp.arange(MOE_TILES, dtype=jnp.int32), n_active[0] - 1)
    tile_expert = jnp.sum((tile_ids[:, None] >= tile_end[None, :]).astype(jnp.int32), axis=1)
    xs = gather_rows(xn, src_token, n_active * TM_MOE)
    flat = lambda w: w.reshape((-1,) + w.shape[2:])
    tile_w = tile_expert + layer * N_EXPERTS
    a = ffn_up(xs, flat(wg), flat(wu), tile_w, n_active, tm=TM_MOE)
    o = ffn_down(a, flat(wd), tile_w, n_active, tm=TM_MOE, row_gate=row_gate)
    return combine_rows(h, o, dest, ple_gain)


def _headnorm_kernel(q_ref, k_ref, qg_ref, kg_ref, qn_ref, kn_ref):
    def norm_pairs(x_ref, g_ref, o_ref, scale):
        lo = lax.broadcasted_iota(jnp.int32, (x_ref.shape[0], LANES), 1) < HEAD_DIM
        for j in range(x_ref.shape[1] // LANES):
            x = x_ref[:, j * LANES:(j + 1) * LANES]
            s = x * x
            s_lo = jnp.sum(jnp.where(lo, s, 0.0), axis=-1, keepdims=True)
            s_hi = jnp.sum(jnp.where(lo, 0.0, s), axis=-1, keepdims=True)
            ms = jnp.where(lo, s_lo, s_hi) * (1.0 / HEAD_DIM)
            y = x * lax.rsqrt(ms + NORM_EPS) * g_ref[...]
            o_ref[:, j * LANES:(j + 1) * LANES] = (y * scale).astype(o_ref.dtype)

    norm_pairs(q_ref, qg_ref, qn_ref, HEAD_DIM ** -0.5)
    norm_pairs(k_ref, kg_ref, kn_ref, 1.0)


def head_norm(qkv, q_gain, k_gain, q_dtype, tm):
    t = qkv.shape[0]
    return pl.pallas_call(
        _headnorm_kernel,
        grid=(t // tm,),
        in_specs=[pl.BlockSpec((tm, NQ), lambda i: (i, 0)),
                  pl.BlockSpec((tm, NKV), lambda i: (i, NQ // NKV)),
                  pl.BlockSpec((1, LANES), lambda i: (0, 0)),
                  pl.BlockSpec((1, LANES), lambda i: (0, 0))],
        out_specs=[pl.BlockSpec((tm, NQ), lambda i: (i, 0)),
                   pl.BlockSpec((tm, NKV), lambda i: (i, 0))],
        out_shape=[jax.ShapeDtypeStruct((t, NQ), q_dtype), jax.ShapeDtypeStruct((t, NKV), F32)],
        compiler_params=_params("arbitrary"),
        name="head_norm",
    )(qkv, qkv, jnp.tile(q_gain, 2).reshape(1, LANES), jnp.tile(k_gain, 2).reshape(1, LANES))


def _attn_kernel(slope_ref, sink_ref, q_ref, kp_ref, kc_ref, vp_ref, vc_ref, o_ref):
    rows = WINDOW
    nkeys = 2 * WINDOW
    rep = ATTN_HEADS // ATTN_KV_HEADS
    qi = lax.broadcasted_iota(jnp.int32, (rows, nkeys), 0)
    kj = lax.broadcasted_iota(jnp.int32, (rows, nkeys), 1)
    delta = WINDOW + qi - kj
    mask = (delta >= 0) & (delta <= WINDOW) & ((kj >= WINDOW) | (pl.program_id(1) > 0))
    deltaf = delta.astype(F32)
    lo_q = lax.broadcasted_iota(jnp.int32, (rows, LANES), 1) < HEAD_DIM

    for jp in range(ATTN_KV_HEADS // 2):
        ksl = slice(jp * LANES, (jp + 1) * LANES)
        k2 = jnp.concatenate([kp_ref[:, ksl], kc_ref[:, ksl]], axis=0).astype(BF16)
        v2 = jnp.concatenate([vp_ref[:, ksl], vc_ref[:, ksl]], axis=0)
        v2r = pltpu.roll(v2, HEAD_DIM, 1).astype(BF16)
        v2 = v2.astype(BF16)
        for qp in range(rep):
            col = (jp * rep + qp) * LANES
            q2 = q_ref[:, col:col + LANES]
            q2r = pltpu.roll(q2, HEAD_DIM, 1)
            g_half = qp // (rep // 2)
            outs = []
            for hh in range(2):
                h = (jp * rep + qp) * 2 + hh
                qsel = q2 if hh == g_half else q2r
                qm = jnp.where(lo_q if g_half == 0 else ~lo_q, qsel, jnp.zeros_like(qsel))
                s = _dot_nt(qm, k2) - slope_ref[h] * deltaf
                s = jnp.where(mask, s, -jnp.inf)
                sink = sink_ref[h]
                m = jnp.maximum(jnp.max(s, axis=-1, keepdims=True), sink)
                e = jnp.exp(s - m)
                probs = e / (jnp.sum(e, axis=-1, keepdims=True) + jnp.exp(sink - m))
                outs.append(_dot(probs.astype(BF16), v2 if hh == g_half else v2r))
            o_ref[:, col:col + LANES] = jnp.where(lo_q, outs[0], outs[1]).astype(o_ref.dtype)


def attention_prompt(hn, h, w_qkv, layer, q_gain, k_gain, sinks, w_o, next_gain):
    qkv = matmul_cols(hn, w_qkv, layer, NQ + 2 * NKV)
    qn, kn = head_norm(qkv, q_gain, k_gain, BF16, TM_MID)
    nblk = SEQ // WINDOW
    vcol = NQ // NKV + 1
    cur = lambda c: (lambda b, n: (b * nblk + n, c))
    prev = lambda c: (lambda b, n: (jnp.maximum(b * nblk + n - 1, 0), c))
    smem = pl.BlockSpec(memory_space=pltpu.SMEM)
    o = pl.pallas_call(
        _attn_kernel,
        grid=(BATCH, nblk),
        in_specs=[smem, smem,
                  pl.BlockSpec((WINDOW, NQ), cur(0)),
                  pl.BlockSpec((WINDOW, NKV), prev(0)),
                  pl.BlockSpec((WINDOW, NKV), cur(0)),
                  pl.BlockSpec((WINDOW, NKV), prev(vcol)),
                  pl.BlockSpec((WINDOW, NKV), cur(vcol))],
        out_specs=pl.BlockSpec((WINDOW, NQ), cur(0)),
        out_shape=jax.ShapeDtypeStruct((T, NQ), BF16),
        compiler_params=_params("arbitrary", "arbitrary"),
        name="window_attention",
    )(_alibi_slopes(), sinks, qn, kn, kn, qkv, qkv)
    h2, hn2 = proj_residual(o, w_o, layer, h, next_gain)
    kv_shape = (BATCH, WINDOW, ATTN_KV_HEADS, HEAD_DIM)
    new_k = kn.reshape(BATCH, SEQ, NKV)[:, -WINDOW:].reshape(kv_shape)
    new_v = qkv[:, NQ + NKV:].reshape(BATCH, SEQ, NKV)[:, -WINDOW:].reshape(kv_shape)
    return h2, hn2, (new_k, new_v)


SCONV_TILE = 512
SCONV_TAIL = SUBLANES


def _sconv_kernel(gb_ref, gc_ref, u_ref, gcp_ref, up_ref, w_ref, y_ref, tail_ref):
    rows = gb_ref.shape[0]
    cu = gc_ref[...] * u_ref[...]
    prev = jnp.where(pl.program_id(1) > 0, gcp_ref[...] * up_ref[...], 0.0)
    row = lax.broadcasted_iota(jnp.int32, cu.shape, 0)
    s1 = jnp.where(row == 0, prev[SCONV_TAIL - 1:], pltpu.roll(cu, 1, 0))
    s2 = jnp.where(row == 0, prev[SCONV_TAIL - 2:SCONV_TAIL - 1],
                   jnp.where(row == 1, prev[SCONV_TAIL - 1:], pltpu.roll(cu, 2, 0)))
    conv = w_ref[0:1] * s2 + w_ref[1:2] * s1 + w_ref[2:3] * cu
    y_ref[...] = (gb_ref[...] * conv).astype(y_ref.dtype)
    tail_ref[...] = cu[rows - SCONV_TAIL:]


def sconv_prompt(hn, h, w_in, conv_w, layer, w_out, next_gain):
    d = D_MODEL
    proj = matmul_cols(hn, w_in, layer, 3 * d)
    nt = SEQ // SCONV_TILE
    tpb = SCONV_TILE // SCONV_TAIL
    cur = lambda c: (lambda b, i: (b * nt + i, c))
    prev = lambda c: (lambda b, i: (jnp.maximum((b * nt + i) * tpb - 1, 0), c))
    y, tails = pl.pallas_call(
        _sconv_kernel,
        grid=(BATCH, nt),
        in_specs=[pl.BlockSpec((SCONV_TILE, d), cur(0)),
                  pl.BlockSpec((SCONV_TILE, d), cur(1)),
                  pl.BlockSpec((SCONV_TILE, d), cur(2)),
                  pl.BlockSpec((SCONV_TAIL, d), prev(1)),
                  pl.BlockSpec((SCONV_TAIL, d), prev(2)),
                  pl.BlockSpec((None, SCONV_WIDTH, d), lambda b, i: (layer, 0, 0))],
        out_specs=[pl.BlockSpec((SCONV_TILE, d), cur(0)),
                   pl.BlockSpec((None, SCONV_TAIL, d), lambda b, i: (b * nt + i, 0, 0))],
        out_shape=[jax.ShapeDtypeStruct((T, d), BF16),
                   jax.ShapeDtypeStruct((BATCH * nt, SCONV_TAIL, d), F32)],
        compiler_params=_params("arbitrary", "arbitrary"),
        name="sconv_prompt",
    )(proj, proj, proj, proj, proj, conv_w)
    h2, hn2 = proj_residual(y, w_out, layer, h, next_gain)
    new_state = tails.reshape(BATCH, nt, SCONV_TAIL, d)[:, -1, -(SCONV_WIDTH - 1):]
    return h2, hn2, new_state


XBC_CHUNK = 512
GROUP_W = D_INNER // SSM_GROUPS
HEADS_PER_GROUP = SSM_HEADS // SSM_GROUPS


def _ssd_kernel(z_ref, xc_ref, xp_ref, dt_ref, cw_ref, cb_ref, dtb_ref, alog_ref, dsk_ref, nw_ref,
                y_ref, hfin_ref, state_ref, xact_ref, acst_ref, yg_ref):
    c = pl.program_id(1)
    q = SSM_CHUNK

    @pl.when(c == 0)
    def _():
        state_ref[...] = jnp.zeros_like(state_ref)

    has_prev = c > 0
    row = lax.broadcasted_iota(jnp.int32, (q, XBC_CHUNK), 0)
    for j in range(SSM_CONV_DIM // XBC_CHUNK):
        sl = slice(j * XBC_CHUNK, (j + 1) * XBC_CHUNK)
        cur = xc_ref[:, sl]
        prv = jnp.where(has_prev, xp_ref[:, sl], 0.0)
        acc = None
        for k in range(SSM_CONV_WIDTH):
            sh = SSM_CONV_WIDTH - 1 - k
            if sh == 0:
                term = cur
            else:
                term = jnp.where(row < sh, pltpu.roll(prv, sh, 0), pltpu.roll(cur, sh, 0))
            term = cw_ref[k:k + 1, sl] * term
            acc = term if acc is None else acc + term
        xact_ref[:, sl] = jax.nn.silu(acc + cb_ref[:, sl])

    dt = _softplus(dt_ref[...] + dtb_ref[...])
    da = dt * (-jnp.exp(alog_ref[...]))
    tri = (lax.broadcasted_iota(jnp.int32, (q, q), 0) >= lax.broadcasted_iota(jnp.int32, (q, q), 1))
    a_cs = jnp.dot(tri.astype(F32), da, preferred_element_type=F32, precision=lax.Precision.HIGHEST)
    acst_ref[...] = a_cs.T
    lane = lax.broadcasted_iota(jnp.int32, (q, LANES), 1)

    def group(g, carry):
        goff = pl.multiple_of(g * GROUP_W, GROUP_W)
        xs_g = xact_ref[:, pl.ds(goff, GROUP_W)]
        bm = xact_ref[:, pl.ds(pl.multiple_of(D_INNER + g * D_STATE, D_STATE), D_STATE)]
        cm = xact_ref[:, pl.ds(pl.multiple_of(D_INNER + SSM_GROUPS * D_STATE + g * D_STATE, D_STATE), D_STATE)]
        cm_b = cm.astype(BF16)
        cb = _dot_nt(cm_b, bm.astype(BF16))
        for hh in range(HEADS_PER_GROUP):
            h = g * HEADS_PER_GROUP + hh
            sel = lane == h
            acs_col = jnp.sum(jnp.where(sel, a_cs, 0.0), axis=1, keepdims=True)
            dt_col = jnp.sum(jnp.where(sel, dt, 0.0), axis=1, keepdims=True)
            acs_row = acst_ref[pl.ds(h, 1), :]
            acs_last = acs_col[q - 1:q]
            xs = xs_g[:, hh * SSM_HEAD_DIM:(hh + 1) * SSM_HEAD_DIM]
            xdt = (xs * dt_col).astype(BF16)
            seg = acs_col - acs_row
            decay_in = jnp.where(tri, jnp.exp(jnp.where(tri, seg, 0.0)), 0.0)
            y = _dot((cb * decay_in).astype(BF16), xdt)
            h_in = state_ref[h]
            y = y + _dot_nt(cm_b, h_in.astype(BF16)) * jnp.exp(acs_col)
            decay_out = jnp.exp(acs_last - acs_col)
            st = _dot_tn(xdt, (bm * decay_out).astype(BF16))
            state_ref[h] = jnp.exp(acs_last) * h_in + st
            yg_ref[:, hh * SSM_HEAD_DIM:(hh + 1) * SSM_HEAD_DIM] = y
        y_g = yg_ref[...] + dsk_ref[:, pl.ds(goff, GROUP_W)] * xs_g
        gg = y_g * jax.nn.silu(z_ref[:, pl.ds(goff, GROUP_W)])
        gg = gg * lax.rsqrt(jnp.mean(gg * gg, axis=-1, keepdims=True) + NORM_EPS)
        y_ref[:, pl.ds(goff, GROUP_W)] = (gg * nw_ref[:, pl.ds(goff, GROUP_W)]).astype(y_ref.dtype)
        return carry

    lax.fori_loop(0, SSM_GROUPS, group, 0)

    @pl.when(c == pl.num_programs(1) - 1)
    def _():
        hfin_ref[...] = state_ref[...]


def _pad_heads(v):
    return jnp.pad(v, (0, LANES - SSM_HEADS)).reshape(1, LANES)


def ssd_prompt(hn, h, w_in, layer, conv_w, conv_b, dt_bias, a_log, d_skip, norm_w, w_out, next_gain):
    cd = SSM_CONV_DIM
    z = matmul_cols(hn, w_in, layer, D_INNER, col_off=0)
    xbc = matmul_cols(hn, w_in, layer, cd, col_off=D_INNER)
    w_dt = jnp.pad(w_in[layer, :, D_INNER + cd:], ((0, 0), (0, LANES - SSM_HEADS)))[None]
    dt_raw = matmul_cols(hn, w_dt, 0, LANES)
    dsk_full = jnp.repeat(d_skip, SSM_HEAD_DIM).reshape(1, D_INNER)
    nc = SEQ // SSM_CHUNK
    cur = lambda b, c: (b * nc + c, 0)
    whole = lambda b, c: (0, 0)
    y, h_fin = pl.pallas_call(
        _ssd_kernel,
        grid=(BATCH, nc),
        in_specs=[pl.BlockSpec((SSM_CHUNK, D_INNER), cur),
                  pl.BlockSpec((SSM_CHUNK, cd), cur),
                  pl.BlockSpec((SSM_CHUNK, cd), lambda b, c: (jnp.maximum(b * nc + c - 1, 0), 0)),
                  pl.BlockSpec((SSM_CHUNK, LANES), cur),
                  pl.BlockSpec((SSM_CONV_WIDTH, cd), whole),
                  pl.BlockSpec((1, cd), whole),
                  pl.BlockSpec((1, LANES), whole),
                  pl.BlockSpec((1, LANES), whole),
                  pl.BlockSpec((1, D_INNER), whole),
                  pl.BlockSpec((1, D_INNER), whole)],
        out_specs=[pl.BlockSpec((SSM_CHUNK, D_INNER), cur),
                   pl.BlockSpec((None, SSM_HEADS, SSM_HEAD_DIM, D_STATE), lambda b, c: (b, 0, 0, 0))],
        out_shape=[jax.ShapeDtypeStruct((T, D_INNER), BF16),
                   jax.ShapeDtypeStruct((BATCH, SSM_HEADS, SSM_HEAD_DIM, D_STATE), F32)],
        scratch_shapes=[pltpu.VMEM((SSM_HEADS, SSM_HEAD_DIM, D_STATE), F32),
                        pltpu.VMEM((SSM_CHUNK, cd), F32),
                        pltpu.VMEM((LANES, SSM_CHUNK), F32),
                        pltpu.VMEM((SSM_CHUNK, GROUP_W), F32)],
        compiler_params=_params("arbitrary", "arbitrary"),
        name="ssd_prompt",
    )(z, xbc, xbc, dt_raw, conv_w, conv_b.reshape(1, cd), _pad_heads(dt_bias), _pad_heads(a_log), dsk_full,
      norm_w.reshape(1, D_INNER))
    h2, hn2 = proj_residual(y, w_out, layer, h, next_gain)
    new_conv = xbc.reshape(BATCH, SEQ, cd)[:, -(SSM_CONV_WIDTH - 1):]
    return h2, hn2, (h_fin, new_conv)


def _split(x):
    hi = x.astype(BF16)
    lo = (x - hi.astype(F32)).astype(BF16)
    return hi, lo


def _hp_kernel(*refs, nk, x_mode, epi):
    refs = list(refs)
    if x_mode == "norm":
        x_ref, g_ref = refs[:2]
        refs = refs[2:]
        x = _rms(x_ref[...], g_ref[...])
    elif x_mode == "swiglu":
        g_ref, u_ref = refs[:2]
        refs = refs[2:]
        x = jax.nn.silu(g_ref[...]) * u_ref[...]
    else:
        x = refs[0][...]
        refs = refs[1:]
    w_ref = refs[0]
    refs = refs[1:]
    if epi == "gated":
        res_ref, pp_ref, o_ref = refs
    elif epi == "res":
        res_ref, o_ref = refs
    else:
        (o_ref,) = refs
    k = pl.program_id(2)
    rows = x.shape[0]
    x_hi, x_lo = _split(x)
    w_hi, w_lo = _split(w_ref[...])
    both = _dot(jnp.concatenate([x_hi, x_lo], axis=0), w_hi)
    part = both[:rows] + both[rows:] + _dot(x_hi, w_lo)

    @pl.when(k == 0)
    def _():
        o_ref[...] = part

    @pl.when(k > 0)
    def _():
        o_ref[...] += part

    if epi is not None:
        @pl.when(k == nk - 1)
        def _():
            acc = o_ref[...]
            if epi == "gated":
                acc = jax.nn.sigmoid(acc) * pp_ref[...]
            o_ref[...] = res_ref[...] + acc


def hp_linear(w, widx, n_out, *, x=None, norm=None, swiglu=None, col_off=0, n_e=1, tk, tn, res=None, pp=None):
    kdim = w.shape[1]
    nk = kdim // tk
    off = col_off // tn
    assert kdim % tk == 0 and n_out % tn == 0 and col_off % tn == 0
    if norm is not None:
        assert nk == 1
        x_mode = "norm"
        args = [norm[0], norm[1].reshape(1, kdim)]
        in_specs = [pl.BlockSpec((NS, tk), lambda e, n, k: (0, k)),
                    pl.BlockSpec((1, tk), lambda e, n, k: (0, k))]
    elif swiglu is not None:
        x_mode = "swiglu"
        args = list(swiglu)
        in_specs = [pl.BlockSpec((None, NS, tk), lambda e, n, k: (e, 0, k))] * 2
    else:
        x_mode = "plain"
        args = [x]
        in_specs = [pl.BlockSpec((NS, tk), lambda e, n, k: (0, k))]
    args.append(w)
    in_specs.append(pl.BlockSpec((None, tk, tn), lambda e, n, k: (widx + e, k, n + off)))
    epi = None
    if res is not None:
        epi = "res"
        args.append(res)
        in_specs.append(pl.BlockSpec((NS, tn), lambda e, n, k: (0, n)))
        if pp is not None:
            epi = "gated"
            args.append(pp)
            in_specs.append(pl.BlockSpec((NS, tn), lambda e, n, k: (0, n)))
    return pl.pallas_call(
        functools.partial(_hp_kernel, nk=nk, x_mode=x_mode, epi=epi),
        grid=(n_e, n_out // tn, nk),
        in_specs=in_specs,
        out_specs=pl.BlockSpec((None, NS, tn), lambda e, n, k: (e, 0, n)),
        out_shape=jax.ShapeDtypeStruct((n_e, NS, n_out), F32),
        compiler_params=_params("arbitrary", "arbitrary", "arbitrary"),
        name="hp_linear_" + x_mode,
    )(*args)


def _attn_step_kernel(slope_ref, sink_ref, q_ref, kc_ref, vc_ref, kn_ref, vn_ref, o_ref):
    rep = ATTN_HEADS // ATTN_KV_HEADS
    lo = lax.broadcasted_iota(jnp.int32, (1, LANES), 1) < HEAD_DIM
    delta = (WINDOW - lax.broadcasted_iota(jnp.int32, (WINDOW, 1), 0)).astype(F32)
    for jp in range(ATTN_KV_HEADS // 2):
        ksl = slice(jp * LANES, (jp + 1) * LANES)
        k2 = kc_ref[:, ksl]
        v2 = vc_ref[:, ksl]
        kn2 = kn_ref[:, ksl]
        vn2 = vn_ref[:, ksl]
        for qp in range(rep):
            col = (jp * rep + qp) * LANES
            q2 = q_ref[:, col:col + LANES]
            q2r = pltpu.roll(q2, HEAD_DIM, 1)
            g_half = qp // (rep // 2)
            kv_lanes = lo if g_half == 0 else ~lo
            out = jnp.zeros((1, LANES), F32)
            for hh in range(2):
                h = (jp * rep + qp) * 2 + hh
                qm = jnp.where(kv_lanes, q2 if hh == g_half else q2r, 0.0)
                s = jnp.sum(k2 * qm, axis=1, keepdims=True) - slope_ref[h] * delta
                s_new = jnp.sum(kn2 * qm, axis=1, keepdims=True)
                sink = sink_ref[h]
                m = jnp.maximum(jnp.maximum(jnp.max(s, axis=0, keepdims=True), s_new), sink)
                e = jnp.exp(s - m)
                e_new = jnp.exp(s_new - m)
                denom = jnp.sum(e, axis=0, keepdims=True) + e_new + jnp.exp(sink - m)
                o = jnp.sum((e / denom) * v2, axis=0, keepdims=True) + (e_new / denom) * vn2
                if hh != g_half:
                    o = pltpu.roll(o, HEAD_DIM, 1)
                out = jnp.where(lo if hh == 0 else ~lo, o, out)
            o_ref[:, col:col + LANES] = out


def _sconv_step_kernel(proj_ref, p_ref, w_ref, y_ref, cu_ref):
    d = D_MODEL
    cu = proj_ref[:, d:2 * d] * proj_ref[:, 2 * d:]
    conv = w_ref[0:1] * p_ref[0] + w_ref[1:2] * p_ref[1] + w_ref[2:3] * cu
    y_ref[...] = proj_ref[:, :d] * conv
    cu_ref[...] = cu


def _ssd_step_pre_kernel(xc_ref, p_ref, dt_ref, cw_ref, cb_ref, dtb_ref, alog_ref, xact_ref, dt_out_ref, dec_ref):
    acc = cw_ref[0:1] * p_ref[0] + cw_ref[1:2] * p_ref[1] + cw_ref[2:3] * p_ref[2] + cw_ref[3:4] * xc_ref[...]
    xact_ref[...] = jax.nn.silu(acc + cb_ref[...])
    dt = _softplus(dt_ref[...] + dtb_ref[...])
    dt_out_ref[...] = dt
    dec_ref[...] = jnp.exp(dt * (-jnp.exp(alog_ref[...])))


def _ssd_step_kernel(dt_ref, dec_ref, dsk_ref, xt_ref, bm_ref, cm_ref, h0_ref, yt_ref, h1_ref):
    b = pl.program_id(0)
    xt = xt_ref[...]
    lane = lax.broadcasted_iota(jnp.int32, xt.shape, 1)

    def head(h, yt):
        g = h // HEADS_PER_GROUP
        sel = lane == h
        x_col = jnp.sum(jnp.where(sel, xt, 0.0), axis=1, keepdims=True)
        bm = bm_ref[pl.ds(g, 1), :]
        cm = cm_ref[pl.ds(g, 1), :]
        h1 = dec_ref[b, h] * h0_ref[h] + (dt_ref[b, h] * x_col) * bm
        h1_ref[h] = h1
        y_col = jnp.sum(h1 * cm, axis=1, keepdims=True) + dsk_ref[h] * x_col
        return jnp.where(sel, y_col, yt)

    yt_ref[...] = lax.fori_loop(0, SSM_HEADS, head, jnp.zeros(xt.shape, F32))


def _ssd_gate_kernel(y_ref, z_ref, nw_ref, o_ref):
    for g in range(SSM_GROUPS):
        sl = slice(g * GROUP_W, (g + 1) * GROUP_W)
        gg = y_ref[:, sl] * jax.nn.silu(z_ref[:, sl])
        gg = gg * lax.rsqrt(jnp.mean(gg * gg, axis=-1, keepdims=True) + NORM_EPS)
        o_ref[:, sl] = gg * nw_ref[:, sl]


def _moe_mix_kernel(h_ref, gates_ref, d_ref, o_ref):
    lane = lax.broadcasted_iota(jnp.int32, gates_ref.shape, 1)
    y = jnp.zeros(h_ref.shape, F32)
    for e in range(N_EXPERTS):
        gate = jnp.sum(jnp.where(lane == e, gates_ref[...], 0.0), axis=1, keepdims=True)
        y = y + gate * d_ref[e]
    o_ref[...] = h_ref[...] + y


def _whole(shape):
    return pl.BlockSpec(shape, lambda *_: (0,) * len(shape))


def _small_call(body, out_shapes, *args, name):
    multi = isinstance(out_shapes, (list, tuple))
    outs = list(out_shapes) if multi else [out_shapes]
    res = pl.pallas_call(
        body,
        grid=(1,),
        in_specs=[_whole(a.shape) for a in args],
        out_specs=[_whole(o.shape) for o in outs],
        out_shape=outs,
        compiler_params=_params("arbitrary"),
        name=name,
    )(*args)
    return res if multi else res[0]


def _sds(*shape):
    return jax.ShapeDtypeStruct(shape, F32)


def attention_sample(h, norm_gain, cache_k, cache_v, w_qkv, layer, q_gain, k_gain, sinks, w_o):
    qkv = hp_linear(w_qkv, layer, NQ + 2 * NKV, norm=(h, norm_gain), tk=D_MODEL, tn=512)[0]
    qn, kn = head_norm(qkv, q_gain, k_gain, F32, NS)
    v_new = qkv[:, NQ + NKV:]
    smem = pl.BlockSpec(memory_space=pltpu.SMEM)
    rowsq = pl.BlockSpec((None, 1, NQ), lambda b: (b, 0, 0))
    rowsk = pl.BlockSpec((None, 1, NKV), lambda b: (b, 0, 0))
    cache = pl.BlockSpec((None, WINDOW, NKV), lambda b: (b, 0, 0))
    o = pl.pallas_call(
        _attn_step_kernel,
        grid=(NS,),
        in_specs=[smem, smem, rowsq, cache, cache, rowsk, rowsk],
        out_specs=rowsq,
        out_shape=_sds(NS, 1, NQ),
        compiler_params=_params("arbitrary"),
        name="attention_sample",
    )(_alibi_slopes(), sinks, qn.reshape(NS, 1, NQ), cache_k.reshape(NS, WINDOW, NKV),
      cache_v.reshape(NS, WINDOW, NKV), kn.reshape(NS, 1, NKV), v_new.reshape(NS, 1, NKV))
    h2 = hp_linear(w_o, layer, D_MODEL, x=o.reshape(NS, NQ), tk=1024, tn=1024, res=h)[0]
    kv_row = (NS, 1, ATTN_KV_HEADS, HEAD_DIM)
    new_k = jnp.concatenate([cache_k[:, 1:], kn.reshape(kv_row)], axis=1)
    new_v = jnp.concatenate([cache_v[:, 1:], v_new.reshape(kv_row)], axis=1)
    return h2, (new_k, new_v)


def sconv_sample(h, norm_gain, past, w_in, conv_w, layer, w_out):
    d = D_MODEL
    proj = hp_linear(w_in, layer, 3 * d, norm=(h, norm_gain), tk=d, tn=512)[0]
    y, cu = _small_call(_sconv_step_kernel, [_sds(NS, d), _sds(NS, d)],
                        proj, past.transpose(1, 0, 2), conv_w[layer], name="sconv_sample")
    h2 = hp_linear(w_out, layer, d, x=y, tk=1024, tn=1024, res=h)[0]
    return h2, jnp.concatenate([past[:, 1:], cu[:, None]], axis=1)


def ssd_sample(h, norm_gain, conv_past, h0, w_in, layer, conv_w, conv_b, dt_bias, a_log, d_skip, norm_w, w_out):
    cd = SSM_CONV_DIM
    z = hp_linear(w_in, layer, D_INNER, norm=(h, norm_gain), tk=D_MODEL, tn=512)[0]
    xbc = hp_linear(w_in, layer, cd, norm=(h, norm_gain), col_off=D_INNER, tk=D_MODEL, tn=512)[0]
    w_dt = jnp.pad(w_in[layer, :, D_INNER + cd:], ((0, 0), (0, LANES - SSM_HEADS)))[None]
    dt_raw = hp_linear(w_dt, 0, LANES, norm=(h, norm_gain), tk=D_MODEL, tn=LANES)[0]
    xact, dt, dec = _small_call(
        _ssd_step_pre_kernel, [_sds(NS, cd), _sds(NS, LANES), _sds(NS, LANES)],
        xbc, conv_past.transpose(1, 0, 2), dt_raw, conv_w, conv_b.reshape(1, cd), _pad_heads(dt_bias),
        _pad_heads(a_log), name="ssd_sample_pre")
    xt = xact[:, :D_INNER].reshape(NS, SSM_HEADS, SSM_HEAD_DIM).transpose(0, 2, 1)
    bm = xact[:, D_INNER:D_INNER + SSM_GROUPS * D_STATE].reshape(NS, SSM_GROUPS, D_STATE)
    cm = xact[:, D_INNER + SSM_GROUPS * D_STATE:].reshape(NS, SSM_GROUPS, D_STATE)
    smem = pl.BlockSpec(memory_space=pltpu.SMEM)
    per_b = lambda *shape: pl.BlockSpec((None,) + shape, lambda b: (b,) + (0,) * len(shape))
    yt, h1 = pl.pallas_call(
        _ssd_step_kernel,
        grid=(NS,),
        in_specs=[smem, smem, smem,
                  per_b(SSM_HEAD_DIM, SSM_HEADS), per_b(SSM_GROUPS, D_STATE), per_b(SSM_GROUPS, D_STATE),
                  per_b(SSM_HEADS, SSM_HEAD_DIM, D_STATE)],
        out_specs=[per_b(SSM_HEAD_DIM, SSM_HEADS), per_b(SSM_HEADS, SSM_HEAD_DIM, D_STATE)],
        out_shape=[_sds(NS, SSM_HEAD_DIM, SSM_HEADS), _sds(NS, SSM_HEADS, SSM_HEAD_DIM, D_STATE)],
        compiler_params=_params("arbitrary"),
        name="ssd_sample_step",
    )(dt[:, :SSM_HEADS], dec[:, :SSM_HEADS], d_skip, xt, bm, cm, h0)
    y = yt.transpose(0, 2, 1).reshape(NS, D_INNER)
    yn = _small_call(_ssd_gate_kernel, _sds(NS, D_INNER), y, z, norm_w.reshape(1, D_INNER), name="ssd_sample_gate")
    h2 = hp_linear(w_out, layer, D_MODEL, x=yn, tk=1024, tn=1024, res=h)[0]
    return h2, (h1, jnp.concatenate([conv_past[:, 1:], xbc[:, None]], axis=1))


def ffn_sample(h, norm_gain, wg, wu, wd, widx, n_e):
    g = hp_linear(wg, widx, D_FF, norm=(h, norm_gain), n_e=n_e, tk=D_MODEL, tn=512)
    u = hp_linear(wu, widx, D_FF, norm=(h, norm_gain), n_e=n_e, tk=D_MODEL, tn=512)
    return g, u


def ple_sample(h, p, ple_gain, w_gate, w_proj, layer):
    pp = hp_linear(w_proj, layer, D_MODEL, x=p, tk=PLE_DIM, tn=D_MODEL)[0]
    return hp_linear(w_gate, layer, D_MODEL, norm=(h, ple_gain), tk=D_MODEL, tn=512, res=h, pp=pp)[0]


def kernel(x_prompt, x_sample, cache_k_win, cache_v_win, state_short_conv, state_ssm, state_ssm_conv, p_prompt, p_sample, norm_mixer, norm_ffn, attn_w_qkv, attn_q_norm, attn_k_norm, attn_sinks, attn_w_out, sconv_w_in, sconv_conv_w, sconv_w_out, ssm_w_in, ssm_conv_w, ssm_conv_b, ssm_dt_bias, ssm_a_log, ssm_d_skip, ssm_norm, ssm_w_out, ffn_w_gate, ffn_w_up, ffn_w_down, moe_w_router, moe_w_gate, moe_w_up, moe_w_down, ple_norm, ple_w_gate, ple_w_proj):
    flat_e = lambda w: w.reshape((-1,) + w.shape[2:])
    moe_wg, moe_wu, moe_wd = flat_e(moe_w_gate), flat_e(moe_w_up), flat_e(moe_w_down)

    def dense_tiles(tm, widx):
        return jnp.full((T // tm,), widx, jnp.int32), jnp.full((1,), T // tm, jnp.int32)

    h = x_prompt.reshape(T, D_MODEL)
    hn = norm_rows(h, norm_mixer[0], BF16)
    k_p, v_p, sc_p, ssm_p, ssmc_p = [], [], [], [], []
    for i in range(DEPTH):
        kind, j = i % 3, i // 3
        if kind == 0:
            h, hn, new = attention_prompt(hn, h, attn_w_qkv, j, attn_q_norm[j], attn_k_norm[j], attn_sinks[j],
                                          attn_w_out, norm_ffn[i])
            k_p.append(new[0]); v_p.append(new[1])
        elif kind == 1:
            h, hn, new = sconv_prompt(hn, h, sconv_w_in, sconv_conv_w, j, sconv_w_out, norm_ffn[i])
            sc_p.append(new)
        else:
            h, hn, new = ssd_prompt(hn, h, ssm_w_in, j, ssm_conv_w[j], ssm_conv_b[j], ssm_dt_bias[j],
                                    ssm_a_log[j], ssm_d_skip[j], ssm_norm[j], ssm_w_out, norm_ffn[i])
            ssm_p.append(new[0]); ssmc_p.append(new[1])
        c = i // 2
        if i % 2 == 0:
            a = ffn_up(hn, ffn_w_gate, ffn_w_up, *dense_tiles(TM_BIG, c), tm=TM_BIG)
            h, hn = ffn_down(a, ffn_w_down, *dense_tiles(TM_MID, c), tm=TM_MID, res=h, gain=ple_norm[i])
        else:
            h, hn = moe_prompt(h, norm_ffn[i], moe_w_router[c], moe_w_gate, moe_w_up, moe_w_down, c, ple_norm[i])
        next_gain = norm_mixer[i + 1] if i + 1 < DEPTH else None
        h, hn = proj_residual(hn, ple_w_gate, i, h, next_gain, p=p_prompt[i].reshape(T, PLE_DIM), wp=ple_w_proj)
    y_p = h.reshape(BATCH, SEQ, D_MODEL)

    s = x_sample.reshape(NS, D_MODEL)
    k_s, v_s, sc_s, ssm_s, ssmc_s = [], [], [], [], []
    for i in range(DEPTH):
        kind, j = i % 3, i // 3
        if kind == 0:
            s, new = attention_sample(s, norm_mixer[i], cache_k_win[j], cache_v_win[j], attn_w_qkv, j,
                                      attn_q_norm[j], attn_k_norm[j], attn_sinks[j], attn_w_out)
            k_s.append(new[0]); v_s.append(new[1])
        elif kind == 1:
            s, new = sconv_sample(s, norm_mixer[i], state_short_conv[j], sconv_w_in, sconv_conv_w, j, sconv_w_out)
            sc_s.append(new)
        else:
            s, new = ssd_sample(s, norm_mixer[i], state_ssm_conv[j], state_ssm[j], ssm_w_in, j, ssm_conv_w[j],
                                ssm_conv_b[j], ssm_dt_bias[j], ssm_a_log[j], ssm_d_skip[j], ssm_norm[j], ssm_w_out)
            ssm_s.append(new[0]); ssmc_s.append(new[1])
        c = i // 2
        if i % 2 == 0:
            g, u = ffn_sample(s, norm_ffn[i], ffn_w_gate, ffn_w_up, ffn_w_down, c, 1)
            s = hp_linear(ffn_w_down, c, D_MODEL, swiglu=(g, u), tk=1024, tn=1024, res=s)[0]
        else:
            _, gates = route_top2(s, norm_ffn[i], moe_w_router[c], NS)
            g, u = ffn_sample(s, norm_ffn[i], moe_wg, moe_wu, moe_wd, c * N_EXPERTS, N_EXPERTS)
            d_e = hp_linear(moe_wd, c * N_EXPERTS, D_MODEL, swiglu=(g, u), n_e=N_EXPERTS, tk=1024, tn=1024)
            s = _small_call(_moe_mix_kernel, _sds(NS, D_MODEL), s, gates, d_e, name="moe_mix_sample")
        s = ple_sample(s, p_sample[i].reshape(NS, PLE_DIM), ple_norm[i], ple_w_gate, ple_w_proj, i)
    y_s = s.reshape(NS, 1, D_MODEL)

    return (y_p, y_s, jnp.stack(k_p), jnp.stack(v_p), jnp.stack(k_s), jnp.stack(v_s),
            jnp.stack(sc_p), jnp.stack(sc_s), jnp.stack(ssm_p), jnp.stack(ssm_s),
            jnp.stack(ssmc_p), jnp.stack(ssmc_s))
```

```python
import functools

import jax
import jax.numpy as jnp
from jax import lax
from jax.experimental import pallas as pl
from jax.experimental.pallas import tpu as pltpu

F32 = jnp.float32
BF16 = jnp.bfloat16

D_MODEL = 2048
BATCH = 2
SEQ = 4096
DEPTH = 4
DEC_BATCH = 32
PAST_LEN = 16384
ATTN_HEADS = 32
ATTN_KV_HEADS = 8
HEAD_DIM = 64
WINDOW = 128
SCONV_WIDTH = 3
D_INNER = 2 * D_MODEL
SSM_HEAD_DIM = 64
SSM_HEADS = D_INNER // SSM_HEAD_DIM
SSM_GROUPS = 8
D_STATE = 128
SSM_CONV_WIDTH = 4
SSM_CHUNK = 128
SSM_CONV_DIM = D_INNER + 2 * SSM_GROUPS * D_STATE
D_FF = 7 * D_MODEL // 2
N_EXPERTS = 8
PLE_DIM = 256
NORM_EPS = 1e-6
NQ = ATTN_HEADS * HEAD_DIM
NKV = ATTN_KV_HEADS * HEAD_DIM
assert PAST_LEN >= WINDOW

LANES = 128
SUBLANES = 8
T = BATCH * SEQ
NS = DEC_BATCH
TM_BIG = 1024
TM_MID = 512
TM_MOE = 512
MOE_TILES = (2 * T + N_EXPERTS * (TM_MOE - 1)) // TM_MOE
R_PAD = MOE_TILES * TM_MOE
ROW_TILES = D_MODEL // LANES
VMEM_LIMIT = 56 * 1024 * 1024


def _params(*sem):
    return pltpu.CompilerParams(dimension_semantics=sem, vmem_limit_bytes=VMEM_LIMIT)


def _rms(x, g):
    return x * lax.rsqrt(jnp.mean(x * x, axis=-1, keepdims=True) + NORM_EPS) * g


def _dot(a, b):
    return jnp.dot(a, b, preferred_element_type=F32)


def _dot_nt(a, b):
    return lax.dot_general(a, b, (((1,), (1,)), ((), ())), preferred_element_type=F32)


def _dot_tn(a, b):
    return lax.dot_general(a, b, (((0,), (0,)), ((), ())), preferred_element_type=F32)


def _softplus(x):
    return jnp.maximum(x, 0.0) + jnp.log1p(jnp.exp(-jnp.abs(x)))


def _alibi_slopes():
    return jnp.exp2(-8.0 * jnp.arange(1, ATTN_HEADS + 1, dtype=F32) / ATTN_HEADS)


def _slab(tiled_ref, s):
    return (pl.ds(s, tiled_ref.shape[0] // ROW_TILES, stride=ROW_TILES), slice(None))


def _rows_from_tiles(tiled_ref, rows_ref):
    for s in range(ROW_TILES):
        rows_ref[:, s * LANES:(s + 1) * LANES] = tiled_ref[_slab(tiled_ref, s)].astype(rows_ref.dtype)


def _norm_kernel(x_ref, g_ref, o_ref, *, tiled):
    y = _rms(x_ref[...], g_ref[...]).astype(o_ref.dtype)
    if tiled:
        for s in range(ROW_TILES):
            o_ref[_slab(o_ref, s)] = y[:, s * LANES:(s + 1) * LANES]
    else:
        o_ref[...] = y


def norm_rows(x, gain, dtype, tiled=False, tm=TM_BIG):
    t, d = x.shape
    out_block, out_full = ((tm * ROW_TILES, LANES), (t * ROW_TILES, LANES)) if tiled else ((tm, d), (t, d))
    return pl.pallas_call(
        functools.partial(_norm_kernel, tiled=tiled),
        grid=(t // tm,),
        in_specs=[pl.BlockSpec((tm, d), lambda i: (i, 0)),
                  pl.BlockSpec((1, d), lambda i: (0, 0))],
        out_specs=pl.BlockSpec(out_block, lambda i: (i, 0)),
        out_shape=jax.ShapeDtypeStruct(out_full, dtype),
        compiler_params=_params("arbitrary"),
        name="norm_rows",
    )(x, gain.reshape(1, d))


def _mm_kernel(x_ref, w_ref, o_ref, wb_ref):
    @pl.when(pl.program_id(1) == 0)
    def _():
        wb_ref[...] = w_ref[...].astype(BF16)

    o_ref[...] = _dot(x_ref[...], wb_ref[...])


def matmul_cols(x, w, layer, n_out, col_off=0, tn=1024, tm=TM_BIG):
    t, k = x.shape
    tn = min(tn, n_out)
    off = col_off // tn
    assert col_off % tn == 0 and n_out % tn == 0
    return pl.pallas_call(
        _mm_kernel,
        grid=(n_out // tn, t // tm),
        in_specs=[pl.BlockSpec((tm, k), lambda n, m: (m, 0)),
                  pl.BlockSpec((None, k, tn), lambda n, m: (layer, 0, n + off))],
        out_specs=pl.BlockSpec((tm, tn), lambda n, m: (m, n)),
        out_shape=jax.ShapeDtypeStruct((t, n_out), F32),
        scratch_shapes=[pltpu.VMEM((k, tn), BF16)],
        compiler_params=_params("arbitrary", "arbitrary"),
        name="matmul_cols",
    )(x, w)


EPILOGUE_ROWS = 256


def _proj_kernel(*refs, tn, nn, gated, with_norm):
    x_ref, w_ref, res_ref = refs[:3]
    refs = refs[3:]
    if gated:
        p_ref, wp_ref = refs[:2]
        refs = refs[2:]
    if with_norm:
        g_ref, h_ref, hn_ref = refs
    else:
        (h_ref,) = refs
    n = pl.program_id(1)
    h_ref[:, pl.ds(pl.multiple_of(n * tn, tn), tn)] = _dot(x_ref[...], w_ref[...].astype(BF16))

    @pl.when(n == nn - 1)
    def _():
        for r in range(h_ref.shape[0] // EPILOGUE_ROWS):
            rows = slice(r * EPILOGUE_ROWS, (r + 1) * EPILOGUE_ROWS)
            upd = h_ref[rows, :]
            if gated:
                upd = jax.nn.sigmoid(upd) * _dot(p_ref[rows, :].astype(BF16), wp_ref[...].astype(BF16))
            h_new = res_ref[rows, :] + upd
            h_ref[rows, :] = h_new
            if with_norm:
                hn_ref[rows, :] = _rms(h_new, g_ref[...]).astype(BF16)


def proj_residual(x, w, layer, res, gain=None, p=None, wp=None, tm=TM_BIG, tn=512):
    t, kdim = x.shape
    d = w.shape[2]
    nn = d // tn
    gated = p is not None
    with_norm = gain is not None
    single = pl.Buffered(1)
    in_specs = [pl.BlockSpec((tm, kdim), lambda m, k: (m, 0)),
                pl.BlockSpec((None, kdim, tn), lambda m, k: (layer, 0, k)),
                pl.BlockSpec((tm, d), lambda m, k: (m, 0), pipeline_mode=single)]
    args = [x, w, res]
    if gated:
        in_specs += [pl.BlockSpec((tm, p.shape[1]), lambda m, k: (m, 0)),
                     pl.BlockSpec((None,) + wp.shape[1:], lambda m, k: (layer, 0, 0))]
        args += [p, wp]
    out_specs = [pl.BlockSpec((tm, d), lambda m, k: (m, 0), pipeline_mode=single)]
    out_shape = [jax.ShapeDtypeStruct((t, d), F32)]
    if with_norm:
        in_specs.append(pl.BlockSpec((1, d), lambda m, k: (0, 0)))
        args.append(gain.reshape(1, d))
        out_specs.append(pl.BlockSpec((tm, d), lambda m, k: (m, 0), pipeline_mode=single))
        out_shape.append(jax.ShapeDtypeStruct((t, d), BF16))
    out = pl.pallas_call(
        functools.partial(_proj_kernel, tn=tn, nn=nn, gated=gated, with_norm=with_norm),
        grid=(t // tm, nn),
        in_specs=in_specs,
        out_specs=out_specs,
        out_shape=out_shape,
        compiler_params=_params("arbitrary", "arbitrary"),
        name="proj_gated" if gated else "proj_residual",
    )(*args)
    return (out[0], out[1]) if with_norm else (out[0], None)


def _tile_row(m, n, te, na):
    return (jnp.minimum(m, na[0] - 1), 0)


def _ffn_up_kernel(te_ref, na_ref, x_ref, wg_ref, wu_ref, a_ref, wgu_ref, *, tf):
    m = pl.program_id(1)

    @pl.when(m < na_ref[0])
    def _():
        @pl.when((m == 0) | (te_ref[m] != te_ref[jnp.maximum(m - 1, 0)]))
        def _():
            wgu_ref[:, :tf] = wg_ref[...].astype(BF16)
            wgu_ref[:, tf:] = wu_ref[...].astype(BF16)

        gu = _dot(x_ref[...], wgu_ref[...])
        a_ref[...] = (jax.nn.silu(gu[:, :tf]) * gu[:, tf:]).astype(BF16)


def ffn_up(x, wg, wu, tile_expert, n_active, *, tm, tf=512):
    r, d = x.shape
    ff = wg.shape[2]
    w_spec = pl.BlockSpec((None, d, tf), lambda n, m, te, na: (te[m], 0, n))
    return pl.pallas_call(
        functools.partial(_ffn_up_kernel, tf=tf),
        grid_spec=pltpu.PrefetchScalarGridSpec(
            num_scalar_prefetch=2,
            grid=(ff // tf, r // tm),
            in_specs=[pl.BlockSpec((tm, d), lambda n, m, te, na: _tile_row(m, n, te, na)), w_spec, w_spec],
            out_specs=pl.BlockSpec((tm, tf), lambda n, m, te, na: (jnp.minimum(m, na[0] - 1), n)),
            scratch_shapes=[pltpu.VMEM((d, 2 * tf), BF16)]),
        out_shape=jax.ShapeDtypeStruct((r, ff), BF16),
        compiler_params=_params("arbitrary", "arbitrary"),
        name="ffn_up",
    )(tile_expert, n_active, x, wg, wu)


def _ffn_down_kernel(te_ref, na_ref, a_ref, wd_ref, *refs, tn, nn, routed):
    if routed:
        gate_ref, out_ref, acc_ref = refs
    else:
        res_ref, g_ref, out_ref, hn_ref = refs
    m = pl.program_id(0)
    n = pl.program_id(1)

    @pl.when(m < na_ref[0])
    def _():
        cols = pl.ds(pl.multiple_of(n * tn, tn), tn)
        part = _dot(a_ref[...], wd_ref[...].astype(BF16))
        if routed:
            acc_ref[:, cols] = gate_ref[...] * part

            @pl.when(n == nn - 1)
            def _():
                for s in range(ROW_TILES):
                    out_ref[_slab(out_ref, s)] = acc_ref[:, s * LANES:(s + 1) * LANES]
        else:
            out_ref[:, cols] = part

            @pl.when(n == nn - 1)
            def _():
                for r in range(out_ref.shape[0] // EPILOGUE_ROWS):
                    rows = slice(r * EPILOGUE_ROWS, (r + 1) * EPILOGUE_ROWS)
                    h_new = res_ref[rows, :] + out_ref[rows, :]
                    out_ref[rows, :] = h_new
                    hn_ref[rows, :] = _rms(h_new, g_ref[...]).astype(BF16)


def ffn_down(a, wd, tile_expert, n_active, *, tm, tn=256, res=None, gain=None, row_gate=None):
    r, ff = a.shape
    d = wd.shape[2]
    nn = d // tn
    routed = row_gate is not None

    def ncol(m, n, na):
        return jnp.where(m < na[0], n, nn - 1)

    w_spec = pl.BlockSpec((None, ff, tn), lambda m, n, te, na: (te[m], 0, ncol(m, n, na)))
    if routed:
        in_specs = [pl.BlockSpec((tm, ff), _tile_row), w_spec, pl.BlockSpec((tm, 1), _tile_row)]
        args = [a, wd, row_gate]
        out_specs = pl.BlockSpec((tm * ROW_TILES, LANES), _tile_row)
        out_shape = jax.ShapeDtypeStruct((r * ROW_TILES, LANES), F32)
        scratch = [pltpu.VMEM((tm, d), F32)]
    else:
        single = pl.Buffered(1)
        in_specs = [pl.BlockSpec((tm, ff), _tile_row, pipeline_mode=single), w_spec,
                    pl.BlockSpec((tm, d), _tile_row, pipeline_mode=single),
                    pl.BlockSpec((1, d), lambda m, n, te, na: (0, 0))]
        args = [a, wd, res, gain.reshape(1, d)]
        out_specs = [pl.BlockSpec((tm, d), _tile_row, pipeline_mode=single),
                     pl.BlockSpec((tm, d), _tile_row, pipeline_mode=single)]
        out_shape = [jax.ShapeDtypeStruct((r, d), F32), jax.ShapeDtypeStruct((r, d), BF16)]
        scratch = []
    return pl.pallas_call(
        functools.partial(_ffn_down_kernel, tn=tn, nn=nn, routed=routed),
        grid_spec=pltpu.PrefetchScalarGridSpec(
            num_scalar_prefetch=2,
            grid=(r // tm, nn),
            in_specs=in_specs,
            out_specs=out_specs,
            scratch_shapes=scratch),
        out_shape=out_shape,
        compiler_params=_params("arbitrary", "arbitrary"),
        name="ffn_down_routed" if routed else "ffn_down_dense",
    )(tile_expert, n_active, *args)


def _router_kernel(h_ref, g_ref, w_ref, o_ref, dense_ref):
    xn = _rms(h_ref[...], g_ref[...])
    logits = jnp.dot(xn, w_ref[...], preferred_element_type=F32, precision=lax.Precision.HIGHEST)
    lane = lax.broadcasted_iota(jnp.int32, logits.shape, 1)
    logits = jnp.where(lane < N_EXPERTS, logits, -jnp.inf)
    e = jnp.exp(logits - jnp.max(logits, axis=-1, keepdims=True))
    probs = e / jnp.sum(e, axis=-1, keepdims=True)
    p1 = jnp.max(probs, axis=-1, keepdims=True)
    e1 = jnp.min(jnp.where(probs == p1, lane, LANES), axis=-1, keepdims=True)
    rest = jnp.where(lane == e1, -1.0, probs)
    p2 = jnp.max(rest, axis=-1, keepdims=True)
    e2 = jnp.min(jnp.where(rest == p2, lane, LANES), axis=-1, keepdims=True)
    tot = p1 + p2
    g1 = p1 / tot
    g2 = p2 / tot
    o_ref[...] = jnp.where(lane == 0, e1.astype(F32),
                           jnp.where(lane == 1, e2.astype(F32), jnp.where(lane == 2, g1, g2)))
    dense_ref[...] = jnp.where(lane == e1, g1, jnp.where(lane == e2, g2, 0.0))


def route_top2(h, gain, w_router, tm):
    t, d = h.shape
    wpad = jnp.pad(w_router, ((0, 0), (0, LANES - N_EXPERTS)))
    return pl.pallas_call(
        _router_kernel,
        grid=(t // tm,),
        in_specs=[pl.BlockSpec((tm, d), lambda i: (i, 0)),
                  pl.BlockSpec((1, d), lambda i: (0, 0)),
                  pl.BlockSpec((d, LANES), lambda i: (0, 0))],
        out_specs=[pl.BlockSpec((tm, LANES), lambda i: (i, 0)),
                   pl.BlockSpec((tm, LANES), lambda i: (i, 0))],
        out_shape=[jax.ShapeDtypeStruct((t, LANES), F32), jax.ShapeDtypeStruct((t, LANES), F32)],
        compiler_params=_params("arbitrary"),
        name="route_top2",
    )(h, gain.reshape(1, d), wpad)


GATHER_TILE = TM_MOE


def _start_row_copies(idx_ref, first, count, src_ref, dst_ref, sem):
    def issue(j, carry):
        t = idx_ref[first + j]
        pltpu.make_async_copy(
            src_ref.at[pl.ds(pl.multiple_of(t * ROW_TILES, ROW_TILES), ROW_TILES), :],
            dst_ref.at[pl.ds(pl.multiple_of(j * ROW_TILES, ROW_TILES), ROW_TILES), :], sem).start()
        return carry

    lax.fori_loop(0, count, issue, 0, unroll=8)


def _wait_row_copies(src_ref, dst_ref, sem):
    pltpu.make_async_copy(src_ref.at[pl.ds(0, dst_ref.shape[0]), :], dst_ref, sem).wait()


def _gather_kernel(idx_ref, n_ref, src_ref, o_ref, buf_ref, sem):
    i = pl.program_id(0)

    @pl.when(i * GATHER_TILE < n_ref[0])
    def _():
        _start_row_copies(idx_ref, i * GATHER_TILE, GATHER_TILE, src_ref, buf_ref, sem)
        _wait_row_copies(src_ref, buf_ref, sem)
        _rows_from_tiles(buf_ref, o_ref)


def gather_rows(src, idx, n):
    r = idx.shape[0]
    last = lambda i, idx_ref, n_ref: (jnp.minimum(i, (n_ref[0] - 1) // GATHER_TILE), 0)
    return pl.pallas_call(
        _gather_kernel,
        grid_spec=pltpu.PrefetchScalarGridSpec(
            num_scalar_prefetch=2,
            grid=(r // GATHER_TILE,),
            in_specs=[pl.BlockSpec(memory_space=pl.ANY)],
            out_specs=pl.BlockSpec((GATHER_TILE, D_MODEL), last),
            scratch_shapes=[pltpu.VMEM((GATHER_TILE * ROW_TILES, LANES), F32), pltpu.SemaphoreType.DMA(())]),
        out_shape=jax.ShapeDtypeStruct((r, D_MODEL), BF16),
        compiler_params=_params("arbitrary"),
        name="gather_rows",
    )(idx, n, src)


def _combine_kernel(pos_ref, h_ref, o_ref, g_ref, h2_ref, hn_ref, o1_ref, o2_ref, sem1, sem2, *, tm, t):
    i = pl.program_id(0)
    _start_row_copies(pos_ref, i * tm, tm, o_ref, o1_ref, sem1)
    _start_row_copies(pos_ref, t + i * tm, tm, o_ref, o2_ref, sem2)
    _wait_row_copies(o_ref, o1_ref, sem1)
    _wait_row_copies(o_ref, o2_ref, sem2)
    for s in range(ROW_TILES):
        sl = slice(s * LANES, (s + 1) * LANES)
        h2_ref[:, sl] = h_ref[:, sl] + (o1_ref[_slab(o1_ref, s)] + o2_ref[_slab(o2_ref, s)])
    hn_ref[...] = _rms(h2_ref[...], g_ref[...]).astype(BF16)


def combine_rows(h, o, pos, gain, tm=TM_MID):
    t, d = h.shape
    tiled = pltpu.VMEM((tm * ROW_TILES, LANES), F32)
    return pl.pallas_call(
        functools.partial(_combine_kernel, tm=tm, t=t),
        grid_spec=pltpu.PrefetchScalarGridSpec(
            num_scalar_prefetch=1,
            grid=(t // tm,),
            in_specs=[pl.BlockSpec((tm, d), lambda i, pos_ref: (i, 0)),
                      pl.BlockSpec(memory_space=pl.ANY),
                      pl.BlockSpec((1, d), lambda i, pos_ref: (0, 0))],
            out_specs=[pl.BlockSpec((tm, d), lambda i, pos_ref: (i, 0)),
                       pl.BlockSpec((tm, d), lambda i, pos_ref: (i, 0))],
            scratch_shapes=[tiled, tiled, pltpu.SemaphoreType.DMA(()), pltpu.SemaphoreType.DMA(())]),
        out_shape=[jax.ShapeDtypeStruct((t, d), F32), jax.ShapeDtypeStruct((t, d), BF16)],
        compiler_params=_params("arbitrary"),
        name="combine_rows",
    )(pos, h, o, gain.reshape(1, d))


def moe_prompt(h, ffn_gain, w_router, wg, wu, wd, layer, ple_gain):
    xn = norm_rows(h, ffn_gain, F32, tiled=True)
    r, _ = route_top2(h, ffn_gain, w_router, TM_BIG)
    experts = jnp.concatenate([r[:, 0], r[:, 1]]).astype(jnp.int32)
    gates = jnp.concatenate([r[:, 2], r[:, 3]])
    onehot = (experts[:, None] == jnp.arange(N_EXPERTS)[None, :]).astype(jnp.int32)
    rank = jnp.sum(jnp.cumsum(onehot, axis=0) * onehot, axis=1) - 1
    counts = jnp.sum(onehot, axis=0)
    tiles = (counts + TM_MOE - 1) // TM_MOE
    tile_end = jnp.cumsum(tiles)
    tile_start = tile_end - tiles
    dest = jnp.sum(onehot * tile_start[None, :], axis=1) * TM_MOE + rank
    token = jnp.tile(jnp.arange(T, dtype=jnp.int32), 2)
    src_token = jnp.zeros((R_PAD,), jnp.int32).at[dest].set(token)
    row_gate = jnp.zeros((R_PAD,), F32).at[dest].set(gates).reshape(R_PAD, 1)
    n_active = tile_end[-1:].astype(jnp.int32)
    tile_ids = jnp.minimum(jnp.arange(MOE_TILES, dtype=jnp.int32), n_active[0] - 1)
    tile_expert = jnp.sum((tile_ids[:, None] >= tile_end[None, :]).astype(jnp.int32), axis=1)
    xs = gather_rows(xn, src_token, n_active * TM_MOE)
    flat = lambda w: w.reshape((-1,) + w.shape[2:])
    tile_w = tile_expert + layer * N_EXPERTS
    a = ffn_up(xs, flat(wg), flat(wu), tile_w, n_active, tm=TM_MOE)
    o = ffn_down(a, flat(wd), tile_w, n_active, tm=TM_MOE, row_gate=row_gate)
    return combine_rows(h, o, dest, ple_gain)


def _headnorm_kernel(q_ref, k_ref, qg_ref, kg_ref, qn_ref, kn_ref):
    def norm_pairs(x_ref, g_ref, o_ref, scale):
        lo = lax.broadcasted_iota(jnp.int32, (x_ref.shape[0], LANES), 1) < HEAD_DIM
        for j in range(x_ref.shape[1] // LANES):
            x = x_ref[:, j * LANES:(j + 1) * LANES]
            s = x * x
            s_lo = jnp.sum(jnp.where(lo, s, 0.0), axis=-1, keepdims=True)
            s_hi = jnp.sum(jnp.where(lo, 0.0, s), axis=-1, keepdims=True)
            ms = jnp.where(lo, s_lo, s_hi) * (1.0 / HEAD_DIM)
            y = x * lax.rsqrt(ms + NORM_EPS) * g_ref[...]
            o_ref[:, j * LANES:(j + 1) * LANES] = (y * scale).astype(o_ref.dtype)

    norm_pairs(q_ref, qg_ref, qn_ref, HEAD_DIM ** -0.5)
    norm_pairs(k_ref, kg_ref, kn_ref, 1.0)


def head_norm(qkv, q_gain, k_gain, q_dtype, tm):
    t = qkv.shape[0]
    return pl.pallas_call(
        _headnorm_kernel,
        grid=(t // tm,),
        in_specs=[pl.BlockSpec((tm, NQ), lambda i: (i, 0)),
                  pl.BlockSpec((tm, NKV), lambda i: (i, NQ // NKV)),
                  pl.BlockSpec((1, LANES), lambda i: (0, 0)),
                  pl.BlockSpec((1, LANES), lambda i: (0, 0))],
        out_specs=[pl.BlockSpec((tm, NQ), lambda i: (i, 0)),
                   pl.BlockSpec((tm, NKV), lambda i: (i, 0))],
        out_shape=[jax.ShapeDtypeStruct((t, NQ), q_dtype), jax.ShapeDtypeStruct((t, NKV), F32)],
        compiler_params=_params("arbitrary"),
        name="head_norm",
    )(qkv, qkv, jnp.tile(q_gain, 2).reshape(1, LANES), jnp.tile(k_gain, 2).reshape(1, LANES))


def _attn_kernel(slope_ref, sink_ref, q_ref, kp_ref, kc_ref, vp_ref, vc_ref, o_ref):
    rows = WINDOW
    nkeys = 2 * WINDOW
    rep = ATTN_HEADS // ATTN_KV_HEADS
    qi = lax.broadcasted_iota(jnp.int32, (rows, nkeys), 0)
    kj = lax.broadcasted_iota(jnp.int32, (rows, nkeys), 1)
    delta = WINDOW + qi - kj
    mask = (delta >= 0) & (delta <= WINDOW) & ((kj >= WINDOW) | (pl.program_id(1) > 0))
    deltaf = delta.astype(F32)
    lo_q = lax.broadcasted_iota(jnp.int32, (rows, LANES), 1) < HEAD_DIM

    for jp in range(ATTN_KV_HEADS // 2):
        ksl = slice(jp * LANES, (jp + 1) * LANES)
        k2 = jnp.concatenate([kp_ref[:, ksl], kc_ref[:, ksl]], axis=0).astype(BF16)
        v2 = jnp.concatenate([vp_ref[:, ksl], vc_ref[:, ksl]], axis=0)
        v2r = pltpu.roll(v2, HEAD_DIM, 1).astype(BF16)
        v2 = v2.astype(BF16)
        for qp in range(rep):
            col = (jp * rep + qp) * LANES
            q2 = q_ref[:, col:col + LANES]
            q2r = pltpu.roll(q2, HEAD_DIM, 1)
            g_half = qp // (rep // 2)
            outs = []
            for hh in range(2):
                h = (jp * rep + qp) * 2 + hh
                qsel = q2 if hh == g_half else q2r
                qm = jnp.where(lo_q if g_half == 0 else ~lo_q, qsel, jnp.zeros_like(qsel))
                s = _dot_nt(qm, k2) - slope_ref[h] * deltaf
                s = jnp.where(mask, s, -jnp.inf)
                sink = sink_ref[h]
                m = jnp.maximum(jnp.max(s, axis=-1, keepdims=True), sink)
                e = jnp.exp(s - m)
                probs = e / (jnp.sum(e, axis=-1, keepdims=True) + jnp.exp(sink - m))
                outs.append(_dot(probs.astype(BF16), v2 if hh == g_half else v2r))
            o_ref[:, col:col + LANES] = jnp.where(lo_q, outs[0], outs[1]).astype(o_ref.dtype)


def attention_prompt(hn, h, w_qkv, layer, q_gain, k_gain, sinks, w_o, next_gain):
    qkv = matmul_cols(hn, w_qkv, layer, NQ + 2 * NKV)
    qn, kn = head_norm(qkv, q_gain, k_gain, BF16, TM_MID)
    nblk = SEQ // WINDOW
    vcol = NQ // NKV + 1
    cur = lambda c: (lambda b, n: (b * nblk + n, c))
    prev = lambda c: (lambda b, n: (jnp.maximum(b * nblk + n - 1, 0), c))
    smem = pl.BlockSpec(memory_space=pltpu.SMEM)
    o = pl.pallas_call(
        _attn_kernel,
        grid=(BATCH, nblk),
        in_specs=[smem, smem,
                  pl.BlockSpec((WINDOW, NQ), cur(0)),
                  pl.BlockSpec((WINDOW, NKV), prev(0)),
                  pl.BlockSpec((WINDOW, NKV), cur(0)),
                  pl.BlockSpec((WINDOW, NKV), prev(vcol)),
                  pl.BlockSpec((WINDOW, NKV), cur(vcol))],
        out_specs=pl.BlockSpec((WINDOW, NQ), cur(0)),
        out_shape=jax.ShapeDtypeStruct((T, NQ), BF16),
        compiler_params=_params("arbitrary", "arbitrary"),
        name="window_attention",
    )(_alibi_slopes(), sinks, qn, kn, kn, qkv, qkv)
    h2, hn2 = proj_residual(o, w_o, layer, h, next_gain)
    kv_shape = (BATCH, WINDOW, ATTN_KV_HEADS, HEAD_DIM)
    new_k = kn.reshape(BATCH, SEQ, NKV)[:, -WINDOW:].reshape(kv_shape)
    new_v = qkv[:, NQ + NKV:].reshape(BATCH, SEQ, NKV)[:, -WINDOW:].reshape(kv_shape)
    return h2, hn2, (new_k, new_v)


SCONV_TILE = 512
SCONV_TAIL = SUBLANES


def _sconv_kernel(gb_ref, gc_ref, u_ref, gcp_ref, up_ref, w_ref, y_ref, tail_ref):
    rows = gb_ref.shape[0]
    cu = gc_ref[...] * u_ref[...]
    prev = jnp.where(pl.program_id(1) > 0, gcp_ref[...] * up_ref[...], 0.0)
    row = lax.broadcasted_iota(jnp.int32, cu.shape, 0)
    s1 = jnp.where(row == 0, prev[SCONV_TAIL - 1:], pltpu.roll(cu, 1, 0))
    s2 = jnp.where(row == 0, prev[SCONV_TAIL - 2:SCONV_TAIL - 1],
                   jnp.where(row == 1, prev[SCONV_TAIL - 1:], pltpu.roll(cu, 2, 0)))
    conv = w_ref[0:1] * s2 + w_ref[1:2] * s1 + w_ref[2:3] * cu
    y_ref[...] = (gb_ref[...] * conv).astype(y_ref.dtype)
    tail_ref[...] = cu[rows - SCONV_TAIL:]


def sconv_prompt(hn, h, w_in, conv_w, layer, w_out, next_gain):
    d = D_MODEL
    proj = matmul_cols(hn, w_in, layer, 3 * d)
    nt = SEQ // SCONV_TILE
    tpb = SCONV_TILE // SCONV_TAIL
    cur = lambda c: (lambda b, i: (b * nt + i, c))
    prev = lambda c: (lambda b, i: (jnp.maximum((b * nt + i) * tpb - 1, 0), c))
    y, tails = pl.pallas_call(
        _sconv_kernel,
        grid=(BATCH, nt),
        in_specs=[pl.BlockSpec((SCONV_TILE, d), cur(0)),
                  pl.BlockSpec((SCONV_TILE, d), cur(1)),
                  pl.BlockSpec((SCONV_TILE, d), cur(2)),
                  pl.BlockSpec((SCONV_TAIL, d), prev(1)),
                  pl.BlockSpec((SCONV_TAIL, d), prev(2)),
                  pl.BlockSpec((None, SCONV_WIDTH, d), lambda b, i: (layer, 0, 0))],
        out_specs=[pl.BlockSpec((SCONV_TILE, d), cur(0)),
                   pl.BlockSpec((None, SCONV_TAIL, d), lambda b, i: (b * nt + i, 0, 0))],
        out_shape=[jax.ShapeDtypeStruct((T, d), BF16),
                   jax.ShapeDtypeStruct((BATCH * nt, SCONV_TAIL, d), F32)],
        compiler_params=_params("arbitrary", "arbitrary"),
        name="sconv_prompt",
    )(proj, proj, proj, proj, proj, conv_w)
    h2, hn2 = proj_residual(y, w_out, layer, h, next_gain)
    new_state = tails.reshape(BATCH, nt, SCONV_TAIL, d)[:, -1, -(SCONV_WIDTH - 1):]
    return h2, hn2, new_state


XBC_CHUNK = 512
GROUP_W = D_INNER // SSM_GROUPS
HEADS_PER_GROUP = SSM_HEADS // SSM_GROUPS


def _ssd_kernel(z_ref, xc_ref, xp_ref, dt_ref, cw_ref, cb_ref, dtb_ref, alog_ref, dsk_ref, nw_ref,
                y_ref, hfin_ref, state_ref, xact_ref, acst_ref, yg_ref):
    c = pl.program_id(1)
    q = SSM_CHUNK

    @pl.when(c == 0)
    def _():
        state_ref[...] = jnp.zeros_like(state_ref)

    has_prev = c > 0
    row = lax.broadcasted_iota(jnp.int32, (q, XBC_CHUNK), 0)
    for j in range(SSM_CONV_DIM // XBC_CHUNK):
        sl = slice(j * XBC_CHUNK, (j + 1) * XBC_CHUNK)
        cur = xc_ref[:, sl]
        prv = jnp.where(has_prev, xp_ref[:, sl], 0.0)
        acc = None
        for k in range(SSM_CONV_WIDTH):
            sh = SSM_CONV_WIDTH - 1 - k
            if sh == 0:
                term = cur
            else:
                term = jnp.where(row < sh, pltpu.roll(prv, sh, 0), pltpu.roll(cur, sh, 0))
            term = cw_ref[k:k + 1, sl] * term
            acc = term if acc is None else acc + term
        xact_ref[:, sl] = jax.nn.silu(acc + cb_ref[:, sl])

    dt = _softplus(dt_ref[...] + dtb_ref[...])
    da = dt * (-jnp.exp(alog_ref[...]))
    tri = (lax.broadcasted_iota(jnp.int32, (q, q), 0) >= lax.broadcasted_iota(jnp.int32, (q, q), 1))
    a_cs = jnp.dot(tri.astype(F32), da, preferred_element_type=F32, precision=lax.Precision.HIGHEST)
    acst_ref[...] = a_cs.T
    lane = lax.broadcasted_iota(jnp.int32, (q, LANES), 1)

    def group(g, carry):
        goff = pl.multiple_of(g * GROUP_W, GROUP_W)
        xs_g = xact_ref[:, pl.ds(goff, GROUP_W)]
        bm = xact_ref[:, pl.ds(pl.multiple_of(D_INNER + g * D_STATE, D_STATE), D_STATE)]
        cm = xact_ref[:, pl.ds(pl.multiple_of(D_INNER + SSM_GROUPS * D_STATE + g * D_STATE, D_STATE), D_STATE)]
        cm_b = cm.astype(BF16)
        cb = _dot_nt(cm_b, bm.astype(BF16))
        for hh in range(HEADS_PER_GROUP):
            h = g * HEADS_PER_GROUP + hh
            sel = lane == h
            acs_col = jnp.sum(jnp.where(sel, a_cs, 0.0), axis=1, keepdims=True)
            dt_col = jnp.sum(jnp.where(sel, dt, 0.0), axis=1, keepdims=True)
            acs_row = acst_ref[pl.ds(h, 1), :]
            acs_last = acs_col[q - 1:q]
            xs = xs_g[:, hh * SSM_HEAD_DIM:(hh + 1) * SSM_HEAD_DIM]
            xdt = (xs * dt_col).astype(BF16)
            seg = acs_col - acs_row
            decay_in = jnp.where(tri, jnp.exp(jnp.where(tri, seg, 0.0)), 0.0)
            y = _dot((cb * decay_in).astype(BF16), xdt)
            h_in = state_ref[h]
            y = y + _dot_nt(cm_b, h_in.astype(BF16)) * jnp.exp(acs_col)
            decay_out = jnp.exp(acs_last - acs_col)
            st = _dot_tn(xdt, (bm * decay_out).astype(BF16))
            state_ref[h] = jnp.exp(acs_last) * h_in + st
            yg_ref[:, hh * SSM_HEAD_DIM:(hh + 1) * SSM_HEAD_DIM] = y
        y_g = yg_ref[...] + dsk_ref[:, pl.ds(goff, GROUP_W)] * xs_g
        gg = y_g * jax.nn.silu(z_ref[:, pl.ds(goff, GROUP_W)])
        gg = gg * lax.rsqrt(jnp.mean(gg * gg, axis=-1, keepdims=True) + NORM_EPS)
        y_ref[:, pl.ds(goff, GROUP_W)] = (gg * nw_ref[:, pl.ds(goff, GROUP_W)]).astype(y_ref.dtype)
        return carry

    lax.fori_loop(0, SSM_GROUPS, group, 0)

    @pl.when(c == pl.num_programs(1) - 1)
    def _():
        hfin_ref[...] = state_ref[...]


def _pad_heads(v):
    return jnp.pad(v, (0, LANES - SSM_HEADS)).reshape(1, LANES)


def ssd_prompt(hn, h, w_in, layer, conv_w, conv_b, dt_bias, a_log, d_skip, norm_w, w_out, next_gain):
    cd = SSM_CONV_DIM
    z = matmul_cols(hn, w_in, layer, D_INNER, col_off=0)
    xbc = matmul_cols(hn, w_in, layer, cd, col_off=D_INNER)
    w_dt = jnp.pad(w_in[layer, :, D_INNER + cd:], ((0, 0), (0, LANES - SSM_HEADS)))[None]
    dt_raw = matmul_cols(hn, w_dt, 0, LANES)
    dsk_full = jnp.repeat(d_skip, SSM_HEAD_DIM).reshape(1, D_INNER)
    nc = SEQ // SSM_CHUNK
    cur = lambda b, c: (b * nc + c, 0)
    whole = lambda b, c: (0, 0)
    y, h_fin = pl.pallas_call(
        _ssd_kernel,
        grid=(BATCH, nc),
        in_specs=[pl.BlockSpec((SSM_CHUNK, D_INNER), cur),
                  pl.BlockSpec((SSM_CHUNK, cd), cur),
                  pl.BlockSpec((SSM_CHUNK, cd), lambda b, c: (jnp.maximum(b * nc + c - 1, 0), 0)),
                  pl.BlockSpec((SSM_CHUNK, LANES), cur),
                  pl.BlockSpec((SSM_CONV_WIDTH, cd), whole),
                  pl.BlockSpec((1, cd), whole),
                  pl.BlockSpec((1, LANES), whole),
                  pl.BlockSpec((1, LANES), whole),
                  pl.BlockSpec((1, D_INNER), whole),
                  pl.BlockSpec((1, D_INNER), whole)],
        out_specs=[pl.BlockSpec((SSM_CHUNK, D_INNER), cur),
                   pl.BlockSpec((None, SSM_HEADS, SSM_HEAD_DIM, D_STATE), lambda b, c: (b, 0, 0, 0))],
        out_shape=[jax.ShapeDtypeStruct((T, D_INNER), BF16),
                   jax.ShapeDtypeStruct((BATCH, SSM_HEADS, SSM_HEAD_DIM, D_STATE), F32)],
        scratch_shapes=[pltpu.VMEM((SSM_HEADS, SSM_HEAD_DIM, D_STATE), F32),
                        pltpu.VMEM((SSM_CHUNK, cd), F32),
                        pltpu.VMEM((LANES, SSM_CHUNK), F32),
                        pltpu.VMEM((SSM_CHUNK, GROUP_W), F32)],
        compiler_params=_params("arbitrary", "arbitrary"),
        name="ssd_prompt",
    )(z, xbc, xbc, dt_raw, conv_w, conv_b.reshape(1, cd), _pad_heads(dt_bias), _pad_heads(a_log), dsk_full,
      norm_w.reshape(1, D_INNER))
    h2, hn2 = proj_residual(y, w_out, layer, h, next_gain, tm=TM_MID)
    new_conv = xbc.reshape(BATCH, SEQ, cd)[:, -(SSM_CONV_WIDTH - 1):]
    return h2, hn2, (h_fin, new_conv)


def _split(x):
    hi = x.astype(BF16)
    lo = (x - hi.astype(F32)).astype(BF16)
    return hi, lo


def _hp_kernel(*refs, nk, x_mode, epi):
    refs = list(refs)
    if x_mode == "norm":
        x_ref, g_ref = refs[:2]
        refs = refs[2:]
        x = _rms(x_ref[...], g_ref[...])
    elif x_mode == "swiglu":
        g_ref, u_ref = refs[:2]
        refs = refs[2:]
        x = jax.nn.silu(g_ref[...]) * u_ref[...]
    else:
        x = refs[0][...]
        refs = refs[1:]
    w_ref = refs[0]
    refs = refs[1:]
    if epi == "gated":
        res_ref, pp_ref, o_ref = refs
    elif epi == "res":
        res_ref, o_ref = refs
    else:
        (o_ref,) = refs
    k = pl.program_id(2)
    rows = x.shape[0]
    x_hi, x_lo = _split(x)
    w_hi, w_lo = _split(w_ref[...])
    both = _dot(jnp.concatenate([x_hi, x_lo], axis=0), w_hi)
    part = both[:rows] + both[rows:] + _dot(x_hi, w_lo)

    @pl.when(k == 0)
    def _():
        o_ref[...] = part

    @pl.when(k > 0)
    def _():
        o_ref[...] += part

    if epi is not None:
        @pl.when(k == nk - 1)
        def _():
            acc = o_ref[...]
            if epi == "gated":
                acc = jax.nn.sigmoid(acc) * pp_ref[...]
            o_ref[...] = res_ref[...] + acc


def hp_linear(w, widx, n_out, *, x=None, norm=None, swiglu=None, col_off=0, n_e=1, tk, tn, res=None, pp=None):
    kdim = w.shape[1]
    nk = kdim // tk
    off = col_off // tn
    assert kdim % tk == 0 and n_out % tn == 0 and col_off % tn == 0
    if norm is not None:
        assert nk == 1
        x_mode = "norm"
        args = [norm[0], norm[1].reshape(1, kdim)]
        in_specs = [pl.BlockSpec((NS, tk), lambda e, n, k: (0, k)),
                    pl.BlockSpec((1, tk), lambda e, n, k: (0, k))]
    elif swiglu is not None:
        x_mode = "swiglu"
        args = list(swiglu)
        in_specs = [pl.BlockSpec((None, NS, tk), lambda e, n, k: (e, 0, k))] * 2
    else:
        x_mode = "plain"
        args = [x]
        in_specs = [pl.BlockSpec((NS, tk), lambda e, n, k: (0, k))]
    args.append(w)
    in_specs.append(pl.BlockSpec((None, tk, tn), lambda e, n, k: (widx + e, k, n + off)))
    epi = None
    if res is not None:
        epi = "res"
        args.append(res)
        in_specs.append(pl.BlockSpec((NS, tn), lambda e, n, k: (0, n)))
        if pp is not None:
            epi = "gated"
            args.append(pp)
            in_specs.append(pl.BlockSpec((NS, tn), lambda e, n, k: (0, n)))
    return pl.pallas_call(
        functools.partial(_hp_kernel, nk=nk, x_mode=x_mode, epi=epi),
        grid=(n_e, n_out // tn, nk),
        in_specs=in_specs,
        out_specs=pl.BlockSpec((None, NS, tn), lambda e, n, k: (e, 0, n)),
        out_shape=jax.ShapeDtypeStruct((n_e, NS, n_out), F32),
        compiler_params=_params("arbitrary", "arbitrary", "arbitrary"),
        name="hp_linear_" + x_mode,
    )(*args)


def _attn_step_kernel(slope_ref, sink_ref, q_ref, kc_ref, vc_ref, kn_ref, vn_ref, o_ref):
    rep = ATTN_HEADS // ATTN_KV_HEADS
    lo = lax.broadcasted_iota(jnp.int32, (1, LANES), 1) < HEAD_DIM
    delta = (WINDOW - lax.broadcasted_iota(jnp.int32, (WINDOW, 1), 0)).astype(F32)
    for jp in range(ATTN_KV_HEADS // 2):
        ksl = slice(jp * LANES, (jp + 1) * LANES)
        k2 = kc_ref[:, ksl]
        v2 = vc_ref[:, ksl]
        kn2 = kn_ref[:, ksl]
        vn2 = vn_ref[:, ksl]
        for qp in range(rep):
            col = (jp * rep + qp) * LANES
            q2 = q_ref[:, col:col + LANES]
            q2r = pltpu.roll(q2, HEAD_DIM, 1)
            g_half = qp // (rep // 2)
            kv_lanes = lo if g_half == 0 else ~lo
            out = jnp.zeros((1, LANES), F32)
            for hh in range(2):
                h = (jp * rep + qp) * 2 + hh
                qm = jnp.where(kv_lanes, q2 if hh == g_half else q2r, 0.0)
                s = jnp.sum(k2 * qm, axis=1, keepdims=True) - slope_ref[h] * delta
                s_new = jnp.sum(kn2 * qm, axis=1, keepdims=True)
                sink = sink_ref[h]
                m = jnp.maximum(jnp.maximum(jnp.max(s, axis=0, keepdims=True), s_new), sink)
                e = jnp.exp(s - m)
                e_new = jnp.exp(s_new - m)
                denom = jnp.sum(e, axis=0, keepdims=True) + e_new + jnp.exp(sink - m)
                o = jnp.sum((e / denom) * v2, axis=0, keepdims=True) + (e_new / denom) * vn2
                if hh != g_half:
                    o = pltpu.roll(o, HEAD_DIM, 1)
                out = jnp.where(lo if hh == 0 else ~lo, o, out)
            o_ref[:, col:col + LANES] = out


def _sconv_step_kernel(proj_ref, p_ref, w_ref, y_ref, cu_ref):
    d = D_MODEL
    cu = proj_ref[:, d:2 * d] * proj_ref[:, 2 * d:]
    conv = w_ref[0:1] * p_ref[0] + w_ref[1:2] * p_ref[1] + w_ref[2:3] * cu
    y_ref[...] = proj_ref[:, :d] * conv
    cu_ref[...] = cu


def _ssd_step_pre_kernel(xc_ref, p_ref, dt_ref, cw_ref, cb_ref, dtb_ref, alog_ref, xact_ref, dt_out_ref, dec_ref):
    acc = cw_ref[0:1] * p_ref[0] + cw_ref[1:2] * p_ref[1] + cw_ref[2:3] * p_ref[2] + cw_ref[3:4] * xc_ref[...]
    xact_ref[...] = jax.nn.silu(acc + cb_ref[...])
    dt = _softplus(dt_ref[...] + dtb_ref[...])
    dt_out_ref[...] = dt
    dec_ref[...] = jnp.exp(dt * (-jnp.exp(alog_ref[...])))


def _ssd_step_kernel(dt_ref, dec_ref, dsk_ref, xt_ref, bm_ref, cm_ref, h0_ref, yt_ref, h1_ref):
    b = pl.program_id(0)
    xt = xt_ref[...]
    lane = lax.broadcasted_iota(jnp.int32, xt.shape, 1)

    yt = jnp.zeros(xt.shape, F32)
    for h in range(SSM_HEADS):
        g = h // HEADS_PER_GROUP
        x_col = xt[:, h:h + 1]
        bm = bm_ref[g:g + 1, :]
        cm = cm_ref[g:g + 1, :]
        h1 = dec_ref[b, h] * h0_ref[h] + (dt_ref[b, h] * x_col) * bm
        h1_ref[h] = h1
        y_col = jnp.sum(h1 * cm, axis=1, keepdims=True) + dsk_ref[h] * x_col
        yt = jnp.where(lane == h, y_col, yt)
    yt_ref[...] = yt


def _ssd_gate_kernel(y_ref, z_ref, nw_ref, o_ref):
    for g in range(SSM_GROUPS):
        sl = slice(g * GROUP_W, (g + 1) * GROUP_W)
        gg = y_ref[:, sl] * jax.nn.silu(z_ref[:, sl])
        gg = gg * lax.rsqrt(jnp.mean(gg * gg, axis=-1, keepdims=True) + NORM_EPS)
        o_ref[:, sl] = gg * nw_ref[:, sl]


def _moe_mix_kernel(h_ref, gates_ref, d_ref, o_ref):
    lane = lax.broadcasted_iota(jnp.int32, gates_ref.shape, 1)
    y = jnp.zeros(h_ref.shape, F32)
    for e in range(N_EXPERTS):
        gate = jnp.sum(jnp.where(lane == e, gates_ref[...], 0.0), axis=1, keepdims=True)
        y = y + gate * d_ref[e]
    o_ref[...] = h_ref[...] + y


def _whole(shape):
    return pl.BlockSpec(shape, lambda *_: (0,) * len(shape))


def _small_call(body, out_shapes, *args, name):
    multi = isinstance(out_shapes, (list, tuple))
    outs = list(out_shapes) if multi else [out_shapes]
    res = pl.pallas_call(
        body,
        grid=(1,),
        in_specs=[_whole(a.shape) for a in args],
        out_specs=[_whole(o.shape) for o in outs],
        out_shape=outs,
        compiler_params=_params("arbitrary"),
        name=name,
    )(*args)
    return res if multi else res[0]


def _sds(*shape):
    return jax.ShapeDtypeStruct(shape, F32)


def attention_sample(h, norm_gain, cache_k, cache_v, w_qkv, layer, q_gain, k_gain, sinks, w_o):
    qkv = hp_linear(w_qkv, layer, NQ + 2 * NKV, norm=(h, norm_gain), tk=D_MODEL, tn=1024)[0]
    qn, kn = head_norm(qkv, q_gain, k_gain, F32, NS)
    v_new = qkv[:, NQ + NKV:]
    smem = pl.BlockSpec(memory_space=pltpu.SMEM)
    rowsq = pl.BlockSpec((None, 1, NQ), lambda b: (b, 0, 0))
    rowsk = pl.BlockSpec((None, 1, NKV), lambda b: (b, 0, 0))
    cache = pl.BlockSpec((None, WINDOW, NKV), lambda b: (b, 0, 0))
    o = pl.pallas_call(
        _attn_step_kernel,
        grid=(NS,),
        in_specs=[smem, smem, rowsq, cache, cache, rowsk, rowsk],
        out_specs=rowsq,
        out_shape=_sds(NS, 1, NQ),
        compiler_params=_params("arbitrary"),
        name="attention_sample",
    )(_alibi_slopes(), sinks, qn.reshape(NS, 1, NQ), cache_k.reshape(NS, WINDOW, NKV),
      cache_v.reshape(NS, WINDOW, NKV), kn.reshape(NS, 1, NKV), v_new.reshape(NS, 1, NKV))
    h2 = hp_linear(w_o, layer, D_MODEL, x=o.reshape(NS, NQ), tk=1024, tn=1024, res=h)[0]
    kv_row = (NS, 1, ATTN_KV_HEADS, HEAD_DIM)
    new_k = jnp.concatenate([cache_k[:, 1:], kn.reshape(kv_row)], axis=1)
    new_v = jnp.concatenate([cache_v[:, 1:], v_new.reshape(kv_row)], axis=1)
    return h2, (new_k, new_v)


def sconv_sample(h, norm_gain, past, w_in, conv_w, layer, w_out):
    d = D_MODEL
    proj = hp_linear(w_in, layer, 3 * d, norm=(h, norm_gain), tk=d, tn=512)[0]
    y, cu = _small_call(_sconv_step_kernel, [_sds(NS, d), _sds(NS, d)],
                        proj, past.transpose(1, 0, 2), conv_w[layer], name="sconv_sample")
    h2 = hp_linear(w_out, layer, d, x=y, tk=1024, tn=1024, res=h)[0]
    return h2, jnp.concatenate([past[:, 1:], cu[:, None]], axis=1)


def ssd_sample(h, norm_gain, conv_past, h0, w_in, layer, conv_w, conv_b, dt_bias, a_log, d_skip, norm_w, w_out):
    cd = SSM_CONV_DIM
    z = hp_linear(w_in, layer, D_INNER, norm=(h, norm_gain), tk=D_MODEL, tn=1024)[0]
    xbc = hp_linear(w_in, layer, cd, norm=(h, norm_gain), col_off=D_INNER, tk=D_MODEL, tn=1024)[0]
    w_dt = jnp.pad(w_in[layer, :, D_INNER + cd:], ((0, 0), (0, LANES - SSM_HEADS)))[None]
    dt_raw = hp_linear(w_dt, 0, LANES, norm=(h, norm_gain), tk=D_MODEL, tn=LANES)[0]
    xact, dt, dec = _small_call(
        _ssd_step_pre_kernel, [_sds(NS, cd), _sds(NS, LANES), _sds(NS, LANES)],
        xbc, conv_past.transpose(1, 0, 2), dt_raw, conv_w, conv_b.reshape(1, cd), _pad_heads(dt_bias),
        _pad_heads(a_log), name="ssd_sample_pre")
    xt = xact[:, :D_INNER].reshape(NS, SSM_HEADS, SSM_HEAD_DIM).transpose(0, 2, 1)
    bm = xact[:, D_INNER:D_INNER + SSM_GROUPS * D_STATE].reshape(NS, SSM_GROUPS, D_STATE)
    cm = xact[:, D_INNER + SSM_GROUPS * D_STATE:].reshape(NS, SSM_GROUPS, D_STATE)
    smem = pl.BlockSpec(memory_space=pltpu.SMEM)
    per_b = lambda *shape: pl.BlockSpec((None,) + shape, lambda b: (b,) + (0,) * len(shape))
    yt, h1 = pl.pallas_call(
        _ssd_step_kernel,
        grid=(NS,),
        in_specs=[smem, smem, smem,
                  per_b(SSM_HEAD_DIM, SSM_HEADS), per_b(SSM_GROUPS, D_STATE), per_b(SSM_GROUPS, D_STATE),
                  per_b(SSM_HEADS, SSM_HEAD_DIM, D_STATE)],
        out_specs=[per_b(SSM_HEAD_DIM, SSM_HEADS), per_b(SSM_HEADS, SSM_HEAD_DIM, D_STATE)],
        out_shape=[_sds(NS, SSM_HEAD_DIM, SSM_HEADS), _sds(NS, SSM_HEADS, SSM_HEAD_DIM, D_STATE)],
        compiler_params=_params("arbitrary"),
        name="ssd_sample_step",
    )(dt[:, :SSM_HEADS], dec[:, :SSM_HEADS], d_skip, xt, bm, cm, h0)
    y = yt.transpose(0, 2, 1).reshape(NS, D_INNER)
    yn = _small_call(_ssd_gate_kernel, _sds(NS, D_INNER), y, z, norm_w.reshape(1, D_INNER), name="ssd_sample_gate")
    h2 = hp_linear(w_out, layer, D_MODEL, x=yn, tk=1024, tn=1024, res=h)[0]
    return h2, (h1, jnp.concatenate([conv_past[:, 1:], xbc[:, None]], axis=1))


def ffn_sample(h, norm_gain, wg, wu, wd, widx, n_e):
    g = hp_linear(wg, widx, D_FF, norm=(h, norm_gain), n_e=n_e, tk=D_MODEL, tn=1024)
    u = hp_linear(wu, widx, D_FF, norm=(h, norm_gain), n_e=n_e, tk=D_MODEL, tn=1024)
    return g, u


def ple_sample(h, p, ple_gain, w_gate, w_proj, layer):
    pp = hp_linear(w_proj, layer, D_MODEL, x=p, tk=PLE_DIM, tn=D_MODEL)[0]
    return hp_linear(w_gate, layer, D_MODEL, norm=(h, ple_gain), tk=D_MODEL, tn=512, res=h, pp=pp)[0]


def kernel(x_prompt, x_sample, cache_k_win, cache_v_win, state_short_conv, state_ssm, state_ssm_conv, p_prompt, p_sample, norm_mixer, norm_ffn, attn_w_qkv, attn_q_norm, attn_k_norm, attn_sinks, attn_w_out, sconv_w_in, sconv_conv_w, sconv_w_out, ssm_w_in, ssm_conv_w, ssm_conv_b, ssm_dt_bias, ssm_a_log, ssm_d_skip, ssm_norm, ssm_w_out, ffn_w_gate, ffn_w_up, ffn_w_down, moe_w_router, moe_w_gate, moe_w_up, moe_w_down, ple_norm, ple_w_gate, ple_w_proj):
    flat_e = lambda w: w.reshape((-1,) + w.shape[2:])
    moe_wg, moe_wu, moe_wd = flat_e(moe_w_gate), flat_e(moe_w_up), flat_e(moe_w_down)

    def dense_tiles(tm, widx):
        return jnp.full((T // tm,), widx, jnp.int32), jnp.full((1,), T // tm, jnp.int32)

    h = x_prompt.reshape(T, D_MODEL)
    hn = norm_rows(h, norm_mixer[0], BF16)
    k_p, v_p, sc_p, ssm_p, ssmc_p = [], [], [], [], []
    for i in range(DEPTH):
        kind, j = i % 3, i // 3
        if kind == 0:
            h, hn, new = attention_prompt(hn, h, attn_w_qkv, j, attn_q_norm[j], attn_k_norm[j], attn_sinks[j],
                                          attn_w_out, norm_ffn[i])
            k_p.append(new[0]); v_p.append(new[1])
        elif kind == 1:
            h, hn, new = sconv_prompt(hn, h, sconv_w_in, sconv_conv_w, j, sconv_w_out, norm_ffn[i])
            sc_p.append(new)
        else:
            h, hn, new = ssd_prompt(hn, h, ssm_w_in, j, ssm_conv_w[j], ssm_conv_b[j], ssm_dt_bias[j],
                                    ssm_a_log[j], ssm_d_skip[j], ssm_norm[j], ssm_w_out, norm_ffn[i])
            ssm_p.append(new[0]); ssmc_p.append(new[1])
        c = i // 2
        if i % 2 == 0:
            a = ffn_up(hn, ffn_w_gate, ffn_w_up, *dense_tiles(TM_BIG, c), tm=TM_BIG)
            h, hn = ffn_down(a, ffn_w_down, *dense_tiles(TM_BIG, c), tm=TM_BIG, res=h, gain=ple_norm[i])
        else:
            h, hn = moe_prompt(h, norm_ffn[i], moe_w_router[c], moe_w_gate, moe_w_up, moe_w_down, c, ple_norm[i])
        next_gain = norm_mixer[i + 1] if i + 1 < DEPTH else None
        h, hn = proj_residual(hn, ple_w_gate, i, h, next_gain, p=p_prompt[i].reshape(T, PLE_DIM), wp=ple_w_proj)
    y_p = h.reshape(BATCH, SEQ, D_MODEL)

    s = x_sample.reshape(NS, D_MODEL)
    k_s, v_s, sc_s, ssm_s, ssmc_s = [], [], [], [], []
    for i in range(DEPTH):
        kind, j = i % 3, i // 3
        if kind == 0:
            s, new = attention_sample(s, norm_mixer[i], cache_k_win[j], cache_v_win[j], attn_w_qkv, j,
                                      attn_q_norm[j], attn_k_norm[j], attn_sinks[j], attn_w_out)
            k_s.append(new[0]); v_s.append(new[1])
        elif kind == 1:
            s, new = sconv_sample(s, norm_mixer[i], state_short_conv[j], sconv_w_in, sconv_conv_w, j, sconv_w_out)
            sc_s.append(new)
        else:
            s, new = ssd_sample(s, norm_mixer[i], state_ssm_conv[j], state_ssm[j], ssm_w_in, j, ssm_conv_w[j],
                                ssm_conv_b[j], ssm_dt_bias[j], ssm_a_log[j], ssm_d_skip[j], ssm_norm[j], ssm_w_out)
            ssm_s.append(new[0]); ssmc_s.append(new[1])
        c = i // 2
        if i % 2 == 0:
            g, u = ffn_sample(s, norm_ffn[i], ffn_w_gate, ffn_w_up, ffn_w_down, c, 1)
            s = hp_linear(ffn_w_down, c, D_MODEL, swiglu=(g, u), tk=1024, tn=1024, res=s)[0]
        else:
            _, gates = route_top2(s, norm_ffn[i], moe_w_router[c], NS)
            g, u = ffn_sample(s, norm_ffn[i], moe_wg, moe_wu, moe_wd, c * N_EXPERTS, N_EXPERTS)
            d_e = hp_linear(moe_wd, c * N_EXPERTS, D_MODEL, swiglu=(g, u), n_e=N_EXPERTS, tk=1024, tn=1024)
            s = _small_call(_moe_mix_kernel, _sds(NS, D_MODEL), s, gates, d_e, name="moe_mix_sample")
        s = ple_sample(s, p_sample[i].reshape(NS, PLE_DIM), ple_norm[i], ple_w_gate, ple_w_proj, i)
    y_s = s.reshape(NS, 1, D_MODEL)

    return (y_p, y_s, jnp.stack(k_p), jnp.stack(v_p), jnp.stack(k_s), jnp.stack(v_s),
            jnp.stack(sc_p), jnp.stack(sc_s), jnp.stack(ssm_p), jnp.stack(ssm_s),
            jnp.stack(ssmc_p), jnp.stack(ssmc_s))
```

```python
import functools

import jax
import jax.numpy as jnp
from jax import lax
from jax.experimental import pallas as pl
from jax.experimental.pallas import tpu as pltpu

F32 = jnp.float32
BF16 = jnp.bfloat16

D_MODEL = 2048
BATCH = 2
SEQ = 4096
DEPTH = 4
DEC_BATCH = 32
PAST_LEN = 16384
ATTN_HEADS = 32
ATTN_KV_HEADS = 8
HEAD_DIM = 64
WINDOW = 128
SCONV_WIDTH = 3
D_INNER = 2 * D_MODEL
SSM_HEAD_DIM = 64
SSM_HEADS = D_INNER // SSM_HEAD_DIM
SSM_GROUPS = 8
D_STATE = 128
SSM_CONV_WIDTH = 4
SSM_CHUNK = 128
SSM_CONV_DIM = D_INNER + 2 * SSM_GROUPS * D_STATE
D_FF = 7 * D_MODEL // 2
N_EXPERTS = 8
PLE_DIM = 256
NORM_EPS = 1e-6
NQ = ATTN_HEADS * HEAD_DIM
NKV = ATTN_KV_HEADS * HEAD_DIM
assert PAST_LEN >= WINDOW

LANES = 128
SUBLANES = 8
T = BATCH * SEQ
NS = DEC_BATCH
TM_BIG = 1024
TM_MID = 512
TM_MOE = 512
MOE_TILES = (2 * T + N_EXPERTS * (TM_MOE - 1)) // TM_MOE
R_PAD = MOE_TILES * TM_MOE
ROW_TILES = D_MODEL // LANES
VMEM_LIMIT = 56 * 1024 * 1024


def _params(*sem):
    return pltpu.CompilerParams(dimension_semantics=sem, vmem_limit_bytes=VMEM_LIMIT)


def _rms(x, g):
    return x * lax.rsqrt(jnp.mean(x * x, axis=-1, keepdims=True) + NORM_EPS) * g


def _dot(a, b):
    return jnp.dot(a, b, preferred_element_type=F32)


def _dot_nt(a, b):
    return lax.dot_general(a, b, (((1,), (1,)), ((), ())), preferred_element_type=F32)


def _dot_tn(a, b):
    return lax.dot_general(a, b, (((0,), (0,)), ((), ())), preferred_element_type=F32)


def _softplus(x):
    return jnp.maximum(x, 0.0) + jnp.log1p(jnp.exp(-jnp.abs(x)))


def _alibi_slopes():
    return jnp.exp2(-8.0 * jnp.arange(1, ATTN_HEADS + 1, dtype=F32) / ATTN_HEADS)


def _norm_kernel(x_ref, g_ref, o_ref, *, slabs):
    y = _rms(x_ref[...], g_ref[...]).astype(o_ref.dtype)
    if slabs:
        for s in range(ROW_TILES):
            o_ref[s] = y[:, s * LANES:(s + 1) * LANES]
    else:
        o_ref[...] = y


def norm_rows(x, gain, dtype, slabs=False, tm=TM_BIG):
    t, d = x.shape
    if slabs:
        out_spec = pl.BlockSpec((ROW_TILES, tm, LANES), lambda i: (0, i, 0))
        out_full = (ROW_TILES, t, LANES)
    else:
        out_spec = pl.BlockSpec((tm, d), lambda i: (i, 0))
        out_full = (t, d)
    return pl.pallas_call(
        functools.partial(_norm_kernel, slabs=slabs),
        grid=(t // tm,),
        in_specs=[pl.BlockSpec((tm, d), lambda i: (i, 0)),
                  pl.BlockSpec((1, d), lambda i: (0, 0))],
        out_specs=out_spec,
        out_shape=jax.ShapeDtypeStruct(out_full, dtype),
        compiler_params=_params("arbitrary"),
        name="norm_rows",
    )(x, gain.reshape(1, d))


def _mm_kernel(x_ref, w_ref, o_ref, wb_ref):
    @pl.when(pl.program_id(1) == 0)
    def _():
        wb_ref[...] = w_ref[...].astype(BF16)

    o_ref[...] = _dot(x_ref[...], wb_ref[...])


def matmul_cols(x, w, layer, n_out, col_off=0, tn=1024, tm=TM_BIG):
    t, k = x.shape
    tn = min(tn, n_out)
    off = col_off // tn
    assert col_off % tn == 0 and n_out % tn == 0
    return pl.pallas_call(
        _mm_kernel,
        grid=(n_out // tn, t // tm),
        in_specs=[pl.BlockSpec((tm, k), lambda n, m: (m, 0)),
                  pl.BlockSpec((None, k, tn), lambda n, m: (layer, 0, n + off))],
        out_specs=pl.BlockSpec((tm, tn), lambda n, m: (m, n)),
        out_shape=jax.ShapeDtypeStruct((t, n_out), F32),
        scratch_shapes=[pltpu.VMEM((k, tn), BF16)],
        compiler_params=_params("arbitrary", "arbitrary"),
        name="matmul_cols",
    )(x, w)


EPILOGUE_ROWS = 256


def _proj_kernel(*refs, tn, nn, gated, with_norm):
    x_ref, w_ref, res_ref = refs[:3]
    refs = refs[3:]
    if gated:
        p_ref, wp_ref = refs[:2]
        refs = refs[2:]
    if with_norm:
        g_ref, h_ref, hn_ref = refs
    else:
        (h_ref,) = refs
    n = pl.program_id(1)
    h_ref[:, pl.ds(pl.multiple_of(n * tn, tn), tn)] = _dot(x_ref[...], w_ref[...].astype(BF16))

    @pl.when(n == nn - 1)
    def _():
        for r in range(h_ref.shape[0] // EPILOGUE_ROWS):
            rows = slice(r * EPILOGUE_ROWS, (r + 1) * EPILOGUE_ROWS)
            upd = h_ref[rows, :]
            if gated:
                upd = jax.nn.sigmoid(upd) * _dot(p_ref[rows, :].astype(BF16), wp_ref[...].astype(BF16))
            h_new = res_ref[rows, :] + upd
            h_ref[rows, :] = h_new
            if with_norm:
                hn_ref[rows, :] = _rms(h_new, g_ref[...]).astype(BF16)


def proj_residual(x, w, layer, res, gain=None, p=None, wp=None, tm=TM_BIG, tn=512):
    t, kdim = x.shape
    d = w.shape[2]
    nn = d // tn
    gated = p is not None
    with_norm = gain is not None
    single = pl.Buffered(1)
    in_specs = [pl.BlockSpec((tm, kdim), lambda m, k: (m, 0)),
                pl.BlockSpec((None, kdim, tn), lambda m, k: (layer, 0, k)),
                pl.BlockSpec((tm, d), lambda m, k: (m, 0), pipeline_mode=single)]
    args = [x, w, res]
    if gated:
        in_specs += [pl.BlockSpec((tm, p.shape[1]), lambda m, k: (m, 0)),
                     pl.BlockSpec((None,) + wp.shape[1:], lambda m, k: (layer, 0, 0))]
        args += [p, wp]
    out_specs = [pl.BlockSpec((tm, d), lambda m, k: (m, 0), pipeline_mode=single)]
    out_shape = [jax.ShapeDtypeStruct((t, d), F32)]
    if with_norm:
        in_specs.append(pl.BlockSpec((1, d), lambda m, k: (0, 0)))
        args.append(gain.reshape(1, d))
        out_specs.append(pl.BlockSpec((tm, d), lambda m, k: (m, 0), pipeline_mode=single))
        out_shape.append(jax.ShapeDtypeStruct((t, d), BF16))
    out = pl.pallas_call(
        functools.partial(_proj_kernel, tn=tn, nn=nn, gated=gated, with_norm=with_norm),
        grid=(t // tm, nn),
        in_specs=in_specs,
        out_specs=out_specs,
        out_shape=out_shape,
        compiler_params=_params("arbitrary", "arbitrary"),
        name="proj_gated" if gated else "proj_residual",
    )(*args)
    return (out[0], out[1]) if with_norm else (out[0], None)


def _tile_row(m, n, te, na):
    return (jnp.minimum(m, na[0] - 1), 0)


def _ffn_up_kernel(te_ref, na_ref, x_ref, wg_ref, wu_ref, a_ref, wgu_ref, *, tf):
    m = pl.program_id(1)

    @pl.when(m < na_ref[0])
    def _():
        @pl.when((m == 0) | (te_ref[m] != te_ref[jnp.maximum(m - 1, 0)]))
        def _():
            wgu_ref[:, :tf] = wg_ref[...].astype(BF16)
            wgu_ref[:, tf:] = wu_ref[...].astype(BF16)

        gu = _dot(x_ref[...], wgu_ref[...])
        a_ref[...] = (jax.nn.silu(gu[:, :tf]) * gu[:, tf:]).astype(BF16)


def ffn_up(x, wg, wu, tile_expert, n_active, *, tm, tf=512):
    r, d = x.shape
    ff = wg.shape[2]
    w_spec = pl.BlockSpec((None, d, tf), lambda n, m, te, na: (te[m], 0, n))
    return pl.pallas_call(
        functools.partial(_ffn_up_kernel, tf=tf),
        grid_spec=pltpu.PrefetchScalarGridSpec(
            num_scalar_prefetch=2,
            grid=(ff // tf, r // tm),
            in_specs=[pl.BlockSpec((tm, d), lambda n, m, te, na: _tile_row(m, n, te, na)), w_spec, w_spec],
            out_specs=pl.BlockSpec((tm, tf), lambda n, m, te, na: (jnp.minimum(m, na[0] - 1), n)),
            scratch_shapes=[pltpu.VMEM((d, 2 * tf), BF16)]),
        out_shape=jax.ShapeDtypeStruct((r, ff), BF16),
        compiler_params=_params("arbitrary", "arbitrary"),
        name="ffn_up",
    )(tile_expert, n_active, x, wg, wu)


def _ffn_down_kernel(te_ref, na_ref, a_ref, wd_ref, res_ref, g_ref, out_ref, hn_ref, *, tn, nn):
    n = pl.program_id(1)
    out_ref[:, pl.ds(pl.multiple_of(n * tn, tn), tn)] = _dot(a_ref[...], wd_ref[...].astype(BF16))

    @pl.when(n == nn - 1)
    def _():
        for r in range(out_ref.shape[0] // EPILOGUE_ROWS):
            rows = slice(r * EPILOGUE_ROWS, (r + 1) * EPILOGUE_ROWS)
            h_new = res_ref[rows, :] + out_ref[rows, :]
            out_ref[rows, :] = h_new
            hn_ref[rows, :] = _rms(h_new, g_ref[...]).astype(BF16)


def ffn_down(a, wd, tile_expert, n_active, *, tm, res, gain, tn=256):
    r, ff = a.shape
    d = wd.shape[2]
    nn = d // tn
    single = pl.Buffered(1)
    return pl.pallas_call(
        functools.partial(_ffn_down_kernel, tn=tn, nn=nn),
        grid_spec=pltpu.PrefetchScalarGridSpec(
            num_scalar_prefetch=2,
            grid=(r // tm, nn),
            in_specs=[pl.BlockSpec((tm, ff), _tile_row, pipeline_mode=single),
                      pl.BlockSpec((None, ff, tn), lambda m, n, te, na: (te[m], 0, n)),
                      pl.BlockSpec((tm, d), _tile_row, pipeline_mode=single),
                      pl.BlockSpec((1, d), lambda m, n, te, na: (0, 0))],
            out_specs=[pl.BlockSpec((tm, d), _tile_row, pipeline_mode=single),
                       pl.BlockSpec((tm, d), _tile_row, pipeline_mode=single)]),
        out_shape=[jax.ShapeDtypeStruct((r, d), F32), jax.ShapeDtypeStruct((r, d), BF16)],
        compiler_params=_params("arbitrary", "arbitrary"),
        name="ffn_down_dense",
    )(tile_expert, n_active, a, wd, res, gain.reshape(1, d))


DOWN_ROWS = 256


def _ffn_down_routed_kernel(te_ref, na_ref, a_ref, wd_ref, gate_ref, out_ref, wb_ref, *, tn):
    m = pl.program_id(1)
    per = TM_MOE // DOWN_ROWS
    tile = m // per

    @pl.when(tile < na_ref[0])
    def _():
        @pl.when((m == 0) | ((m % per == 0) & (te_ref[tile] != te_ref[jnp.maximum(tile - 1, 0)])))
        def _():
            wb_ref[...] = wd_ref[...].astype(BF16)

        part = gate_ref[...] * _dot(a_ref[...], wb_ref[...])
        for j in range(tn // LANES):
            out_ref[j] = part[:, j * LANES:(j + 1) * LANES]


def ffn_down_routed(a, wd, tile_expert, n_active, row_gate, tn=512):
    r, ff = a.shape
    d = wd.shape[2]
    per = TM_MOE // DOWN_ROWS
    rows = lambda n, m, te, na: (jnp.minimum(m, na[0] * per - 1), 0)
    return pl.pallas_call(
        functools.partial(_ffn_down_routed_kernel, tn=tn),
        grid_spec=pltpu.PrefetchScalarGridSpec(
            num_scalar_prefetch=2,
            grid=(d // tn, r // DOWN_ROWS),
            in_specs=[pl.BlockSpec((DOWN_ROWS, ff), rows),
                      pl.BlockSpec((None, ff, tn), lambda n, m, te, na: (te[m // per], 0, n)),
                      pl.BlockSpec((DOWN_ROWS, 1), rows)],
            out_specs=pl.BlockSpec((tn // LANES, DOWN_ROWS, LANES),
                                   lambda n, m, te, na: (n, jnp.minimum(m, na[0] * per - 1), 0)),
            scratch_shapes=[pltpu.VMEM((ff, tn), BF16)]),
        out_shape=jax.ShapeDtypeStruct((ROW_TILES, r, LANES), F32),
        compiler_params=_params("arbitrary", "arbitrary"),
        name="ffn_down_routed",
    )(tile_expert, n_active, a, wd, row_gate)


def _router_kernel(h_ref, g_ref, w_ref, o_ref, dense_ref):
    xn = _rms(h_ref[...], g_ref[...])
    logits = jnp.dot(xn, w_ref[...], preferred_element_type=F32, precision=lax.Precision.HIGHEST)
    lane = lax.broadcasted_iota(jnp.int32, logits.shape, 1)
    logits = jnp.where(lane < N_EXPERTS, logits, -jnp.inf)
    e = jnp.exp(logits - jnp.max(logits, axis=-1, keepdims=True))
    probs = e / jnp.sum(e, axis=-1, keepdims=True)
    p1 = jnp.max(probs, axis=-1, keepdims=True)
    e1 = jnp.min(jnp.where(probs == p1, lane, LANES), axis=-1, keepdims=True)
    rest = jnp.where(lane == e1, -1.0, probs)
    p2 = jnp.max(rest, axis=-1, keepdims=True)
    e2 = jnp.min(jnp.where(rest == p2, lane, LANES), axis=-1, keepdims=True)
    tot = p1 + p2
    g1 = p1 / tot
    g2 = p2 / tot
    o_ref[...] = jnp.where(lane == 0, e1.astype(F32),
                           jnp.where(lane == 1, e2.astype(F32), jnp.where(lane == 2, g1, g2)))
    dense_ref[...] = jnp.where(lane == e1, g1, jnp.where(lane == e2, g2, 0.0))


def route_top2(h, gain, w_router, tm):
    t, d = h.shape
    wpad = jnp.pad(w_router, ((0, 0), (0, LANES - N_EXPERTS)))
    return pl.pallas_call(
        _router_kernel,
        grid=(t // tm,),
        in_specs=[pl.BlockSpec((tm, d), lambda i: (i, 0)),
                  pl.BlockSpec((1, d), lambda i: (0, 0)),
                  pl.BlockSpec((d, LANES), lambda i: (0, 0))],
        out_specs=[pl.BlockSpec((tm, LANES), lambda i: (i, 0)),
                   pl.BlockSpec((tm, LANES), lambda i: (i, 0))],
        out_shape=[jax.ShapeDtypeStruct((t, LANES), F32), jax.ShapeDtypeStruct((t, LANES), F32)],
        compiler_params=_params("arbitrary"),
        name="route_top2",
    )(h, gain.reshape(1, d), wpad)


GATHER_TILE = TM_MOE


def _start_row_copies(idx_ref, first, count, src_ref, dst_ref, sem):
    def issue(j, carry):
        t = idx_ref[first + j]
        pltpu.make_async_copy(src_ref.at[:, pl.ds(t, 1), :], dst_ref.at[:, pl.ds(j, 1), :], sem).start()
        return carry

    lax.fori_loop(0, count, issue, 0, unroll=8)


def _wait_row_copies(src_ref, dst_ref, sem):
    pltpu.make_async_copy(src_ref.at[:, pl.ds(0, dst_ref.shape[1]), :], dst_ref, sem).wait()


def _gather_kernel(idx_ref, n_ref, src_ref, o_ref, buf_ref, sem):
    i = pl.program_id(0)

    @pl.when(i * GATHER_TILE < n_ref[0])
    def _():
        _start_row_copies(idx_ref, i * GATHER_TILE, GATHER_TILE, src_ref, buf_ref, sem)
        _wait_row_copies(src_ref, buf_ref, sem)
        for s in range(ROW_TILES):
            o_ref[:, s * LANES:(s + 1) * LANES] = buf_ref[s].astype(o_ref.dtype)


def gather_rows(src, idx, n):
    r = idx.shape[0]
    last = lambda i, idx_ref, n_ref: (jnp.minimum(i, (n_ref[0] - 1) // GATHER_TILE), 0)
    return pl.pallas_call(
        _gather_kernel,
        grid_spec=pltpu.PrefetchScalarGridSpec(
            num_scalar_prefetch=2,
            grid=(r // GATHER_TILE,),
            in_specs=[pl.BlockSpec(memory_space=pl.ANY)],
            out_specs=pl.BlockSpec((GATHER_TILE, D_MODEL), last),
            scratch_shapes=[pltpu.VMEM((ROW_TILES, GATHER_TILE, LANES), F32), pltpu.SemaphoreType.DMA(())]),
        out_shape=jax.ShapeDtypeStruct((r, D_MODEL), BF16),
        compiler_params=_params("arbitrary"),
        name="gather_rows",
    )(idx, n, src)


def _combine_kernel(pos_ref, h_ref, o_ref, g_ref, h2_ref, hn_ref, o1_ref, o2_ref, sem1, sem2, *, tm, t):
    i = pl.program_id(0)
    _start_row_copies(pos_ref, i * tm, tm, o_ref, o1_ref, sem1)
    _start_row_copies(pos_ref, t + i * tm, tm, o_ref, o2_ref, sem2)
    _wait_row_copies(o_ref, o1_ref, sem1)
    _wait_row_copies(o_ref, o2_ref, sem2)
    for s in range(ROW_TILES):
        sl = slice(s * LANES, (s + 1) * LANES)
        h2_ref[:, sl] = h_ref[:, sl] + (o1_ref[s] + o2_ref[s])
    hn_ref[...] = _rms(h2_ref[...], g_ref[...]).astype(BF16)


def combine_rows(h, o, pos, gain, tm=TM_MID):
    t, d = h.shape
    tiled = pltpu.VMEM((ROW_TILES, tm, LANES), F32)
    return pl.pallas_call(
        functools.partial(_combine_kernel, tm=tm, t=t),
        grid_spec=pltpu.PrefetchScalarGridSpec(
            num_scalar_prefetch=1,
            grid=(t // tm,),
            in_specs=[pl.BlockSpec((tm, d), lambda i, pos_ref: (i, 0)),
                      pl.BlockSpec(memory_space=pl.ANY),
                      pl.BlockSpec((1, d), lambda i, pos_ref: (0, 0))],
            out_specs=[pl.BlockSpec((tm, d), lambda i, pos_ref: (i, 0)),
                       pl.BlockSpec((tm, d), lambda i, pos_ref: (i, 0))],
            scratch_shapes=[tiled, tiled, pltpu.SemaphoreType.DMA(()), pltpu.SemaphoreType.DMA(())]),
        out_shape=[jax.ShapeDtypeStruct((t, d), F32), jax.ShapeDtypeStruct((t, d), BF16)],
        compiler_params=_params("arbitrary"),
        name="combine_rows",
    )(pos, h, o, gain.reshape(1, d))


def moe_prompt(h, ffn_gain, w_router, wg, wu, wd, layer, ple_gain):
    xn = norm_rows(h, ffn_gain, F32, slabs=True)
    r, _ = route_top2(h, ffn_gain, w_router, TM_BIG)
    experts = jnp.concatenate([r[:, 0], r[:, 1]]).astype(jnp.int32)
    gates = jnp.concatenate([r[:, 2], r[:, 3]])
    onehot = (experts[:, None] == jnp.arange(N_EXPERTS)[None, :]).astype(jnp.int32)
    rank = jnp.sum(jnp.cumsum(onehot, axis=0) * onehot, axis=1) - 1
    counts = jnp.sum(onehot, axis=0)
    tiles = (counts + TM_MOE - 1) // TM_MOE
    tile_end = jnp.cumsum(tiles)
    tile_start = tile_end - tiles
    dest = jnp.sum(onehot * tile_start[None, :], axis=1) * TM_MOE + rank
    token = jnp.tile(jnp.arange(T, dtype=jnp.int32), 2)
    src_token = jnp.zeros((R_PAD,), jnp.int32).at[dest].set(token)
    row_gate = jnp.zeros((R_PAD,), F32).at[dest].set(gates).reshape(R_PAD, 1)
    n_active = tile_end[-1:].astype(jnp.int32)
    tile_ids = jnp.minimum(jnp.arange(MOE_TILES, dtype=jnp.int32), n_active[0] - 1)
    tile_expert = jnp.sum((tile_ids[:, None] >= tile_end[None, :]).astype(jnp.int32), axis=1)
    xs = gather_rows(xn, src_token, n_active * TM_MOE)
    flat = lambda w: w.reshape((-1,) + w.shape[2:])
    tile_w = tile_expert + layer * N_EXPERTS
    a = ffn_up(xs, flat(wg), flat(wu), tile_w, n_active, tm=TM_MOE)
    o = ffn_down_routed(a, flat(wd), tile_w, n_active, row_gate)
    return combine_rows(h, o, dest, ple_gain)


def _headnorm_kernel(q_ref, k_ref, qg_ref, kg_ref, qn_ref, kn_ref):
    def norm_pairs(x_ref, g_ref, o_ref, scale):
        lo = lax.broadcasted_iota(jnp.int32, (x_ref.shape[0], LANES), 1) < HEAD_DIM
        for j in range(x_ref.shape[1] // LANES):
            x = x_ref[:, j * LANES:(j + 1) * LANES]
            s = x * x
            s_lo = jnp.sum(jnp.where(lo, s, 0.0), axis=-1, keepdims=True)
            s_hi = jnp.sum(jnp.where(lo, 0.0, s), axis=-1, keepdims=True)
            ms = jnp.where(lo, s_lo, s_hi) * (1.0 / HEAD_DIM)
            y = x * lax.rsqrt(ms + NORM_EPS) * g_ref[...]
            o_ref[:, j * LANES:(j + 1) * LANES] = (y * scale).astype(o_ref.dtype)

    norm_pairs(q_ref, qg_ref, qn_ref, HEAD_DIM ** -0.5)
    norm_pairs(k_ref, kg_ref, kn_ref, 1.0)


def head_norm(qkv, q_gain, k_gain, q_dtype, tm):
    t = qkv.shape[0]
    return pl.pallas_call(
        _headnorm_kernel,
        grid=(t // tm,),
        in_specs=[pl.BlockSpec((tm, NQ), lambda i: (i, 0)),
                  pl.BlockSpec((tm, NKV), lambda i: (i, NQ // NKV)),
                  pl.BlockSpec((1, LANES), lambda i: (0, 0)),
                  pl.BlockSpec((1, LANES), lambda i: (0, 0))],
        out_specs=[pl.BlockSpec((tm, NQ), lambda i: (i, 0)),
                   pl.BlockSpec((tm, NKV), lambda i: (i, 0))],
        out_shape=[jax.ShapeDtypeStruct((t, NQ), q_dtype), jax.ShapeDtypeStruct((t, NKV), F32)],
        compiler_params=_params("arbitrary"),
        name="head_norm",
    )(qkv, qkv, jnp.tile(q_gain, 2).reshape(1, LANES), jnp.tile(k_gain, 2).reshape(1, LANES))


def _attn_kernel(slope_ref, sink_ref, q_ref, kp_ref, kc_ref, vp_ref, vc_ref, o_ref):
    rows = WINDOW
    nkeys = 2 * WINDOW
    rep = ATTN_HEADS // ATTN_KV_HEADS
    qi = lax.broadcasted_iota(jnp.int32, (rows, nkeys), 0)
    kj = lax.broadcasted_iota(jnp.int32, (rows, nkeys), 1)
    delta = WINDOW + qi - kj
    mask = (delta >= 0) & (delta <= WINDOW) & ((kj >= WINDOW) | (pl.program_id(1) > 0))
    deltaf = delta.astype(F32)
    lo_q = lax.broadcasted_iota(jnp.int32, (rows, LANES), 1) < HEAD_DIM

    for jp in range(ATTN_KV_HEADS // 2):
        ksl = slice(jp * LANES, (jp + 1) * LANES)
        k2 = jnp.concatenate([kp_ref[:, ksl], kc_ref[:, ksl]], axis=0).astype(BF16)
        v2 = jnp.concatenate([vp_ref[:, ksl], vc_ref[:, ksl]], axis=0)
        v2r = pltpu.roll(v2, HEAD_DIM, 1).astype(BF16)
        v2 = v2.astype(BF16)
        for qp in range(rep):
            col = (jp * rep + qp) * LANES
            q2 = q_ref[:, col:col + LANES]
            q2r = pltpu.roll(q2, HEAD_DIM, 1)
            g_half = qp // (rep // 2)
            outs = []
            for hh in range(2):
                h = (jp * rep + qp) * 2 + hh
                qsel = q2 if hh == g_half else q2r
                qm = jnp.where(lo_q if g_half == 0 else ~lo_q, qsel, jnp.zeros_like(qsel))
                s = _dot_nt(qm, k2) - slope_ref[h] * deltaf
                s = jnp.where(mask, s, -jnp.inf)
                sink = sink_ref[h]
                m = jnp.maximum(jnp.max(s, axis=-1, keepdims=True), sink)
                e = jnp.exp(s - m)
                probs = e / (jnp.sum(e, axis=-1, keepdims=True) + jnp.exp(sink - m))
                outs.append(_dot(probs.astype(BF16), v2 if hh == g_half else v2r))
            o_ref[:, col:col + LANES] = jnp.where(lo_q, outs[0], outs[1]).astype(o_ref.dtype)


def attention_prompt(hn, h, w_qkv, layer, q_gain, k_gain, sinks, w_o, next_gain):
    qkv = matmul_cols(hn, w_qkv, layer, NQ + 2 * NKV)
    qn, kn = head_norm(qkv, q_gain, k_gain, BF16, TM_MID)
    nblk = SEQ // WINDOW
    vcol = NQ // NKV + 1
    cur = lambda c: (lambda b, n: (b * nblk + n, c))
    prev = lambda c: (lambda b, n: (jnp.maximum(b * nblk + n - 1, 0), c))
    smem = pl.BlockSpec(memory_space=pltpu.SMEM)
    o = pl.pallas_call(
        _attn_kernel,
        grid=(BATCH, nblk),
        in_specs=[smem, smem,
                  pl.BlockSpec((WINDOW, NQ), cur(0)),
                  pl.BlockSpec((WINDOW, NKV), prev(0)),
                  pl.BlockSpec((WINDOW, NKV), cur(0)),
                  pl.BlockSpec((WINDOW, NKV), prev(vcol)),
                  pl.BlockSpec((WINDOW, NKV), cur(vcol))],
        out_specs=pl.BlockSpec((WINDOW, NQ), cur(0)),
        out_shape=jax.ShapeDtypeStruct((T, NQ), BF16),
        compiler_params=_params("arbitrary", "arbitrary"),
        name="window_attention",
    )(_alibi_slopes(), sinks, qn, kn, kn, qkv, qkv)
    h2, hn2 = proj_residual(o, w_o, layer, h, next_gain)
    kv_shape = (BATCH, WINDOW, ATTN_KV_HEADS, HEAD_DIM)
    new_k = kn.reshape(BATCH, SEQ, NKV)[:, -WINDOW:].reshape(kv_shape)
    new_v = qkv[:, NQ + NKV:].reshape(BATCH, SEQ, NKV)[:, -WINDOW:].reshape(kv_shape)
    return h2, hn2, (new_k, new_v)


SCONV_TILE = 512
SCONV_TAIL = SUBLANES


def _sconv_kernel(gb_ref, gc_ref, u_ref, gcp_ref, up_ref, w_ref, y_ref, tail_ref):
    rows = gb_ref.shape[0]
    cu = gc_ref[...] * u_ref[...]
    prev = jnp.where(pl.program_id(1) > 0, gcp_ref[...] * up_ref[...], 0.0)
    row = lax.broadcasted_iota(jnp.int32, cu.shape, 0)
    s1 = jnp.where(row == 0, prev[SCONV_TAIL - 1:], pltpu.roll(cu, 1, 0))
    s2 = jnp.where(row == 0, prev[SCONV_TAIL - 2:SCONV_TAIL - 1],
                   jnp.where(row == 1, prev[SCONV_TAIL - 1:], pltpu.roll(cu, 2, 0)))
    conv = w_ref[0:1] * s2 + w_ref[1:2] * s1 + w_ref[2:3] * cu
    y_ref[...] = (gb_ref[...] * conv).astype(y_ref.dtype)
    tail_ref[...] = cu[rows - SCONV_TAIL:]


def sconv_prompt(hn, h, w_in, conv_w, layer, w_out, next_gain):
    d = D_MODEL
    proj = matmul_cols(hn, w_in, layer, 3 * d)
    nt = SEQ // SCONV_TILE
    tpb = SCONV_TILE // SCONV_TAIL
    cur = lambda c: (lambda b, i: (b * nt + i, c))
    prev = lambda c: (lambda b, i: (jnp.maximum((b * nt + i) * tpb - 1, 0), c))
    y, tails = pl.pallas_call(
        _sconv_kernel,
        grid=(BATCH, nt),
        in_specs=[pl.BlockSpec((SCONV_TILE, d), cur(0)),
                  pl.BlockSpec((SCONV_TILE, d), cur(1)),
                  pl.BlockSpec((SCONV_TILE, d), cur(2)),
                  pl.BlockSpec((SCONV_TAIL, d), prev(1)),
                  pl.BlockSpec((SCONV_TAIL, d), prev(2)),
                  pl.BlockSpec((None, SCONV_WIDTH, d), lambda b, i: (layer, 0, 0))],
        out_specs=[pl.BlockSpec((SCONV_TILE, d), cur(0)),
                   pl.BlockSpec((None, SCONV_TAIL, d), lambda b, i: (b * nt + i, 0, 0))],
        out_shape=[jax.ShapeDtypeStruct((T, d), BF16),
                   jax.ShapeDtypeStruct((BATCH * nt, SCONV_TAIL, d), F32)],
        compiler_params=_params("arbitrary", "arbitrary"),
        name="sconv_prompt",
    )(proj, proj, proj, proj, proj, conv_w)
    h2, hn2 = proj_residual(y, w_out, layer, h, next_gain)
    new_state = tails.reshape(BATCH, nt, SCONV_TAIL, d)[:, -1, -(SCONV_WIDTH - 1):]
    return h2, hn2, new_state


XBC_CHUNK = 512
GROUP_W = D_INNER // SSM_GROUPS
HEADS_PER_GROUP = SSM_HEADS // SSM_GROUPS


def _ssd_kernel(z_ref, xc_ref, xp_ref, dt_ref, cw_ref, cb_ref, dtb_ref, alog_ref, dsk_ref, nw_ref,
                y_ref, hfin_ref, state_ref, xact_ref, acst_ref, yg_ref):
    c = pl.program_id(1)
    q = SSM_CHUNK

    @pl.when(c == 0)
    def _():
        state_ref[...] = jnp.zeros_like(state_ref)

    has_prev = c > 0
    row = lax.broadcasted_iota(jnp.int32, (q, XBC_CHUNK), 0)
    for j in range(SSM_CONV_DIM // XBC_CHUNK):
        sl = slice(j * XBC_CHUNK, (j + 1) * XBC_CHUNK)
        cur = xc_ref[:, sl]
        prv = jnp.where(has_prev, xp_ref[:, sl], 0.0)
        acc = None
        for k in range(SSM_CONV_WIDTH):
            sh = SSM_CONV_WIDTH - 1 - k
            if sh == 0:
                term = cur
            else:
                term = jnp.where(row < sh, pltpu.roll(prv, sh, 0), pltpu.roll(cur, sh, 0))
            term = cw_ref[k:k + 1, sl] * term
            acc = term if acc is None else acc + term
        xact_ref[:, sl] = jax.nn.silu(acc + cb_ref[:, sl])

    dt = _softplus(dt_ref[...] + dtb_ref[...])
    da = dt * (-jnp.exp(alog_ref[...]))
    tri = (lax.broadcasted_iota(jnp.int32, (q, q), 0) >= lax.broadcasted_iota(jnp.int32, (q, q), 1))
    a_cs = jnp.dot(tri.astype(F32), da, preferred_element_type=F32, precision=lax.Precision.HIGHEST)
    acst_ref[...] = a_cs.T
    lane = lax.broadcasted_iota(jnp.int32, (q, LANES), 1)

    def group(g, carry):
        goff = pl.multiple_of(g * GROUP_W, GROUP_W)
        xs_g = xact_ref[:, pl.ds(goff, GROUP_W)]
        bm = xact_ref[:, pl.ds(pl.multiple_of(D_INNER + g * D_STATE, D_STATE), D_STATE)]
        cm = xact_ref[:, pl.ds(pl.multiple_of(D_INNER + SSM_GROUPS * D_STATE + g * D_STATE, D_STATE), D_STATE)]
        cm_b = cm.astype(BF16)
        cb = _dot_nt(cm_b, bm.astype(BF16))
        for hh in range(HEADS_PER_GROUP):
            h = g * HEADS_PER_GROUP + hh
            sel = lane == h
            acs_col = jnp.sum(jnp.where(sel, a_cs, 0.0), axis=1, keepdims=True)
            dt_col = jnp.sum(jnp.where(sel, dt, 0.0), axis=1, keepdims=True)
            acs_row = acst_ref[pl.ds(h, 1), :]
            acs_last = acs_col[q - 1:q]
            xs = xs_g[:, hh * SSM_HEAD_DIM:(hh + 1) * SSM_HEAD_DIM]
            xdt = (xs * dt_col).astype(BF16)
            seg = acs_col - acs_row
            decay_in = jnp.where(tri, jnp.exp(jnp.where(tri, seg, 0.0)), 0.0)
            y = _dot((cb * decay_in).astype(BF16), xdt)
            h_in = state_ref[h]
            y = y + _dot_nt(cm_b, h_in.astype(BF16)) * jnp.exp(acs_col)
            decay_out = jnp.exp(acs_last - acs_col)
            st = _dot_tn(xdt, (bm * decay_out).astype(BF16))
            state_ref[h] = jnp.exp(acs_last) * h_in + st
            yg_ref[:, hh * SSM_HEAD_DIM:(hh + 1) * SSM_HEAD_DIM] = y
        y_g = yg_ref[...] + dsk_ref[:, pl.ds(goff, GROUP_W)] * xs_g
        gg = y_g * jax.nn.silu(z_ref[:, pl.ds(goff, GROUP_W)])
        gg = gg * lax.rsqrt(jnp.mean(gg * gg, axis=-1, keepdims=True) + NORM_EPS)
        y_ref[:, pl.ds(goff, GROUP_W)] = (gg * nw_ref[:, pl.ds(goff, GROUP_W)]).astype(y_ref.dtype)
        return carry

    lax.fori_loop(0, SSM_GROUPS, group, 0)

    @pl.when(c == pl.num_programs(1) - 1)
    def _():
        hfin_ref[...] = state_ref[...]


def _pad_heads(v):
    return jnp.pad(v, (0, LANES - SSM_HEADS)).reshape(1, LANES)


def ssd_prompt(hn, h, w_in, layer, conv_w, conv_b, dt_bias, a_log, d_skip, norm_w, w_out, next_gain):
    cd = SSM_CONV_DIM
    z = matmul_cols(hn, w_in, layer, D_INNER, col_off=0)
    xbc = matmul_cols(hn, w_in, layer, cd, col_off=D_INNER)
    w_dt = jnp.pad(w_in[layer, :, D_INNER + cd:], ((0, 0), (0, LANES - SSM_HEADS)))[None]
    dt_raw = matmul_cols(hn, w_dt, 0, LANES)
    dsk_full = jnp.repeat(d_skip, SSM_HEAD_DIM).reshape(1, D_INNER)
    nc = SEQ // SSM_CHUNK
    cur = lambda b, c: (b * nc + c, 0)
    whole = lambda b, c: (0, 0)
    y, h_fin = pl.pallas_call(
        _ssd_kernel,
        grid=(BATCH, nc),
        in_specs=[pl.BlockSpec((SSM_CHUNK, D_INNER), cur),
                  pl.BlockSpec((SSM_CHUNK, cd), cur),
                  pl.BlockSpec((SSM_CHUNK, cd), lambda b, c: (jnp.maximum(b * nc + c - 1, 0), 0)),
                  pl.BlockSpec((SSM_CHUNK, LANES), cur),
                  pl.BlockSpec((SSM_CONV_WIDTH, cd), whole),
                  pl.BlockSpec((1, cd), whole),
                  pl.BlockSpec((1, LANES), whole),
                  pl.BlockSpec((1, LANES), whole),
                  pl.BlockSpec((1, D_INNER), whole),
                  pl.BlockSpec((1, D_INNER), whole)],
        out_specs=[pl.BlockSpec((SSM_CHUNK, D_INNER), cur),
                   pl.BlockSpec((None, SSM_HEADS, SSM_HEAD_DIM, D_STATE), lambda b, c: (b, 0, 0, 0))],
        out_shape=[jax.ShapeDtypeStruct((T, D_INNER), BF16),
                   jax.ShapeDtypeStruct((BATCH, SSM_HEADS, SSM_HEAD_DIM, D_STATE), F32)],
        scratch_shapes=[pltpu.VMEM((SSM_HEADS, SSM_HEAD_DIM, D_STATE), F32),
                        pltpu.VMEM((SSM_CHUNK, cd), F32),
                        pltpu.VMEM((LANES, SSM_CHUNK), F32),
                        pltpu.VMEM((SSM_CHUNK, GROUP_W), F32)],
        compiler_params=_params("arbitrary", "arbitrary"),
        name="ssd_prompt",
    )(z, xbc, xbc, dt_raw, conv_w, conv_b.reshape(1, cd), _pad_heads(dt_bias), _pad_heads(a_log), dsk_full,
      norm_w.reshape(1, D_INNER))
    h2, hn2 = proj_residual(y, w_out, layer, h, next_gain, tm=TM_MID)
    new_conv = xbc.reshape(BATCH, SEQ, cd)[:, -(SSM_CONV_WIDTH - 1):]
    return h2, hn2, (h_fin, new_conv)


def _split(x):
    hi = x.astype(BF16)
    lo = (x - hi.astype(F32)).astype(BF16)
    return hi, lo


def _hp_kernel(*refs, nk, x_mode, epi):
    refs = list(refs)
    if x_mode == "norm":
        x_ref, g_ref = refs[:2]
        refs = refs[2:]
        x = _rms(x_ref[...], g_ref[...])
    elif x_mode == "swiglu":
        g_ref, u_ref = refs[:2]
        refs = refs[2:]
        x = jax.nn.silu(g_ref[...]) * u_ref[...]
    else:
        x = refs[0][...]
        refs = refs[1:]
    w_ref = refs[0]
    refs = refs[1:]
    if epi == "gated":
        res_ref, pp_ref, o_ref = refs
    elif epi == "res":
        res_ref, o_ref = refs
    else:
        (o_ref,) = refs
    k = pl.program_id(2)
    rows = x.shape[0]
    x_hi, x_lo = _split(x)
    w_hi, w_lo = _split(w_ref[...])
    both = _dot(jnp.concatenate([x_hi, x_lo], axis=0), w_hi)
    part = both[:rows] + both[rows:] + _dot(x_hi, w_lo)

    @pl.when(k == 0)
    def _():
        o_ref[...] = part

    @pl.when(k > 0)
    def _():
        o_ref[...] += part

    if epi is not None:
        @pl.when(k == nk - 1)
        def _():
            acc = o_ref[...]
            if epi == "gated":
                acc = jax.nn.sigmoid(acc) * pp_ref[...]
            o_ref[...] = res_ref[...] + acc


def hp_linear(w, widx, n_out, *, x=None, norm=None, swiglu=None, col_off=0, n_e=1, tk, tn, res=None, pp=None):
    kdim = w.shape[1]
    nk = kdim // tk
    off = col_off // tn
    assert kdim % tk == 0 and n_out % tn == 0 and col_off % tn == 0
    if norm is not None:
        assert nk == 1
        x_mode = "norm"
        args = [norm[0], norm[1].reshape(1, kdim)]
        in_specs = [pl.BlockSpec((NS, tk), lambda e, n, k: (0, k)),
                    pl.BlockSpec((1, tk), lambda e, n, k: (0, k))]
    elif swiglu is not None:
        x_mode = "swiglu"
        args = list(swiglu)
        in_specs = [pl.BlockSpec((None, NS, tk), lambda e, n, k: (e, 0, k))] * 2
    else:
        x_mode = "plain"
        args = [x]
        in_specs = [pl.BlockSpec((NS, tk), lambda e, n, k: (0, k))]
    args.append(w)
    in_specs.append(pl.BlockSpec((None, tk, tn), lambda e, n, k: (widx + e, k, n + off)))
    epi = None
    if res is not None:
        epi = "res"
        args.append(res)
        in_specs.append(pl.BlockSpec((NS, tn), lambda e, n, k: (0, n)))
        if pp is not None:
            epi = "gated"
            args.append(pp)
            in_specs.append(pl.BlockSpec((NS, tn), lambda e, n, k: (0, n)))
    return pl.pallas_call(
        functools.partial(_hp_kernel, nk=nk, x_mode=x_mode, epi=epi),
        grid=(n_e, n_out // tn, nk),
        in_specs=in_specs,
        out_specs=pl.BlockSpec((None, NS, tn), lambda e, n, k: (e, 0, n)),
        out_shape=jax.ShapeDtypeStruct((n_e, NS, n_out), F32),
        compiler_params=_params("arbitrary", "arbitrary", "arbitrary"),
        name="hp_linear_" + x_mode,
    )(*args)


def _attn_step_kernel(slope_ref, sink_ref, q_ref, kc_ref, vc_ref, kn_ref, vn_ref, o_ref):
    rep = ATTN_HEADS // ATTN_KV_HEADS
    lo = lax.broadcasted_iota(jnp.int32, (1, LANES), 1) < HEAD_DIM
    delta = (WINDOW - lax.broadcasted_iota(jnp.int32, (WINDOW, 1), 0)).astype(F32)
    for jp in range(ATTN_KV_HEADS // 2):
        ksl = slice(jp * LANES, (jp + 1) * LANES)
        k2 = kc_ref[:, ksl]
        v2 = vc_ref[:, ksl]
        kn2 = kn_ref[:, ksl]
        vn2 = vn_ref[:, ksl]
        for qp in range(rep):
            col = (jp * rep + qp) * LANES
            q2 = q_ref[:, col:col + LANES]
            q2r = pltpu.roll(q2, HEAD_DIM, 1)
            g_half = qp // (rep // 2)
            kv_lanes = lo if g_half == 0 else ~lo
            out = jnp.zeros((1, LANES), F32)
            for hh in range(2):
                h = (jp * rep + qp) * 2 + hh
                qm = jnp.where(kv_lanes, q2 if hh == g_half else q2r, 0.0)
                s = jnp.sum(k2 * qm, axis=1, keepdims=True) - slope_ref[h] * delta
                s_new = jnp.sum(kn2 * qm, axis=1, keepdims=True)
                sink = sink_ref[h]
                m = jnp.maximum(jnp.maximum(jnp.max(s, axis=0, keepdims=True), s_new), sink)
                e = jnp.exp(s - m)
                e_new = jnp.exp(s_new - m)
                denom = jnp.sum(e, axis=0, keepdims=True) + e_new + jnp.exp(sink - m)
                o = jnp.sum((e / denom) * v2, axis=0, keepdims=True) + (e_new / denom) * vn2
                if hh != g_half:
                    o = pltpu.roll(o, HEAD_DIM, 1)
                out = jnp.where(lo if hh == 0 else ~lo, o, out)
            o_ref[:, col:col + LANES] = out


def _sconv_step_kernel(proj_ref, p_ref, w_ref, y_ref, cu_ref):
    d = D_MODEL
    cu = proj_ref[:, d:2 * d] * proj_ref[:, 2 * d:]
    conv = w_ref[0:1] * p_ref[0] + w_ref[1:2] * p_ref[1] + w_ref[2:3] * cu
    y_ref[...] = proj_ref[:, :d] * conv
    cu_ref[...] = cu


def _ssd_step_pre_kernel(xc_ref, p_ref, dt_ref, cw_ref, cb_ref, dtb_ref, alog_ref, xact_ref, dt_out_ref, dec_ref):
    acc = cw_ref[0:1] * p_ref[0] + cw_ref[1:2] * p_ref[1] + cw_ref[2:3] * p_ref[2] + cw_ref[3:4] * xc_ref[...]
    xact_ref[...] = jax.nn.silu(acc + cb_ref[...])
    dt = _softplus(dt_ref[...] + dtb_ref[...])
    dt_out_ref[...] = dt
    dec_ref[...] = jnp.exp(dt * (-jnp.exp(alog_ref[...])))


def _ssd_step_kernel(dt_ref, dec_ref, dsk_ref, xt_ref, bm_ref, cm_ref, h0_ref, yt_ref, h1_ref):
    b = pl.program_id(0)
    xt = xt_ref[...]
    lane = lax.broadcasted_iota(jnp.int32, xt.shape, 1)

    def head(h, yt):
        g = h // HEADS_PER_GROUP
        sel = lane == h
        x_col = jnp.sum(jnp.where(sel, xt, 0.0), axis=1, keepdims=True)
        bm = bm_ref[pl.ds(g, 1), :]
        cm = cm_ref[pl.ds(g, 1), :]
        h1 = dec_ref[b, h] * h0_ref[h] + (dt_ref[b, h] * x_col) * bm
        h1_ref[h] = h1
        y_col = jnp.sum(h1 * cm, axis=1, keepdims=True) + dsk_ref[h] * x_col
        return jnp.where(sel, y_col, yt)

    yt_ref[...] = lax.fori_loop(0, SSM_HEADS, head, jnp.zeros(xt.shape, F32), unroll=4)


def _ssd_gate_kernel(y_ref, z_ref, nw_ref, o_ref):
    for g in range(SSM_GROUPS):
        sl = slice(g * GROUP_W, (g + 1) * GROUP_W)
        gg = y_ref[:, sl] * jax.nn.silu(z_ref[:, sl])
        gg = gg * lax.rsqrt(jnp.mean(gg * gg, axis=-1, keepdims=True) + NORM_EPS)
        o_ref[:, sl] = gg * nw_ref[:, sl]


def _moe_mix_kernel(h_ref, gates_ref, d_ref, o_ref):
    lane = lax.broadcasted_iota(jnp.int32, gates_ref.shape, 1)
    y = jnp.zeros(h_ref.shape, F32)
    for e in range(N_EXPERTS):
        gate = jnp.sum(jnp.where(lane == e, gates_ref[...], 0.0), axis=1, keepdims=True)
        y = y + gate * d_ref[e]
    o_ref[...] = h_ref[...] + y


def _whole(shape):
    return pl.BlockSpec(shape, lambda *_: (0,) * len(shape))


def _small_call(body, out_shapes, *args, name):
    multi = isinstance(out_shapes, (list, tuple))
    outs = list(out_shapes) if multi else [out_shapes]
    res = pl.pallas_call(
        body,
        grid=(1,),
        in_specs=[_whole(a.shape) for a in args],
        out_specs=[_whole(o.shape) for o in outs],
        out_shape=outs,
        compiler_params=_params("arbitrary"),
        name=name,
    )(*args)
    return res if multi else res[0]


def _sds(*shape):
    return jax.ShapeDtypeStruct(shape, F32)


def attention_sample(h, norm_gain, cache_k, cache_v, w_qkv, layer, q_gain, k_gain, sinks, w_o):
    qkv = hp_linear(w_qkv, layer, NQ + 2 * NKV, norm=(h, norm_gain), tk=D_MODEL, tn=1024)[0]
    qn, kn = head_norm(qkv, q_gain, k_gain, F32, NS)
    v_new = qkv[:, NQ + NKV:]
    smem = pl.BlockSpec(memory_space=pltpu.SMEM)
    rowsq = pl.BlockSpec((None, 1, NQ), lambda b: (b, 0, 0))
    rowsk = pl.BlockSpec((None, 1, NKV), lambda b: (b, 0, 0))
    cache = pl.BlockSpec((None, WINDOW, NKV), lambda b: (b, 0, 0))
    o = pl.pallas_call(
        _attn_step_kernel,
        grid=(NS,),
        in_specs=[smem, smem, rowsq, cache, cache, rowsk, rowsk],
        out_specs=rowsq,
        out_shape=_sds(NS, 1, NQ),
        compiler_params=_params("arbitrary"),
        name="attention_sample",
    )(_alibi_slopes(), sinks, qn.reshape(NS, 1, NQ), cache_k.reshape(NS, WINDOW, NKV),
      cache_v.reshape(NS, WINDOW, NKV), kn.reshape(NS, 1, NKV), v_new.reshape(NS, 1, NKV))
    h2 = hp_linear(w_o, layer, D_MODEL, x=o.reshape(NS, NQ), tk=1024, tn=1024, res=h)[0]
    kv_row = (NS, 1, ATTN_KV_HEADS, HEAD_DIM)
    new_k = jnp.concatenate([cache_k[:, 1:], kn.reshape(kv_row)], axis=1)
    new_v = jnp.concatenate([cache_v[:, 1:], v_new.reshape(kv_row)], axis=1)
    return h2, (new_k, new_v)


def sconv_sample(h, norm_gain, past, w_in, conv_w, layer, w_out):
    d = D_MODEL
    proj = hp_linear(w_in, layer, 3 * d, norm=(h, norm_gain), tk=d, tn=512)[0]
    y, cu = _small_call(_sconv_step_kernel, [_sds(NS, d), _sds(NS, d)],
                        proj, past.transpose(1, 0, 2), conv_w[layer], name="sconv_sample")
    h2 = hp_linear(w_out, layer, d, x=y, tk=1024, tn=1024, res=h)[0]
    return h2, jnp.concatenate([past[:, 1:], cu[:, None]], axis=1)


def ssd_sample(h, norm_gain, conv_past, h0, w_in, layer, conv_w, conv_b, dt_bias, a_log, d_skip, norm_w, w_out):
    cd = SSM_CONV_DIM
    z = hp_linear(w_in, layer, D_INNER, norm=(h, norm_gain), tk=D_MODEL, tn=1024)[0]
    xbc = hp_linear(w_in, layer, cd, norm=(h, norm_gain), col_off=D_INNER, tk=D_MODEL, tn=1024)[0]
    w_dt = jnp.pad(w_in[layer, :, D_INNER + cd:], ((0, 0), (0, LANES - SSM_HEADS)))[None]
    dt_raw = hp_linear(w_dt, 0, LANES, norm=(h, norm_gain), tk=D_MODEL, tn=LANES)[0]
    xact, dt, dec = _small_call(
        _ssd_step_pre_kernel, [_sds(NS, cd), _sds(NS, LANES), _sds(NS, LANES)],
        xbc, conv_past.transpose(1, 0, 2), dt_raw, conv_w, conv_b.reshape(1, cd), _pad_heads(dt_bias),
        _pad_heads(a_log), name="ssd_sample_pre")
    xt = xact[:, :D_INNER].reshape(NS, SSM_HEADS, SSM_HEAD_DIM).transpose(0, 2, 1)
    bm = xact[:, D_INNER:D_INNER + SSM_GROUPS * D_STATE].reshape(NS, SSM_GROUPS, D_STATE)
    cm = xact[:, D_INNER + SSM_GROUPS * D_STATE:].reshape(NS, SSM_GROUPS, D_STATE)
    smem = pl.BlockSpec(memory_space=pltpu.SMEM)
    per_b = lambda *shape: pl.BlockSpec((None,) + shape, lambda b: (b,) + (0,) * len(shape))
    yt, h1 = pl.pallas_call(
        _ssd_step_kernel,
        grid=(NS,),
        in_specs=[smem, smem, smem,
                  per_b(SSM_HEAD_DIM, SSM_HEADS), per_b(SSM_GROUPS, D_STATE), per_b(SSM_GROUPS, D_STATE),
                  per_b(SSM_HEADS, SSM_HEAD_DIM, D_STATE)],
        out_specs=[per_b(SSM_HEAD_DIM, SSM_HEADS), per_b(SSM_HEADS, SSM_HEAD_DIM, D_STATE)],
        out_shape=[_sds(NS, SSM_HEAD_DIM, SSM_HEADS), _sds(NS, SSM_HEADS, SSM_HEAD_DIM, D_STATE)],
        compiler_params=_params("arbitrary"),
        name="ssd_sample_step",
    )(dt[:, :SSM_HEADS], dec[:, :SSM_HEADS], d_skip, xt, bm, cm, h0)
    y = yt.transpose(0, 2, 1).reshape(NS, D_INNER)
    yn = _small_call(_ssd_gate_kernel, _sds(NS, D_INNER), y, z, norm_w.reshape(1, D_INNER), name="ssd_sample_gate")
    h2 = hp_linear(w_out, layer, D_MODEL, x=yn, tk=1024, tn=1024, res=h)[0]
    return h2, (h1, jnp.concatenate([conv_past[:, 1:], xbc[:, None]], axis=1))


def ffn_sample(h, norm_gain, wg, wu, wd, widx, n_e):
    g = hp_linear(wg, widx, D_FF, norm=(h, norm_gain), n_e=n_e, tk=D_MODEL, tn=1024)
    u = hp_linear(wu, widx, D_FF, norm=(h, norm_gain), n_e=n_e, tk=D_MODEL, tn=1024)
    return g, u


def ple_sample(h, p, ple_gain, w_gate, w_proj, layer):
    pp = hp_linear(w_proj, layer, D_MODEL, x=p, tk=PLE_DIM, tn=D_MODEL)[0]
    return hp_linear(w_gate, layer, D_MODEL, norm=(h, ple_gain), tk=D_MODEL, tn=512, res=h, pp=pp)[0]


def kernel(x_prompt, x_sample, cache_k_win, cache_v_win, state_short_conv, state_ssm, state_ssm_conv, p_prompt, p_sample, norm_mixer, norm_ffn, attn_w_qkv, attn_q_norm, attn_k_norm, attn_sinks, attn_w_out, sconv_w_in, sconv_conv_w, sconv_w_out, ssm_w_in, ssm_conv_w, ssm_conv_b, ssm_dt_bias, ssm_a_log, ssm_d_skip, ssm_norm, ssm_w_out, ffn_w_gate, ffn_w_up, ffn_w_down, moe_w_router, moe_w_gate, moe_w_up, moe_w_down, ple_norm, ple_w_gate, ple_w_proj):
    flat_e = lambda w: w.reshape((-1,) + w.shape[2:])
    moe_wg, moe_wu, moe_wd = flat_e(moe_w_gate), flat_e(moe_w_up), flat_e(moe_w_down)

    def dense_tiles(tm, widx):
        return jnp.full((T // tm,), widx, jnp.int32), jnp.full((1,), T // tm, jnp.int32)

    h = x_prompt.reshape(T, D_MODEL)
    hn = norm_rows(h, norm_mixer[0], BF16)
    k_p, v_p, sc_p, ssm_p, ssmc_p = [], [], [], [], []
    for i in range(DEPTH):
        kind, j = i % 3, i // 3
        if kind == 0:
            h, hn, new = attention_prompt(hn, h, attn_w_qkv, j, attn_q_norm[j], attn_k_norm[j], attn_sinks[j],
                                          attn_w_out, norm_ffn[i])
            k_p.append(new[0]); v_p.append(new[1])
        elif kind == 1:
            h, hn, new = sconv_prompt(hn, h, sconv_w_in, sconv_conv_w, j, sconv_w_out, norm_ffn[i])
            sc_p.append(new)
        else:
            h, hn, new = ssd_prompt(hn, h, ssm_w_in, j, ssm_conv_w[j], ssm_conv_b[j], ssm_dt_bias[j],
                                    ssm_a_log[j], ssm_d_skip[j], ssm_norm[j], ssm_w_out, norm_ffn[i])
            ssm_p.append(new[0]); ssmc_p.append(new[1])
        c = i // 2
        if i % 2 == 0:
            a = ffn_up(hn, ffn_w_gate, ffn_w_up, *dense_tiles(TM_BIG, c), tm=TM_BIG)
            h, hn = ffn_down(a, ffn_w_down, *dense_tiles(TM_BIG, c), tm=TM_BIG, res=h, gain=ple_norm[i])
        else:
            h, hn = moe_prompt(h, norm_ffn[i], moe_w_router[c], moe_w_gate, moe_w_up, moe_w_down, c, ple_norm[i])
        next_gain = norm_mixer[i + 1] if i + 1 < DEPTH else None
        h, hn = proj_residual(hn, ple_w_gate, i, h, next_gain, p=p_prompt[i].reshape(T, PLE_DIM), wp=ple_w_proj)
    y_p = h.reshape(BATCH, SEQ, D_MODEL)

    s = x_sample.reshape(NS, D_MODEL)
    k_s, v_s, sc_s, ssm_s, ssmc_s = [], [], [], [], []
    for i in range(DEPTH):
        kind, j = i % 3, i // 3
        if kind == 0:
            s, new = attention_sample(s, norm_mixer[i], cache_k_win[j], cache_v_win[j], attn_w_qkv, j,
                                      attn_q_norm[j], attn_k_norm[j], attn_sinks[j], attn_w_out)
            k_s.append(new[0]); v_s.append(new[1])
        elif kind == 1:
            s, new = sconv_sample(s, norm_mixer[i], state_short_conv[j], sconv_w_in, sconv_conv_w, j, sconv_w_out)
            sc_s.append(new)
        else:
            s, new = ssd_sample(s, norm_mixer[i], state_ssm_conv[j], state_ssm[j], ssm_w_in, j, ssm_conv_w[j],
                                ssm_conv_b[j], ssm_dt_bias[j], ssm_a_log[j], ssm_d_skip[j], ssm_norm[j], ssm_w_out)
            ssm_s.append(new[0]); ssmc_s.append(new[1])
        c = i // 2
        if i % 2 == 0:
            g, u = ffn_sample(s, norm_ffn[i], ffn_w_gate, ffn_w_up, ffn_w_down, c, 1)
            s = hp_linear(ffn_w_down, c, D_MODEL, swiglu=(g, u), tk=1024, tn=1024, res=s)[0]
        else:
            _, gates = route_top2(s, norm_ffn[i], moe_w_router[c], NS)
            g, u = ffn_sample(s, norm_ffn[i], moe_wg, moe_wu, moe_wd, c * N_EXPERTS, N_EXPERTS)
            d_e = hp_linear(moe_wd, c * N_EXPERTS, D_MODEL, swiglu=(g, u), n_e=N_EXPERTS, tk=1024, tn=1024)
            s = _small_call(_moe_mix_kernel, _sds(NS, D_MODEL), s, gates, d_e, name="moe_mix_sample")
        s = ple_sample(s, p_sample[i].reshape(NS, PLE_DIM), ple_norm[i], ple_w_gate, ple_w_proj, i)
    y_s = s.reshape(NS, 1, D_MODEL)

    return (y_p, y_s, jnp.stack(k_p), jnp.stack(v_p), jnp.stack(k_s), jnp.stack(v_s),
            jnp.stack(sc_p), jnp.stack(sc_s), jnp.stack(ssm_p), jnp.stack(ssm_s),
            jnp.stack(ssmc_p), jnp.stack(ssmc_s))
```

```python
import functools

import jax
import jax.numpy as jnp
from jax import lax
from jax.experimental import pallas as pl
from jax.experimental.pallas import tpu as pltpu

F32 = jnp.float32
BF16 = jnp.bfloat16

D_MODEL = 2048
BATCH = 2
SEQ = 4096
DEPTH = 4
DEC_BATCH = 32
PAST_LEN = 16384
ATTN_HEADS = 32
ATTN_KV_HEADS = 8
HEAD_DIM = 64
WINDOW = 128
SCONV_WIDTH = 3
D_INNER = 2 * D_MODEL
SSM_HEAD_DIM = 64
SSM_HEADS = D_INNER // SSM_HEAD_DIM
SSM_GROUPS = 8
D_STATE = 128
SSM_CONV_WIDTH = 4
SSM_CHUNK = 128
SSM_CONV_DIM = D_INNER + 2 * SSM_GROUPS * D_STATE
D_FF = 7 * D_MODEL // 2
N_EXPERTS = 8
PLE_DIM = 256
NORM_EPS = 1e-6
NQ = ATTN_HEADS * HEAD_DIM
NKV = ATTN_KV_HEADS * HEAD_DIM
assert PAST_LEN >= WINDOW

LANES = 128
SUBLANES = 8
T = BATCH * SEQ
NS = DEC_BATCH
TM_BIG = 1024
TM_MID = 512
TM_MOE = 1024
HALF_MOE = TM_MOE // 2
MOE_TILES = (2 * T + N_EXPERTS * (TM_MOE - 1)) // TM_MOE
R_PAD = MOE_TILES * TM_MOE
ROW_TILES = D_MODEL // LANES
VMEM_LIMIT = 56 * 1024 * 1024


def _params(*sem):
    return pltpu.CompilerParams(dimension_semantics=sem, vmem_limit_bytes=VMEM_LIMIT)


def _rms(x, g):
    return x * lax.rsqrt(jnp.mean(x * x, axis=-1, keepdims=True) + NORM_EPS) * g


def _dot(a, b):
    return jnp.dot(a, b, preferred_element_type=F32)


def _dot_nt(a, b):
    return lax.dot_general(a, b, (((1,), (1,)), ((), ())), preferred_element_type=F32)


def _dot_tn(a, b):
    return lax.dot_general(a, b, (((0,), (0,)), ((), ())), preferred_element_type=F32)


def _softplus(x):
    return jnp.maximum(x, 0.0) + jnp.log1p(jnp.exp(-jnp.abs(x)))


def _alibi_slopes():
    return jnp.exp2(-8.0 * jnp.arange(1, ATTN_HEADS + 1, dtype=F32) / ATTN_HEADS)


def _norm_kernel(x_ref, g_ref, o_ref, *, slabs):
    y = _rms(x_ref[...], g_ref[...]).astype(o_ref.dtype)
    if slabs:
        for s in range(ROW_TILES):
            o_ref[s] = y[:, s * LANES:(s + 1) * LANES]
    else:
        o_ref[...] = y


def norm_rows(x, gain, dtype, slabs=False, tm=TM_BIG):
    t, d = x.shape
    if slabs:
        out_spec = pl.BlockSpec((ROW_TILES, tm, LANES), lambda i: (0, i, 0))
        out_full = (ROW_TILES, t, LANES)
    else:
        out_spec = pl.BlockSpec((tm, d), lambda i: (i, 0))
        out_full = (t, d)
    return pl.pallas_call(
        functools.partial(_norm_kernel, slabs=slabs),
        grid=(t // tm,),
        in_specs=[pl.BlockSpec((tm, d), lambda i: (i, 0)),
                  pl.BlockSpec((1, d), lambda i: (0, 0))],
        out_specs=out_spec,
        out_shape=jax.ShapeDtypeStruct(out_full, dtype),
        compiler_params=_params("arbitrary"),
        name="norm_rows",
    )(x, gain.reshape(1, d))


def _mm_kernel(x_ref, w_ref, o_ref, wb_ref):
    @pl.when(pl.program_id(1) == 0)
    def _():
        wb_ref[...] = w_ref[...].astype(BF16)

    o_ref[...] = _dot(x_ref[...], wb_ref[...])


def matmul_cols(x, w, layer, n_out, col_off=0, tn=1024, tm=TM_BIG):
    t, k = x.shape
    tn = min(tn, n_out)
    off = col_off // tn
    assert col_off % tn == 0 and n_out % tn == 0
    return pl.pallas_call(
        _mm_kernel,
        grid=(n_out // tn, t // tm),
        in_specs=[pl.BlockSpec((tm, k), lambda n, m: (m, 0)),
                  pl.BlockSpec((None, k, tn), lambda n, m: (layer, 0, n + off))],
        out_specs=pl.BlockSpec((tm, tn), lambda n, m: (m, n)),
        out_shape=jax.ShapeDtypeStruct((t, n_out), F32),
        scratch_shapes=[pltpu.VMEM((k, tn), BF16)],
        compiler_params=_params("arbitrary", "arbitrary"),
        name="matmul_cols",
    )(x, w)


EPILOGUE_ROWS = 256


def _proj_kernel(*refs, tn, nn, gated, with_norm):
    x_ref, w_ref, res_ref = refs[:3]
    refs = refs[3:]
    if gated:
        p_ref, wp_ref = refs[:2]
        refs = refs[2:]
    if with_norm:
        g_ref, h_ref, hn_ref = refs
    else:
        (h_ref,) = refs
    n = pl.program_id(1)
    h_ref[:, pl.ds(pl.multiple_of(n * tn, tn), tn)] = _dot(x_ref[...], w_ref[...].astype(BF16))

    @pl.when(n == nn - 1)
    def _():
        for r in range(h_ref.shape[0] // EPILOGUE_ROWS):
            rows = slice(r * EPILOGUE_ROWS, (r + 1) * EPILOGUE_ROWS)
            upd = h_ref[rows, :]
            if gated:
                upd = jax.nn.sigmoid(upd) * _dot(p_ref[rows, :].astype(BF16), wp_ref[...].astype(BF16))
            h_new = res_ref[rows, :] + upd
            h_ref[rows, :] = h_new
            if with_norm:
                hn_ref[rows, :] = _rms(h_new, g_ref[...]).astype(BF16)


def proj_residual(x, w, layer, res, gain=None, p=None, wp=None, tm=TM_BIG, tn=512):
    t, kdim = x.shape
    d = w.shape[2]
    nn = d // tn
    gated = p is not None
    with_norm = gain is not None
    single = pl.Buffered(1)
    in_specs = [pl.BlockSpec((tm, kdim), lambda m, k: (m, 0)),
                pl.BlockSpec((None, kdim, tn), lambda m, k: (layer, 0, k)),
                pl.BlockSpec((tm, d), lambda m, k: (m, 0), pipeline_mode=single)]
    args = [x, w, res]
    if gated:
        in_specs += [pl.BlockSpec((tm, p.shape[1]), lambda m, k: (m, 0)),
                     pl.BlockSpec((None,) + wp.shape[1:], lambda m, k: (layer, 0, 0))]
        args += [p, wp]
    out_specs = [pl.BlockSpec((tm, d), lambda m, k: (m, 0), pipeline_mode=single)]
    out_shape = [jax.ShapeDtypeStruct((t, d), F32)]
    if with_norm:
        in_specs.append(pl.BlockSpec((1, d), lambda m, k: (0, 0)))
        args.append(gain.reshape(1, d))
        out_specs.append(pl.BlockSpec((tm, d), lambda m, k: (m, 0), pipeline_mode=single))
        out_shape.append(jax.ShapeDtypeStruct((t, d), BF16))
    out = pl.pallas_call(
        functools.partial(_proj_kernel, tn=tn, nn=nn, gated=gated, with_norm=with_norm),
        grid=(t // tm, nn),
        in_specs=in_specs,
        out_specs=out_specs,
        out_shape=out_shape,
        compiler_params=_params("arbitrary", "arbitrary"),
        name="proj_gated" if gated else "proj_residual",
    )(*args)
    return (out[0], out[1]) if with_norm else (out[0], None)


def _tile_row(m, n, te, na):
    return (jnp.minimum(m, na[0] - 1), 0)


def _ffn_up_kernel(te_ref, nl_ref, ts_ref, x_ref, wg_ref, wu_ref, a_ref, wgu_ref, *, tf):
    m = pl.program_id(1)
    nl = nl_ref[m]
    half = x_ref.shape[0] // 2

    def swiglu(x):
        gu = _dot(x, wgu_ref[...])
        return (jax.nn.silu(gu[:, :tf]) * gu[:, tf:]).astype(BF16)

    @pl.when(nl > 0)
    def _():
        @pl.when((m == 0) | (te_ref[m] != te_ref[jnp.maximum(m - 1, 0)]))
        def _():
            wgu_ref[:, :tf] = wg_ref[...].astype(BF16)
            wgu_ref[:, tf:] = wu_ref[...].astype(BF16)

        @pl.when(nl == 2)
        def _():
            a_ref[...] = swiglu(x_ref[...])

        @pl.when(nl == 1)
        def _():
            a_ref[:half] = swiglu(x_ref[:half])
            a_ref[half:] = jnp.zeros((half, tf), BF16)


def ffn_up(x, wg, wu, tile_expert, tile_live, tile_src, *, tm, tf=512):
    r, d = x.shape
    ff = wg.shape[2]
    w_spec = pl.BlockSpec((None, d, tf), lambda n, m, te, nl, ts: (te[m], 0, n))
    return pl.pallas_call(
        functools.partial(_ffn_up_kernel, tf=tf),
        grid_spec=pltpu.PrefetchScalarGridSpec(
            num_scalar_prefetch=3,
            grid=(ff // tf, r // tm),
            in_specs=[pl.BlockSpec((tm, d), lambda n, m, te, nl, ts: (ts[m], 0)), w_spec, w_spec],
            out_specs=pl.BlockSpec((tm, tf), lambda n, m, te, nl, ts: (ts[m], n)),
            scratch_shapes=[pltpu.VMEM((d, 2 * tf), BF16)]),
        out_shape=jax.ShapeDtypeStruct((r, ff), BF16),
        compiler_params=_params("arbitrary", "arbitrary"),
        name="ffn_up",
    )(tile_expert, tile_live, tile_src, x, wg, wu)


def _ffn_down_kernel(te_ref, na_ref, a_ref, wd_ref, res_ref, g_ref, out_ref, hn_ref, *, tn, nn):
    n = pl.program_id(1)
    out_ref[:, pl.ds(pl.multiple_of(n * tn, tn), tn)] = _dot(a_ref[...], wd_ref[...].astype(BF16))

    @pl.when(n == nn - 1)
    def _():
        for r in range(out_ref.shape[0] // EPILOGUE_ROWS):
            rows = slice(r * EPILOGUE_ROWS, (r + 1) * EPILOGUE_ROWS)
            h_new = res_ref[rows, :] + out_ref[rows, :]
            out_ref[rows, :] = h_new
            hn_ref[rows, :] = _rms(h_new, g_ref[...]).astype(BF16)


def ffn_down(a, wd, tile_expert, n_active, *, tm, res, gain, tn=256):
    r, ff = a.shape
    d = wd.shape[2]
    nn = d // tn
    single = pl.Buffered(1)
    return pl.pallas_call(
        functools.partial(_ffn_down_kernel, tn=tn, nn=nn),
        grid_spec=pltpu.PrefetchScalarGridSpec(
            num_scalar_prefetch=2,
            grid=(r // tm, nn),
            in_specs=[pl.BlockSpec((tm, ff), _tile_row, pipeline_mode=single),
                      pl.BlockSpec((None, ff, tn), lambda m, n, te, na: (te[m], 0, n)),
                      pl.BlockSpec((tm, d), _tile_row, pipeline_mode=single),
                      pl.BlockSpec((1, d), lambda m, n, te, na: (0, 0))],
            out_specs=[pl.BlockSpec((tm, d), _tile_row, pipeline_mode=single),
                       pl.BlockSpec((tm, d), _tile_row, pipeline_mode=single)]),
        out_shape=[jax.ShapeDtypeStruct((r, d), F32), jax.ShapeDtypeStruct((r, d), BF16)],
        compiler_params=_params("arbitrary", "arbitrary"),
        name="ffn_down_dense",
    )(tile_expert, n_active, a, wd, res, gain.reshape(1, d))


DOWN_ROWS = 256


def _down_block(m, hl, hs):
    per = HALF_MOE // DOWN_ROWS
    h = m // per
    return jnp.where(hl[h] > 0, m, hs[h] * per + per - 1)


def _ffn_down_routed_kernel(te_ref, hl_ref, hs_ref, a_ref, wd_ref, gate_ref, out_ref, wb_ref, *, tn):
    m = pl.program_id(1)
    per_tile = TM_MOE // DOWN_ROWS
    tile = m // per_tile

    @pl.when(hl_ref[m // (HALF_MOE // DOWN_ROWS)] > 0)
    def _():
        @pl.when((m == 0) | ((m % per_tile == 0) & (te_ref[tile] != te_ref[jnp.maximum(tile - 1, 0)])))
        def _():
            wb_ref[...] = wd_ref[...].astype(BF16)

        part = gate_ref[...] * _dot(a_ref[...], wb_ref[...])
        for j in range(tn // LANES):
            out_ref[j] = part[:, j * LANES:(j + 1) * LANES]


def ffn_down_routed(a, wd, tile_expert, half_live, half_src, row_gate, tn=512):
    r, ff = a.shape
    d = wd.shape[2]
    per_tile = TM_MOE // DOWN_ROWS
    rows = lambda n, m, te, hl, hs: (_down_block(m, hl, hs), 0)
    return pl.pallas_call(
        functools.partial(_ffn_down_routed_kernel, tn=tn),
        grid_spec=pltpu.PrefetchScalarGridSpec(
            num_scalar_prefetch=3,
            grid=(d // tn, r // DOWN_ROWS),
            in_specs=[pl.BlockSpec((DOWN_ROWS, ff), rows),
                      pl.BlockSpec((None, ff, tn), lambda n, m, te, hl, hs: (te[m // per_tile], 0, n)),
                      pl.BlockSpec((DOWN_ROWS, 1), rows)],
            out_specs=pl.BlockSpec((tn // LANES, DOWN_ROWS, LANES),
                                   lambda n, m, te, hl, hs: (n, _down_block(m, hl, hs), 0)),
            scratch_shapes=[pltpu.VMEM((ff, tn), BF16)]),
        out_shape=jax.ShapeDtypeStruct((ROW_TILES, r, LANES), F32),
        compiler_params=_params("arbitrary", "arbitrary"),
        name="ffn_down_routed",
    )(tile_expert, half_live, half_src, a, wd, row_gate)


def _router_kernel(h_ref, g_ref, w_ref, o_ref, dense_ref, *xn_ref):
    xn = _rms(h_ref[...], g_ref[...])
    for ref in xn_ref:
        for s in range(ROW_TILES):
            ref[s] = xn[:, s * LANES:(s + 1) * LANES]
    logits = jnp.dot(xn, w_ref[...], preferred_element_type=F32, precision=lax.Precision.HIGHEST)
    lane = lax.broadcasted_iota(jnp.int32, logits.shape, 1)
    logits = jnp.where(lane < N_EXPERTS, logits, -jnp.inf)
    e = jnp.exp(logits - jnp.max(logits, axis=-1, keepdims=True))
    probs = e / jnp.sum(e, axis=-1, keepdims=True)
    p1 = jnp.max(probs, axis=-1, keepdims=True)
    e1 = jnp.min(jnp.where(probs == p1, lane, LANES), axis=-1, keepdims=True)
    rest = jnp.where(lane == e1, -1.0, probs)
    p2 = jnp.max(rest, axis=-1, keepdims=True)
    e2 = jnp.min(jnp.where(rest == p2, lane, LANES), axis=-1, keepdims=True)
    tot = p1 + p2
    g1 = p1 / tot
    g2 = p2 / tot
    o_ref[...] = jnp.where(lane == 0, e1.astype(F32),
                           jnp.where(lane == 1, e2.astype(F32), jnp.where(lane == 2, g1, g2)))
    dense_ref[...] = jnp.where(lane == e1, g1, jnp.where(lane == e2, g2, 0.0))


def route_top2(h, gain, w_router, tm, slabs=False):
    t, d = h.shape
    wpad = jnp.pad(w_router, ((0, 0), (0, LANES - N_EXPERTS)))
    out_specs = [pl.BlockSpec((tm, LANES), lambda i: (i, 0)), pl.BlockSpec((tm, LANES), lambda i: (i, 0))]
    out_shape = [jax.ShapeDtypeStruct((t, LANES), F32), jax.ShapeDtypeStruct((t, LANES), F32)]
    if slabs:
        out_specs.append(pl.BlockSpec((ROW_TILES, tm, LANES), lambda i: (0, i, 0)))
        out_shape.append(jax.ShapeDtypeStruct((ROW_TILES, t, LANES), F32))
    return pl.pallas_call(
        _router_kernel,
        grid=(t // tm,),
        in_specs=[pl.BlockSpec((tm, d), lambda i: (i, 0)),
                  pl.BlockSpec((1, d), lambda i: (0, 0)),
                  pl.BlockSpec((d, LANES), lambda i: (0, 0))],
        out_specs=out_specs,
        out_shape=out_shape,
        compiler_params=_params("arbitrary"),
        name="route_top2",
    )(h, gain.reshape(1, d), wpad)


GATHER_TILE = HALF_MOE


def _start_row_copies(idx_ref, first, count, src_ref, dst_ref, sem):
    def issue(j, carry):
        t = idx_ref[first + j]
        pltpu.make_async_copy(src_ref.at[:, pl.ds(t, 1), :], dst_ref.at[:, pl.ds(j, 1), :], sem).start()
        return carry

    lax.fori_loop(0, count, issue, 0, unroll=8)


def _wait_row_copies(src_ref, dst_ref, sem):
    pltpu.make_async_copy(src_ref.at[:, pl.ds(0, dst_ref.shape[1]), :], dst_ref, sem).wait()


def _gather_kernel(idx_ref, hl_ref, hs_ref, src_ref, o_ref, buf_ref, sem):
    i = pl.program_id(0)

    @pl.when(hl_ref[i] > 0)
    def _():
        _start_row_copies(idx_ref, i * GATHER_TILE, GATHER_TILE, src_ref, buf_ref, sem)
        _wait_row_copies(src_ref, buf_ref, sem)
        for s in range(ROW_TILES):
            o_ref[:, s * LANES:(s + 1) * LANES] = buf_ref[s].astype(o_ref.dtype)


def gather_rows(src, idx, tile_live, tile_src):
    r = idx.shape[0]
    return pl.pallas_call(
        _gather_kernel,
        grid_spec=pltpu.PrefetchScalarGridSpec(
            num_scalar_prefetch=3,
            grid=(r // GATHER_TILE,),
            in_specs=[pl.BlockSpec(memory_space=pl.ANY)],
            out_specs=pl.BlockSpec((GATHER_TILE, D_MODEL), lambda i, idx_ref, hl, hs: (hs[i], 0)),
            scratch_shapes=[pltpu.VMEM((ROW_TILES, GATHER_TILE, LANES), F32), pltpu.SemaphoreType.DMA(())]),
        out_shape=jax.ShapeDtypeStruct((r, D_MODEL), BF16),
        compiler_params=_params("arbitrary"),
        name="gather_rows",
    )(idx, tile_live, tile_src, src)


def _combine_kernel(pos_ref, h_ref, o_ref, g_ref, h2_ref, hn_ref, o1_ref, o2_ref, sem1, sem2, *, tm, t):
    i = pl.program_id(0)
    _start_row_copies(pos_ref, i * tm, tm, o_ref, o1_ref, sem1)
    _start_row_copies(pos_ref, t + i * tm, tm, o_ref, o2_ref, sem2)
    _wait_row_copies(o_ref, o1_ref, sem1)
    _wait_row_copies(o_ref, o2_ref, sem2)
    for s in range(ROW_TILES):
        sl = slice(s * LANES, (s + 1) * LANES)
        h2_ref[:, sl] = h_ref[:, sl] + (o1_ref[s] + o2_ref[s])
    hn_ref[...] = _rms(h2_ref[...], g_ref[...]).astype(BF16)


def combine_rows(h, o, pos, gain, tm=TM_MID):
    t, d = h.shape
    tiled = pltpu.VMEM((ROW_TILES, tm, LANES), F32)
    return pl.pallas_call(
        functools.partial(_combine_kernel, tm=tm, t=t),
        grid_spec=pltpu.PrefetchScalarGridSpec(
            num_scalar_prefetch=1,
            grid=(t // tm,),
            in_specs=[pl.BlockSpec((tm, d), lambda i, pos_ref: (i, 0)),
                      pl.BlockSpec(memory_space=pl.ANY),
                      pl.BlockSpec((1, d), lambda i, pos_ref: (0, 0))],
            out_specs=[pl.BlockSpec((tm, d), lambda i, pos_ref: (i, 0)),
                       pl.BlockSpec((tm, d), lambda i, pos_ref: (i, 0))],
            scratch_shapes=[tiled, tiled, pltpu.SemaphoreType.DMA(()), pltpu.SemaphoreType.DMA(())]),
        out_shape=[jax.ShapeDtypeStruct((t, d), F32), jax.ShapeDtypeStruct((t, d), BF16)],
        compiler_params=_params("arbitrary"),
        name="combine_rows",
    )(pos, h, o, gain.reshape(1, d))


def moe_prompt(h, ffn_gain, w_router, wg, wu, wd, layer, ple_gain):
    r, _, xn = route_top2(h, ffn_gain, w_router, TM_BIG, slabs=True)
    experts = jnp.concatenate([r[:, 0], r[:, 1]]).astype(jnp.int32)
    gates = jnp.concatenate([r[:, 2], r[:, 3]])
    onehot = (experts[:, None] == jnp.arange(N_EXPERTS)[None, :]).astype(jnp.int32)
    rank = jnp.sum(jnp.cumsum(onehot, axis=0) * onehot, axis=1) - 1
    counts = jnp.sum(onehot, axis=0)
    halves = (counts + HALF_MOE - 1) // HALF_MOE
    tiles = (halves + 1) // 2
    tile_end = jnp.cumsum(tiles)
    tile_start = tile_end - tiles
    dest = jnp.sum(onehot * tile_start[None, :], axis=1) * TM_MOE + rank
    token = jnp.tile(jnp.arange(T, dtype=jnp.int32), 2)
    src_token = jnp.zeros((R_PAD,), jnp.int32).at[dest].set(token)
    row_gate = jnp.zeros((R_PAD,), F32).at[dest].set(gates).reshape(R_PAD, 1)
    n_active = tile_end[-1]
    tile_idx = jnp.arange(MOE_TILES, dtype=jnp.int32)
    tile_src = jnp.minimum(tile_idx, n_active - 1)
    of_expert = (tile_src[:, None] >= tile_end[None, :]).astype(jnp.int32)
    tile_expert = jnp.sum(of_expert, axis=1)
    mine = (tile_expert[:, None] == jnp.arange(N_EXPERTS)[None, :]).astype(jnp.int32)
    left = jnp.sum(mine * (halves[None, :] - 2 * (tile_src[:, None] - tile_start[None, :])), axis=1)
    tile_live = jnp.where(tile_idx < n_active, jnp.clip(left, 1, 2), 0).astype(jnp.int32)
    half_idx = jnp.arange(2 * MOE_TILES, dtype=jnp.int32)
    half_live = (half_idx % 2 < tile_live[half_idx // 2]).astype(jnp.int32)
    half_src = lax.cummax(jnp.where(half_live > 0, half_idx, 0))
    xs = gather_rows(xn, src_token, half_live, half_src)
    flat = lambda w: w.reshape((-1,) + w.shape[2:])
    tile_w = (tile_expert + layer * N_EXPERTS).astype(jnp.int32)
    a = ffn_up(xs, flat(wg), flat(wu), tile_w, tile_live, tile_src, tm=TM_MOE)
    o = ffn_down_routed(a, flat(wd), tile_w, half_live, half_src, row_gate)
    return combine_rows(h, o, dest, ple_gain)


def _headnorm_kernel(q_ref, k_ref, qg_ref, kg_ref, qn_ref, kn_ref):
    def norm_pairs(x_ref, g_ref, o_ref, scale):
        lo = lax.broadcasted_iota(jnp.int32, (x_ref.shape[0], LANES), 1) < HEAD_DIM
        for j in range(x_ref.shape[1] // LANES):
            x = x_ref[:, j * LANES:(j + 1) * LANES]
            s = x * x
            s_lo = jnp.sum(jnp.where(lo, s, 0.0), axis=-1, keepdims=True)
            s_hi = jnp.sum(jnp.where(lo, 0.0, s), axis=-1, keepdims=True)
            ms = jnp.where(lo, s_lo, s_hi) * (1.0 / HEAD_DIM)
            y = x * lax.rsqrt(ms + NORM_EPS) * g_ref[...]
            o_ref[:, j * LANES:(j + 1) * LANES] = (y * scale).astype(o_ref.dtype)

    norm_pairs(q_ref, qg_ref, qn_ref, HEAD_DIM ** -0.5)
    norm_pairs(k_ref, kg_ref, kn_ref, 1.0)


def head_norm(qkv, q_gain, k_gain, q_dtype, tm):
    t = qkv.shape[0]
    return pl.pallas_call(
        _headnorm_kernel,
        grid=(t // tm,),
        in_specs=[pl.BlockSpec((tm, NQ), lambda i: (i, 0)),
                  pl.BlockSpec((tm, NKV), lambda i: (i, NQ // NKV)),
                  pl.BlockSpec((1, LANES), lambda i: (0, 0)),
                  pl.BlockSpec((1, LANES), lambda i: (0, 0))],
        out_specs=[pl.BlockSpec((tm, NQ), lambda i: (i, 0)),
                   pl.BlockSpec((tm, NKV), lambda i: (i, 0))],
        out_shape=[jax.ShapeDtypeStruct((t, NQ), q_dtype), jax.ShapeDtypeStruct((t, NKV), F32)],
        compiler_params=_params("arbitrary"),
        name="head_norm",
    )(qkv, qkv, jnp.tile(q_gain, 2).reshape(1, LANES), jnp.tile(k_gain, 2).reshape(1, LANES))


def _attn_kernel(slope_ref, sink_ref, q_ref, kp_ref, kc_ref, vp_ref, vc_ref, o_ref):
    rows = WINDOW
    nkeys = 2 * WINDOW
    rep = ATTN_HEADS // ATTN_KV_HEADS
    qi = lax.broadcasted_iota(jnp.int32, (rows, nkeys), 0)
    kj = lax.broadcasted_iota(jnp.int32, (rows, nkeys), 1)
    delta = WINDOW + qi - kj
    mask = (delta >= 0) & (delta <= WINDOW) & ((kj >= WINDOW) | (pl.program_id(1) > 0))
    deltaf = delta.astype(F32)
    lo_q = lax.broadcasted_iota(jnp.int32, (rows, LANES), 1) < HEAD_DIM

    for jp in range(ATTN_KV_HEADS // 2):
        ksl = slice(jp * LANES, (jp + 1) * LANES)
        k2 = jnp.concatenate([kp_ref[:, ksl], kc_ref[:, ksl]], axis=0).astype(BF16)
        v2 = jnp.concatenate([vp_ref[:, ksl], vc_ref[:, ksl]], axis=0)
        v2r = pltpu.roll(v2, HEAD_DIM, 1).astype(BF16)
        v2 = v2.astype(BF16)
        for qp in range(rep):
            col = (jp * rep + qp) * LANES
            q2 = q_ref[:, col:col + LANES]
            q2r = pltpu.roll(q2, HEAD_DIM, 1)
            g_half = qp // (rep // 2)
            outs = []
            for hh in range(2):
                h = (jp * rep + qp) * 2 + hh
                qsel = q2 if hh == g_half else q2r
                qm = jnp.where(lo_q if g_half == 0 else ~lo_q, qsel, jnp.zeros_like(qsel))
                s = _dot_nt(qm, k2) - slope_ref[h] * deltaf
                s = jnp.where(mask, s, -jnp.inf)
                sink = sink_ref[h]
                m = jnp.maximum(jnp.max(s, axis=-1, keepdims=True), sink)
                e = jnp.exp(s - m)
                probs = e / (jnp.sum(e, axis=-1, keepdims=True) + jnp.exp(sink - m))
                outs.append(_dot(probs.astype(BF16), v2 if hh == g_half else v2r))
            o_ref[:, col:col + LANES] = jnp.where(lo_q, outs[0], outs[1]).astype(o_ref.dtype)


def attention_prompt(hn, h, w_qkv, layer, q_gain, k_gain, sinks, w_o, next_gain):
    qkv = matmul_cols(hn, w_qkv, layer, NQ + 2 * NKV)
    qn, kn = head_norm(qkv, q_gain, k_gain, BF16, TM_MID)
    nblk = SEQ // WINDOW
    vcol = NQ // NKV + 1
    cur = lambda c: (lambda b, n: (b * nblk + n, c))
    prev = lambda c: (lambda b, n: (jnp.maximum(b * nblk + n - 1, 0), c))
    smem = pl.BlockSpec(memory_space=pltpu.SMEM)
    o = pl.pallas_call(
        _attn_kernel,
        grid=(BATCH, nblk),
        in_specs=[smem, smem,
                  pl.BlockSpec((WINDOW, NQ), cur(0)),
                  pl.BlockSpec((WINDOW, NKV), prev(0)),
                  pl.BlockSpec((WINDOW, NKV), cur(0)),
                  pl.BlockSpec((WINDOW, NKV), prev(vcol)),
                  pl.BlockSpec((WINDOW, NKV), cur(vcol))],
        out_specs=pl.BlockSpec((WINDOW, NQ), cur(0)),
        out_shape=jax.ShapeDtypeStruct((T, NQ), BF16),
        compiler_params=_params("arbitrary", "arbitrary"),
        name="window_attention",
    )(_alibi_slopes(), sinks, qn, kn, kn, qkv, qkv)
    h2, hn2 = proj_residual(o, w_o, layer, h, next_gain)
    kv_shape = (BATCH, WINDOW, ATTN_KV_HEADS, HEAD_DIM)
    new_k = kn.reshape(BATCH, SEQ, NKV)[:, -WINDOW:].reshape(kv_shape)
    new_v = qkv[:, NQ + NKV:].reshape(BATCH, SEQ, NKV)[:, -WINDOW:].reshape(kv_shape)
    return h2, hn2, (new_k, new_v)


SCONV_TILE = 512
SCONV_TAIL = SUBLANES


def _sconv_kernel(gb_ref, gc_ref, u_ref, gcp_ref, up_ref, w_ref, y_ref, tail_ref):
    rows = gb_ref.shape[0]
    cu = gc_ref[...] * u_ref[...]
    prev = jnp.where(pl.program_id(1) > 0, gcp_ref[...] * up_ref[...], 0.0)
    row = lax.broadcasted_iota(jnp.int32, cu.shape, 0)
    s1 = jnp.where(row == 0, prev[SCONV_TAIL - 1:], pltpu.roll(cu, 1, 0))
    s2 = jnp.where(row == 0, prev[SCONV_TAIL - 2:SCONV_TAIL - 1],
                   jnp.where(row == 1, prev[SCONV_TAIL - 1:], pltpu.roll(cu, 2, 0)))
    conv = w_ref[0:1] * s2 + w_ref[1:2] * s1 + w_ref[2:3] * cu
    y_ref[...] = (gb_ref[...] * conv).astype(y_ref.dtype)
    tail_ref[...] = cu[rows - SCONV_TAIL:]


def sconv_prompt(hn, h, w_in, conv_w, layer, w_out, next_gain):
    d = D_MODEL
    proj = matmul_cols(hn, w_in, layer, 3 * d)
    nt = SEQ // SCONV_TILE
    tpb = SCONV_TILE // SCONV_TAIL
    cur = lambda c: (lambda b, i: (b * nt + i, c))
    prev = lambda c: (lambda b, i: (jnp.maximum((b * nt + i) * tpb - 1, 0), c))
    y, tails = pl.pallas_call(
        _sconv_kernel,
        grid=(BATCH, nt),
        in_specs=[pl.BlockSpec((SCONV_TILE, d), cur(0)),
                  pl.BlockSpec((SCONV_TILE, d), cur(1)),
                  pl.BlockSpec((SCONV_TILE, d), cur(2)),
                  pl.BlockSpec((SCONV_TAIL, d), prev(1)),
                  pl.BlockSpec((SCONV_TAIL, d), prev(2)),
                  pl.BlockSpec((None, SCONV_WIDTH, d), lambda b, i: (layer, 0, 0))],
        out_specs=[pl.BlockSpec((SCONV_TILE, d), cur(0)),
                   pl.BlockSpec((None, SCONV_TAIL, d), lambda b, i: (b * nt + i, 0, 0))],
        out_shape=[jax.ShapeDtypeStruct((T, d), BF16),
                   jax.ShapeDtypeStruct((BATCH * nt, SCONV_TAIL, d), F32)],
        compiler_params=_params("arbitrary", "arbitrary"),
        name="sconv_prompt",
    )(proj, proj, proj, proj, proj, conv_w)
    h2, hn2 = proj_residual(y, w_out, layer, h, next_gain)
    new_state = tails.reshape(BATCH, nt, SCONV_TAIL, d)[:, -1, -(SCONV_WIDTH - 1):]
    return h2, hn2, new_state


XBC_CHUNK = 512
GROUP_W = D_INNER // SSM_GROUPS
HEADS_PER_GROUP = SSM_HEADS // SSM_GROUPS


def _ssd_kernel(z_ref, xc_ref, xp_ref, dt_ref, cw_ref, cb_ref, dtb_ref, alog_ref, dsk_ref, nw_ref,
                y_ref, hfin_ref, state_ref, xact_ref, acst_ref, yg_ref):
    c = pl.program_id(1)
    q = SSM_CHUNK

    @pl.when(c == 0)
    def _():
        state_ref[...] = jnp.zeros_like(state_ref)

    has_prev = c > 0
    row = lax.broadcasted_iota(jnp.int32, (q, XBC_CHUNK), 0)
    for j in range(SSM_CONV_DIM // XBC_CHUNK):
        sl = slice(j * XBC_CHUNK, (j + 1) * XBC_CHUNK)
        cur = xc_ref[:, sl]
        prv = jnp.where(has_prev, xp_ref[:, sl], 0.0)
        acc = None
        for k in range(SSM_CONV_WIDTH):
            sh = SSM_CONV_WIDTH - 1 - k
            if sh == 0:
                term = cur
            else:
                term = jnp.where(row < sh, pltpu.roll(prv, sh, 0), pltpu.roll(cur, sh, 0))
            term = cw_ref[k:k + 1, sl] * term
            acc = term if acc is None else acc + term
        xact_ref[:, sl] = jax.nn.silu(acc + cb_ref[:, sl])

    dt = _softplus(dt_ref[...] + dtb_ref[...])
    da = dt * (-jnp.exp(alog_ref[...]))
    tri = (lax.broadcasted_iota(jnp.int32, (q, q), 0) >= lax.broadcasted_iota(jnp.int32, (q, q), 1))
    a_cs = jnp.dot(tri.astype(F32), da, preferred_element_type=F32, precision=lax.Precision.HIGHEST)
    acst_ref[...] = a_cs.T
    lane = lax.broadcasted_iota(jnp.int32, (q, LANES), 1)

    def group(g, carry):
        goff = pl.multiple_of(g * GROUP_W, GROUP_W)
        xs_g = xact_ref[:, pl.ds(goff, GROUP_W)]
        bm = xact_ref[:, pl.ds(pl.multiple_of(D_INNER + g * D_STATE, D_STATE), D_STATE)]
        cm = xact_ref[:, pl.ds(pl.multiple_of(D_INNER + SSM_GROUPS * D_STATE + g * D_STATE, D_STATE), D_STATE)]
        cm_b = cm.astype(BF16)
        bm_b = bm.astype(BF16)
        cb = _dot_nt(cm_b, bm_b)
        spread = (lax.broadcasted_iota(jnp.int32, (LANES, GROUP_W), 0) == g * HEADS_PER_GROUP
                  + lax.broadcasted_iota(jnp.int32, (LANES, GROUP_W), 1) // SSM_HEAD_DIM).astype(F32)
        acs_w = jnp.dot(a_cs, spread, preferred_element_type=F32, precision=lax.Precision.HIGHEST)
        dt_w = jnp.dot(dt, spread, preferred_element_type=F32, precision=lax.Precision.HIGHEST)
        acs_last_w = acs_w[q - 1:q]
        xdt = xs_g * dt_w
        xdt_b = xdt.astype(BF16)
        h_in = state_ref[g]
        h_in2 = h_in.reshape(GROUP_W, D_STATE)
        y_off = _dot_nt(cm_b, h_in2.astype(BF16)) * jnp.exp(acs_w)
        st = _dot_tn((xdt * jnp.exp(acs_last_w - acs_w)).astype(BF16), bm_b)
        chunk_decay = jnp.exp(acst_ref[pl.ds(pl.multiple_of(g * HEADS_PER_GROUP, HEADS_PER_GROUP),
                                             HEADS_PER_GROUP), :][:, q - 1:q])
        state_ref[g] = (chunk_decay.reshape(HEADS_PER_GROUP, 1, 1) * h_in
                        + st.reshape(HEADS_PER_GROUP, SSM_HEAD_DIM, D_STATE))
        for hh in range(HEADS_PER_GROUP):
            hsl = slice(hh * SSM_HEAD_DIM, (hh + 1) * SSM_HEAD_DIM)
            acs_col = acs_w[:, hh * SSM_HEAD_DIM:hh * SSM_HEAD_DIM + 1]
            acs_row = acst_ref[pl.ds(g * HEADS_PER_GROUP + hh, 1), :]
            seg = acs_col - acs_row
            decay_in = jnp.where(tri, jnp.exp(jnp.where(tri, seg, 0.0)), 0.0)
            yg_ref[:, hsl] = _dot((cb * decay_in).astype(BF16), xdt_b[:, hsl])
        y_g = yg_ref[...] + y_off + dsk_ref[:, pl.ds(goff, GROUP_W)] * xs_g
        gg = y_g * jax.nn.silu(z_ref[:, pl.ds(goff, GROUP_W)])
        gg = gg * lax.rsqrt(jnp.mean(gg * gg, axis=-1, keepdims=True) + NORM_EPS)
        y_ref[:, pl.ds(goff, GROUP_W)] = (gg * nw_ref[:, pl.ds(goff, GROUP_W)]).astype(y_ref.dtype)
        return carry

    lax.fori_loop(0, SSM_GROUPS, group, 0)

    @pl.when(c == pl.num_programs(1) - 1)
    def _():
        hfin_ref[...] = state_ref[...].reshape(hfin_ref.shape)


def _pad_heads(v):
    return jnp.pad(v, (0, LANES - SSM_HEADS)).reshape(1, LANES)


def ssd_prompt(hn, h, w_in, layer, conv_w, conv_b, dt_bias, a_log, d_skip, norm_w, w_out, next_gain):
    cd = SSM_CONV_DIM
    z = matmul_cols(hn, w_in, layer, D_INNER, col_off=0)
    xbc = matmul_cols(hn, w_in, layer, cd, col_off=D_INNER)
    w_dt = jnp.pad(w_in[layer, :, D_INNER + cd:], ((0, 0), (0, LANES - SSM_HEADS)))[None]
    dt_raw = matmul_cols(hn, w_dt, 0, LANES)
    dsk_full = jnp.repeat(d_skip, SSM_HEAD_DIM).reshape(1, D_INNER)
    nc = SEQ // SSM_CHUNK
    cur = lambda b, c: (b * nc + c, 0)
    whole = lambda b, c: (0, 0)
    y, h_fin = pl.pallas_call(
        _ssd_kernel,
        grid=(BATCH, nc),
        in_specs=[pl.BlockSpec((SSM_CHUNK, D_INNER), cur),
                  pl.BlockSpec((SSM_CHUNK, cd), cur),
                  pl.BlockSpec((SSM_CHUNK, cd), lambda b, c: (jnp.maximum(b * nc + c - 1, 0), 0)),
                  pl.BlockSpec((SSM_CHUNK, LANES), cur),
                  pl.BlockSpec((SSM_CONV_WIDTH, cd), whole),
                  pl.BlockSpec((1, cd), whole),
                  pl.BlockSpec((1, LANES), whole),
                  pl.BlockSpec((1, LANES), whole),
                  pl.BlockSpec((1, D_INNER), whole),
                  pl.BlockSpec((1, D_INNER), whole)],
        out_specs=[pl.BlockSpec((SSM_CHUNK, D_INNER), cur),
                   pl.BlockSpec((None, SSM_HEADS, SSM_HEAD_DIM, D_STATE), lambda b, c: (b, 0, 0, 0))],
        out_shape=[jax.ShapeDtypeStruct((T, D_INNER), BF16),
                   jax.ShapeDtypeStruct((BATCH, SSM_HEADS, SSM_HEAD_DIM, D_STATE), F32)],
        scratch_shapes=[pltpu.VMEM((SSM_GROUPS, HEADS_PER_GROUP, SSM_HEAD_DIM, D_STATE), F32),
                        pltpu.VMEM((SSM_CHUNK, cd), F32),
                        pltpu.VMEM((LANES, SSM_CHUNK), F32),
                        pltpu.VMEM((SSM_CHUNK, GROUP_W), F32)],
        compiler_params=_params("arbitrary", "arbitrary"),
        name="ssd_prompt",
    )(z, xbc, xbc, dt_raw, conv_w, conv_b.reshape(1, cd), _pad_heads(dt_bias), _pad_heads(a_log), dsk_full,
      norm_w.reshape(1, D_INNER))
    h2, hn2 = proj_residual(y, w_out, layer, h, next_gain, tm=TM_MID)
    new_conv = xbc.reshape(BATCH, SEQ, cd)[:, -(SSM_CONV_WIDTH - 1):]
    return h2, hn2, (h_fin, new_conv)


def _split(x):
    hi = x.astype(BF16)
    lo = (x - hi.astype(F32)).astype(BF16)
    return hi, lo


HP_K_CHUNK = 512


def _hp_kernel(*refs, nk, x_mode, epi):
    refs = list(refs)
    if x_mode == "norm":
        x_ref, g_ref = refs[:2]
        refs = refs[2:]
        x = _rms(x_ref[...], g_ref[...])
    elif x_mode == "swiglu":
        g_ref, u_ref = refs[:2]
        refs = refs[2:]
        x = jax.nn.silu(g_ref[...]) * u_ref[...]
    else:
        x = refs[0][...]
        refs = refs[1:]
    w_ref = refs[0]
    refs = refs[1:]
    if epi == "gated":
        res_ref, pp_ref, o_ref = refs
    elif epi == "res":
        res_ref, o_ref = refs
    else:
        (o_ref,) = refs
    k = pl.program_id(2)
    rows = x.shape[0]
    x_hi, x_lo = _split(x)
    x_both = jnp.concatenate([x_hi, x_lo], axis=0)
    part = None
    chunk = min(HP_K_CHUNK, w_ref.shape[0])
    for c in range(w_ref.shape[0] // chunk):
        ksl = slice(c * chunk, (c + 1) * chunk)
        w_hi, w_lo = _split(w_ref[ksl, :])
        both = _dot(x_both[:, ksl], w_hi)
        term = both[:rows] + both[rows:] + _dot(x_hi[:, ksl], w_lo)
        part = term if part is None else part + term

    @pl.when(k == 0)
    def _():
        o_ref[...] = part

    @pl.when(k > 0)
    def _():
        o_ref[...] += part

    if epi is not None:
        @pl.when(k == nk - 1)
        def _():
            acc = o_ref[...]
            if epi == "gated":
                acc = jax.nn.sigmoid(acc) * pp_ref[...]
            o_ref[...] = res_ref[...] + acc


def hp_linear(w, widx, n_out, *, x=None, norm=None, swiglu=None, col_off=0, n_e=1, tk, tn, res=None, pp=None):
    kdim = w.shape[1]
    nk = kdim // tk
    off = col_off // tn
    assert kdim % tk == 0 and n_out % tn == 0 and col_off % tn == 0
    if norm is not None:
        assert nk == 1
        x_mode = "norm"
        args = [norm[0], norm[1].reshape(1, kdim)]
        in_specs = [pl.BlockSpec((NS, tk), lambda e, n, k: (0, k)),
                    pl.BlockSpec((1, tk), lambda e, n, k: (0, k))]
    elif swiglu is not None:
        x_mode = "swiglu"
        args = list(swiglu)
        in_specs = [pl.BlockSpec((None, NS, tk), lambda e, n, k: (e, 0, k))] * 2
    else:
        x_mode = "plain"
        args = [x]
        in_specs = [pl.BlockSpec((NS, tk), lambda e, n, k: (0, k))]
    args.append(w)
    in_specs.append(pl.BlockSpec((None, tk, tn), lambda e, n, k: (widx + e, k, n + off)))
    epi = None
    if res is not None:
        epi = "res"
        args.append(res)
        in_specs.append(pl.BlockSpec((NS, tn), lambda e, n, k: (0, n)))
        if pp is not None:
            epi = "gated"
            args.append(pp)
            in_specs.append(pl.BlockSpec((NS, tn), lambda e, n, k: (0, n)))
    return pl.pallas_call(
        functools.partial(_hp_kernel, nk=nk, x_mode=x_mode, epi=epi),
        grid=(n_e, n_out // tn, nk),
        in_specs=in_specs,
        out_specs=pl.BlockSpec((None, NS, tn), lambda e, n, k: (e, 0, n)),
        out_shape=jax.ShapeDtypeStruct((n_e, NS, n_out), F32),
        compiler_params=_params("arbitrary", "arbitrary", "arbitrary"),
        name="hp_linear_" + x_mode,
    )(*args)


def _attn_step_kernel(slope_ref, sink_ref, q_ref, kc_ref, vc_ref, kn_ref, vn_ref, o_ref):
    rep = ATTN_HEADS // ATTN_KV_HEADS
    lo = lax.broadcasted_iota(jnp.int32, (1, LANES), 1) < HEAD_DIM
    delta = (WINDOW - lax.broadcasted_iota(jnp.int32, (WINDOW, 1), 0)).astype(F32)
    for jp in range(ATTN_KV_HEADS // 2):
        ksl = slice(jp * LANES, (jp + 1) * LANES)
        k2 = kc_ref[:, ksl]
        v2 = vc_ref[:, ksl]
        kn2 = kn_ref[:, ksl]
        vn2 = vn_ref[:, ksl]
        for qp in range(rep):
            col = (jp * rep + qp) * LANES
            q2 = q_ref[:, col:col + LANES]
            q2r = pltpu.roll(q2, HEAD_DIM, 1)
            g_half = qp // (rep // 2)
            kv_lanes = lo if g_half == 0 else ~lo
            out = jnp.zeros((1, LANES), F32)
            for hh in range(2):
                h = (jp * rep + qp) * 2 + hh
                qm = jnp.where(kv_lanes, q2 if hh == g_half else q2r, 0.0)
                s = jnp.sum(k2 * qm, axis=1, keepdims=True) - slope_ref[h] * delta
                s_new = jnp.sum(kn2 * qm, axis=1, keepdims=True)
                sink = sink_ref[h]
                m = jnp.maximum(jnp.maximum(jnp.max(s, axis=0, keepdims=True), s_new), sink)
                e = jnp.exp(s - m)
                e_new = jnp.exp(s_new - m)
                denom = jnp.sum(e, axis=0, keepdims=True) + e_new + jnp.exp(sink - m)
                o = jnp.sum((e / denom) * v2, axis=0, keepdims=True) + (e_new / denom) * vn2
                if hh != g_half:
                    o = pltpu.roll(o, HEAD_DIM, 1)
                out = jnp.where(lo if hh == 0 else ~lo, o, out)
            o_ref[:, col:col + LANES] = out


def _sconv_step_kernel(proj_ref, p_ref, w_ref, y_ref, cu_ref):
    d = D_MODEL
    cu = proj_ref[:, d:2 * d] * proj_ref[:, 2 * d:]
    conv = w_ref[0:1] * p_ref[0] + w_ref[1:2] * p_ref[1] + w_ref[2:3] * cu
    y_ref[...] = proj_ref[:, :d] * conv
    cu_ref[...] = cu


def _ssd_step_pre_kernel(xc_ref, p_ref, dt_ref, cw_ref, cb_ref, dtb_ref, alog_ref, xact_ref, dt_out_ref, dec_ref):
    acc = cw_ref[0:1] * p_ref[0] + cw_ref[1:2] * p_ref[1] + cw_ref[2:3] * p_ref[2] + cw_ref[3:4] * xc_ref[...]
    xact_ref[...] = jax.nn.silu(acc + cb_ref[...])
    dt = _softplus(dt_ref[...] + dtb_ref[...])
    dt_out_ref[...] = dt
    dec_ref[...] = jnp.exp(dt * (-jnp.exp(alog_ref[...])))


def _ssd_step_kernel(dt_ref, dec_ref, dsk_ref, xt_ref, bm_ref, cm_ref, h0_ref, yt_ref, h1_ref):
    b = pl.program_id(0)
    xt = xt_ref[...]
    lane = lax.broadcasted_iota(jnp.int32, xt.shape, 1)

    def head(h, yt):
        g = h // HEADS_PER_GROUP
        sel = lane == h
        x_col = jnp.sum(jnp.where(sel, xt, 0.0), axis=1, keepdims=True)
        bm = bm_ref[pl.ds(g, 1), :]
        cm = cm_ref[pl.ds(g, 1), :]
        h1 = dec_ref[b, h] * h0_ref[h] + (dt_ref[b, h] * x_col) * bm
        h1_ref[h] = h1
        y_col = jnp.sum(h1 * cm, axis=1, keepdims=True) + dsk_ref[h] * x_col
        return jnp.where(sel, y_col, yt)

    yt_ref[...] = lax.fori_loop(0, SSM_HEADS, head, jnp.zeros(xt.shape, F32), unroll=4)


def _ssd_gate_kernel(y_ref, z_ref, nw_ref, o_ref):
    for g in range(SSM_GROUPS):
        sl = slice(g * GROUP_W, (g + 1) * GROUP_W)
        gg = y_ref[:, sl] * jax.nn.silu(z_ref[:, sl])
        gg = gg * lax.rsqrt(jnp.mean(gg * gg, axis=-1, keepdims=True) + NORM_EPS)
        o_ref[:, sl] = gg * nw_ref[:, sl]


def _moe_mix_kernel(h_ref, gates_ref, d_ref, o_ref):
    lane = lax.broadcasted_iota(jnp.int32, gates_ref.shape, 1)
    y = jnp.zeros(h_ref.shape, F32)
    for e in range(N_EXPERTS):
        gate = jnp.sum(jnp.where(lane == e, gates_ref[...], 0.0), axis=1, keepdims=True)
        y = y + gate * d_ref[e]
    o_ref[...] = h_ref[...] + y


def _whole(shape):
    return pl.BlockSpec(shape, lambda *_: (0,) * len(shape))


def _small_call(body, out_shapes, *args, name):
    multi = isinstance(out_shapes, (list, tuple))
    outs = list(out_shapes) if multi else [out_shapes]
    res = pl.pallas_call(
        body,
        grid=(1,),
        in_specs=[_whole(a.shape) for a in args],
        out_specs=[_whole(o.shape) for o in outs],
        out_shape=outs,
        compiler_params=_params("arbitrary"),
        name=name,
    )(*args)
    return res if multi else res[0]


def _sds(*shape):
    return jax.ShapeDtypeStruct(shape, F32)


def attention_sample(h, norm_gain, cache_k, cache_v, w_qkv, layer, q_gain, k_gain, sinks, w_o):
    qkv = hp_linear(w_qkv, layer, NQ + 2 * NKV, norm=(h, norm_gain), tk=D_MODEL, tn=1024)[0]
    qn, kn = head_norm(qkv, q_gain, k_gain, F32, NS)
    v_new = qkv[:, NQ + NKV:]
    smem = pl.BlockSpec(memory_space=pltpu.SMEM)
    rowsq = pl.BlockSpec((None, 1, NQ), lambda b: (b, 0, 0))
    rowsk = pl.BlockSpec((None, 1, NKV), lambda b: (b, 0, 0))
    cache = pl.BlockSpec((None, WINDOW, NKV), lambda b: (b, 0, 0))
    o = pl.pallas_call(
        _attn_step_kernel,
        grid=(NS,),
        in_specs=[smem, smem, rowsq, cache, cache, rowsk, rowsk],
        out_specs=rowsq,
        out_shape=_sds(NS, 1, NQ),
        compiler_params=_params("arbitrary"),
        name="attention_sample",
    )(_alibi_slopes(), sinks, qn.reshape(NS, 1, NQ), cache_k.reshape(NS, WINDOW, NKV),
      cache_v.reshape(NS, WINDOW, NKV), kn.reshape(NS, 1, NKV), v_new.reshape(NS, 1, NKV))
    h2 = hp_linear(w_o, layer, D_MODEL, x=o.reshape(NS, NQ), tk=1024, tn=1024, res=h)[0]
    kv_row = (NS, 1, ATTN_KV_HEADS, HEAD_DIM)
    new_k = jnp.concatenate([cache_k[:, 1:], kn.reshape(kv_row)], axis=1)
    new_v = jnp.concatenate([cache_v[:, 1:], v_new.reshape(kv_row)], axis=1)
    return h2, (new_k, new_v)


def sconv_sample(h, norm_gain, past, w_in, conv_w, layer, w_out):
    d = D_MODEL
    proj = hp_linear(w_in, layer, 3 * d, norm=(h, norm_gain), tk=d, tn=512)[0]
    y, cu = _small_call(_sconv_step_kernel, [_sds(NS, d), _sds(NS, d)],
                        proj, past.transpose(1, 0, 2), conv_w[layer], name="sconv_sample")
    h2 = hp_linear(w_out, layer, d, x=y, tk=1024, tn=1024, res=h)[0]
    return h2, jnp.concatenate([past[:, 1:], cu[:, None]], axis=1)


def ssd_sample(h, norm_gain, conv_past, h0, w_in, layer, conv_w, conv_b, dt_bias, a_log, d_skip, norm_w, w_out):
    cd = SSM_CONV_DIM
    z = hp_linear(w_in, layer, D_INNER, norm=(h, norm_gain), tk=D_MODEL, tn=1024)[0]
    xbc = hp_linear(w_in, layer, cd, norm=(h, norm_gain), col_off=D_INNER, tk=D_MODEL, tn=1024)[0]
    w_dt = jnp.pad(w_in[layer, :, D_INNER + cd:], ((0, 0), (0, LANES - SSM_HEADS)))[None]
    dt_raw = hp_linear(w_dt, 0, LANES, norm=(h, norm_gain), tk=D_MODEL, tn=LANES)[0]
    xact, dt, dec = _small_call(
        _ssd_step_pre_kernel, [_sds(NS, cd), _sds(NS, LANES), _sds(NS, LANES)],
        xbc, conv_past.transpose(1, 0, 2), dt_raw, conv_w, conv_b.reshape(1, cd), _pad_heads(dt_bias),
        _pad_heads(a_log), name="ssd_sample_pre")
    xt = xact[:, :D_INNER].reshape(NS, SSM_HEADS, SSM_HEAD_DIM).transpose(0, 2, 1)
    bm = xact[:, D_INNER:D_INNER + SSM_GROUPS * D_STATE].reshape(NS, SSM_GROUPS, D_STATE)
    cm = xact[:, D_INNER + SSM_GROUPS * D_STATE:].reshape(NS, SSM_GROUPS, D_STATE)
    smem = pl.BlockSpec(memory_space=pltpu.SMEM)
    per_b = lambda *shape: pl.BlockSpec((None,) + shape, lambda b: (b,) + (0,) * len(shape))
    yt, h1 = pl.pallas_call(
        _ssd_step_kernel,
        grid=(NS,),
        in_specs=[smem, smem, smem,
                  per_b(SSM_HEAD_DIM, SSM_HEADS), per_b(SSM_GROUPS, D_STATE), per_b(SSM_GROUPS, D_STATE),
                  per_b(SSM_HEADS, SSM_HEAD_DIM, D_STATE)],
        out_specs=[per_b(SSM_HEAD_DIM, SSM_HEADS), per_b(SSM_HEADS, SSM_HEAD_DIM, D_STATE)],
        out_shape=[_sds(NS, SSM_HEAD_DIM, SSM_HEADS), _sds(NS, SSM_HEADS, SSM_HEAD_DIM, D_STATE)],
        compiler_params=_params("arbitrary"),
        name="ssd_sample_step",
    )(dt[:, :SSM_HEADS], dec[:, :SSM_HEADS], d_skip, xt, bm, cm, h0)
    y = yt.transpose(0, 2, 1).reshape(NS, D_INNER)
    yn = _small_call(_ssd_gate_kernel, _sds(NS, D_INNER), y, z, norm_w.reshape(1, D_INNER), name="ssd_sample_gate")
    h2 = hp_linear(w_out, layer, D_MODEL, x=yn, tk=1024, tn=1024, res=h)[0]
    return h2, (h1, jnp.concatenate([conv_past[:, 1:], xbc[:, None]], axis=1))


def ffn_sample(h, norm_gain, wg, wu, wd, widx, n_e):
    g = hp_linear(wg, widx, D_FF, norm=(h, norm_gain), n_e=n_e, tk=D_MODEL, tn=1024)
    u = hp_linear(wu, widx, D_FF, norm=(h, norm_gain), n_e=n_e, tk=D_MODEL, tn=1024)
    return g, u


def ple_sample(h, p, ple_gain, w_gate, w_proj, layer):
    pp = hp_linear(w_proj, layer, D_MODEL, x=p, tk=PLE_DIM, tn=D_MODEL)[0]
    return hp_linear(w_gate, layer, D_MODEL, norm=(h, ple_gain), tk=D_MODEL, tn=512, res=h, pp=pp)[0]


def kernel(x_prompt, x_sample, cache_k_win, cache_v_win, state_short_conv, state_ssm, state_ssm_conv, p_prompt, p_sample, norm_mixer, norm_ffn, attn_w_qkv, attn_q_norm, attn_k_norm, attn_sinks, attn_w_out, sconv_w_in, sconv_conv_w, sconv_w_out, ssm_w_in, ssm_conv_w, ssm_conv_b, ssm_dt_bias, ssm_a_log, ssm_d_skip, ssm_norm, ssm_w_out, ffn_w_gate, ffn_w_up, ffn_w_down, moe_w_router, moe_w_gate, moe_w_up, moe_w_down, ple_norm, ple_w_gate, ple_w_proj):
    flat_e = lambda w: w.reshape((-1,) + w.shape[2:])
    moe_wg, moe_wu, moe_wd = flat_e(moe_w_gate), flat_e(moe_w_up), flat_e(moe_w_down)

    def dense_tiles(tm, widx):
        return jnp.full((T // tm,), widx, jnp.int32), jnp.full((1,), T // tm, jnp.int32)

    h = x_prompt.reshape(T, D_MODEL)
    hn = norm_rows(h, norm_mixer[0], BF16)
    k_p, v_p, sc_p, ssm_p, ssmc_p = [], [], [], [], []
    for i in range(DEPTH):
        kind, j = i % 3, i // 3
        if kind == 0:
            h, hn, new = attention_prompt(hn, h, attn_w_qkv, j, attn_q_norm[j], attn_k_norm[j], attn_sinks[j],
                                          attn_w_out, norm_ffn[i])
            k_p.append(new[0]); v_p.append(new[1])
        elif kind == 1:
            h, hn, new = sconv_prompt(hn, h, sconv_w_in, sconv_conv_w, j, sconv_w_out, norm_ffn[i])
            sc_p.append(new)
        else:
            h, hn, new = ssd_prompt(hn, h, ssm_w_in, j, ssm_conv_w[j], ssm_conv_b[j], ssm_dt_bias[j],
                                    ssm_a_log[j], ssm_d_skip[j], ssm_norm[j], ssm_w_out, norm_ffn[i])
            ssm_p.append(new[0]); ssmc_p.append(new[1])
        c = i // 2
        if i % 2 == 0:
            n_tiles = T // TM_BIG
            a = ffn_up(hn, ffn_w_gate, ffn_w_up, jnp.full((n_tiles,), c, jnp.int32),
                       jnp.full((n_tiles,), 2, jnp.int32), jnp.arange(n_tiles, dtype=jnp.int32), tm=TM_BIG)
            h, hn = ffn_down(a, ffn_w_down, *dense_tiles(TM_BIG, c), tm=TM_BIG, res=h, gain=ple_norm[i])
        else:
            h, hn = moe_prompt(h, norm_ffn[i], moe_w_router[c], moe_w_gate, moe_w_up, moe_w_down, c, ple_norm[i])
        next_gain = norm_mixer[i + 1] if i + 1 < DEPTH else None
        h, hn = proj_residual(hn, ple_w_gate, i, h, next_gain, p=p_prompt[i].reshape(T, PLE_DIM), wp=ple_w_proj)
    y_p = h.reshape(BATCH, SEQ, D_MODEL)

    s = x_sample.reshape(NS, D_MODEL)
    k_s, v_s, sc_s, ssm_s, ssmc_s = [], [], [], [], []
    for i in range(DEPTH):
        kind, j = i % 3, i // 3
        if kind == 0:
            s, new = attention_sample(s, norm_mixer[i], cache_k_win[j], cache_v_win[j], attn_w_qkv, j,
                                      attn_q_norm[j], attn_k_norm[j], attn_sinks[j], attn_w_out)
            k_s.append(new[0]); v_s.append(new[1])
        elif kind == 1:
            s, new = sconv_sample(s, norm_mixer[i], state_short_conv[j], sconv_w_in, sconv_conv_w, j, sconv_w_out)
            sc_s.append(new)
        else:
            s, new = ssd_sample(s, norm_mixer[i], state_ssm_conv[j], state_ssm[j], ssm_w_in, j, ssm_conv_w[j],
                                ssm_conv_b[j], ssm_dt_bias[j], ssm_a_log[j], ssm_d_skip[j], ssm_norm[j], ssm_w_out)
            ssm_s.append(new[0]); ssmc_s.append(new[1])
        c = i // 2
        if i % 2 == 0:
            g, u = ffn_sample(s, norm_ffn[i], ffn_w_gate, ffn_w_up, ffn_w_down, c, 1)
            s = hp_linear(ffn_w_down, c, D_MODEL, swiglu=(g, u), tk=1024, tn=1024, res=s)[0]
        else:
            _, gates = route_top2(s, norm_ffn[i], moe_w_router[c], NS)
            g, u = ffn_sample(s, norm_ffn[i], moe_wg, moe_wu, moe_wd, c * N_EXPERTS, N_EXPERTS)
            d_e = hp_linear(moe_wd, c * N_EXPERTS, D_MODEL, swiglu=(g, u), n_e=N_EXPERTS, tk=1024, tn=1024)
            s = _small_call(_moe_mix_kernel, _sds(NS, D_MODEL), s, gates, d_e, name="moe_mix_sample")
        s = ple_sample(s, p_sample[i].reshape(NS, PLE_DIM), ple_norm[i], ple_w_gate, ple_w_proj, i)
    y_s = s.reshape(NS, 1, D_MODEL)

    return (y_p, y_s, jnp.stack(k_p), jnp.stack(v_p), jnp.stack(k_s), jnp.stack(v_s),
            jnp.stack(sc_p), jnp.stack(sc_s), jnp.stack(ssm_p), jnp.stack(ssm_s),
            jnp.stack(ssmc_p), jnp.stack(ssmc_s))
```

```python
import functools

import jax
import jax.numpy as jnp
from jax import lax
from jax.experimental import pallas as pl
from jax.experimental.pallas import tpu as pltpu

F32 = jnp.float32
BF16 = jnp.bfloat16

D_MODEL = 2048
BATCH = 2
SEQ = 4096
DEPTH = 4
DEC_BATCH = 32
PAST_LEN = 16384
ATTN_HEADS = 32
ATTN_KV_HEADS = 8
HEAD_DIM = 64
WINDOW = 128
SCONV_WIDTH = 3
D_INNER = 2 * D_MODEL
SSM_HEAD_DIM = 64
SSM_HEADS = D_INNER // SSM_HEAD_DIM
SSM_GROUPS = 8
D_STATE = 128
SSM_CONV_WIDTH = 4
SSM_CHUNK = 128
SSM_CONV_DIM = D_INNER + 2 * SSM_GROUPS * D_STATE
D_FF = 7 * D_MODEL // 2
N_EXPERTS = 8
PLE_DIM = 256
NORM_EPS = 1e-6
NQ = ATTN_HEADS * HEAD_DIM
NKV = ATTN_KV_HEADS * HEAD_DIM
assert PAST_LEN >= WINDOW

LANES = 128
SUBLANES = 8
T = BATCH * SEQ
NS = DEC_BATCH
TM_BIG = 1024
TM_MID = 512
TM_MOE = 1024
HALF_MOE = TM_MOE // 2
MOE_TILES = (2 * T + N_EXPERTS * (TM_MOE - 1)) // TM_MOE
R_PAD = MOE_TILES * TM_MOE
ROW_TILES = D_MODEL // LANES
VMEM_LIMIT = 60 * 1024 * 1024


def _params(*sem):
    return pltpu.CompilerParams(dimension_semantics=sem, vmem_limit_bytes=VMEM_LIMIT)


def _rms(x, g):
    return x * lax.rsqrt(jnp.mean(x * x, axis=-1, keepdims=True) + NORM_EPS) * g


def _dot(a, b):
    return jnp.dot(a, b, preferred_element_type=F32)


def _dot_nt(a, b):
    return lax.dot_general(a, b, (((1,), (1,)), ((), ())), preferred_element_type=F32)


def _dot_tn(a, b):
    return lax.dot_general(a, b, (((0,), (0,)), ((), ())), preferred_element_type=F32)


def _softplus(x):
    return jnp.maximum(x, 0.0) + jnp.log1p(jnp.exp(-jnp.abs(x)))


def _alibi_slopes():
    return jnp.exp2(-8.0 * jnp.arange(1, ATTN_HEADS + 1, dtype=F32) / ATTN_HEADS)


def _norm_kernel(x_ref, g_ref, o_ref):
    o_ref[...] = _rms(x_ref[...], g_ref[...]).astype(o_ref.dtype)


def norm_rows(x, gain, dtype, tm=TM_BIG):
    t, d = x.shape
    return pl.pallas_call(
        _norm_kernel,
        grid=(t // tm,),
        in_specs=[pl.BlockSpec((tm, d), lambda i: (i, 0)),
                  pl.BlockSpec((1, d), lambda i: (0, 0))],
        out_specs=pl.BlockSpec((tm, d), lambda i: (i, 0)),
        out_shape=jax.ShapeDtypeStruct((t, d), dtype),
        compiler_params=_params("arbitrary"),
        name="norm_rows",
    )(x, gain.reshape(1, d))


def _mm_kernel(x_ref, w_ref, o_ref, wb_ref):
    @pl.when(pl.program_id(1) == 0)
    def _():
        wb_ref[...] = w_ref[...].astype(BF16)

    o_ref[...] = _dot(x_ref[...], wb_ref[...])


def matmul_cols(x, w, layer, n_out, col_off=0, tn=1024, tm=TM_BIG):
    t, k = x.shape
    tn = min(tn, n_out)
    off = col_off // tn
    assert col_off % tn == 0 and n_out % tn == 0
    return pl.pallas_call(
        _mm_kernel,
        grid=(n_out // tn, t // tm),
        in_specs=[pl.BlockSpec((tm, k), lambda n, m: (m, 0)),
                  pl.BlockSpec((None, k, tn), lambda n, m: (layer, 0, n + off))],
        out_specs=pl.BlockSpec((tm, tn), lambda n, m: (m, n)),
        out_shape=jax.ShapeDtypeStruct((t, n_out), F32),
        scratch_shapes=[pltpu.VMEM((k, tn), BF16)],
        compiler_params=_params("arbitrary", "arbitrary"),
        name="matmul_cols",
    )(x, w)


EPILOGUE_ROWS = 256


def _proj_kernel(*refs, tn, nn, gated, with_norm):
    x_ref, w_ref, res_ref = refs[:3]
    refs = refs[3:]
    if gated:
        p_ref, wp_ref = refs[:2]
        refs = refs[2:]
    if with_norm:
        g_ref, h_ref, hn_ref = refs
    else:
        (h_ref,) = refs
    n = pl.program_id(1)
    h_ref[:, pl.ds(pl.multiple_of(n * tn, tn), tn)] = _dot(x_ref[...], w_ref[...].astype(BF16))

    @pl.when(n == nn - 1)
    def _():
        for r in range(h_ref.shape[0] // EPILOGUE_ROWS):
            rows = slice(r * EPILOGUE_ROWS, (r + 1) * EPILOGUE_ROWS)
            upd = h_ref[rows, :]
            if gated:
                upd = jax.nn.sigmoid(upd) * _dot(p_ref[rows, :].astype(BF16), wp_ref[...].astype(BF16))
            h_new = res_ref[rows, :] + upd
            h_ref[rows, :] = h_new
            if with_norm:
                hn_ref[rows, :] = _rms(h_new, g_ref[...]).astype(BF16)


def proj_residual(x, w, layer, res, gain=None, p=None, wp=None, tm=TM_BIG, tn=512):
    t, kdim = x.shape
    d = w.shape[2]
    nn = d // tn
    gated = p is not None
    with_norm = gain is not None
    single = pl.Buffered(1)
    in_specs = [pl.BlockSpec((tm, kdim), lambda m, k: (m, 0)),
                pl.BlockSpec((None, kdim, tn), lambda m, k: (layer, 0, k)),
                pl.BlockSpec((tm, d), lambda m, k: (m, 0), pipeline_mode=single)]
    args = [x, w, res]
    if gated:
        in_specs += [pl.BlockSpec((tm, p.shape[1]), lambda m, k: (m, 0)),
                     pl.BlockSpec((None,) + wp.shape[1:], lambda m, k: (layer, 0, 0))]
        args += [p, wp]
    out_specs = [pl.BlockSpec((tm, d), lambda m, k: (m, 0), pipeline_mode=single)]
    out_shape = [jax.ShapeDtypeStruct((t, d), F32)]
    if with_norm:
        in_specs.append(pl.BlockSpec((1, d), lambda m, k: (0, 0)))
        args.append(gain.reshape(1, d))
        out_specs.append(pl.BlockSpec((tm, d), lambda m, k: (m, 0), pipeline_mode=single))
        out_shape.append(jax.ShapeDtypeStruct((t, d), BF16))
    out = pl.pallas_call(
        functools.partial(_proj_kernel, tn=tn, nn=nn, gated=gated, with_norm=with_norm),
        grid=(t // tm, nn),
        in_specs=in_specs,
        out_specs=out_specs,
        out_shape=out_shape,
        compiler_params=_params("arbitrary", "arbitrary"),
        name="proj_gated" if gated else "proj_residual",
    )(*args)
    return (out[0], out[1]) if with_norm else (out[0], None)


def _tile_row(m, n, te, na):
    return (jnp.minimum(m, na[0] - 1), 0)


def _ffn_up_kernel(te_ref, nl_ref, ts_ref, x_ref, wg_ref, wu_ref, a_ref, wgu_ref, *, tf):
    m = pl.program_id(1)
    nl = nl_ref[m]
    half = x_ref.shape[0] // 2

    def swiglu(x):
        gu = _dot(x, wgu_ref[...])
        return (jax.nn.silu(gu[:, :tf]) * gu[:, tf:]).astype(BF16)

    @pl.when(nl > 0)
    def _():
        @pl.when((m == 0) | (te_ref[m] != te_ref[jnp.maximum(m - 1, 0)]))
        def _():
            wgu_ref[:, :tf] = wg_ref[...].astype(BF16)
            wgu_ref[:, tf:] = wu_ref[...].astype(BF16)

        @pl.when(nl == 2)
        def _():
            a_ref[...] = swiglu(x_ref[...])

        @pl.when(nl == 1)
        def _():
            a_ref[:half] = swiglu(x_ref[:half])
            a_ref[half:] = jnp.zeros((half, tf), BF16)


def ffn_up(x, wg, wu, tile_expert, tile_live, tile_src, *, tm, tf=512):
    r, d = x.shape
    ff = wg.shape[2]
    w_spec = pl.BlockSpec((None, d, tf), lambda n, m, te, nl, ts: (te[m], 0, n))
    return pl.pallas_call(
        functools.partial(_ffn_up_kernel, tf=tf),
        grid_spec=pltpu.PrefetchScalarGridSpec(
            num_scalar_prefetch=3,
            grid=(ff // tf, r // tm),
            in_specs=[pl.BlockSpec((tm, d), lambda n, m, te, nl, ts: (ts[m], 0)), w_spec, w_spec],
            out_specs=pl.BlockSpec((tm, tf), lambda n, m, te, nl, ts: (ts[m], n)),
            scratch_shapes=[pltpu.VMEM((d, 2 * tf), BF16)]),
        out_shape=jax.ShapeDtypeStruct((r, ff), BF16),
        compiler_params=_params("arbitrary", "arbitrary"),
        name="ffn_up",
    )(tile_expert, tile_live, tile_src, x, wg, wu)


def _ffn_down_kernel(te_ref, na_ref, a_ref, wd_ref, res_ref, g_ref, out_ref, hn_ref, *, tn, nn):
    n = pl.program_id(1)
    out_ref[:, pl.ds(pl.multiple_of(n * tn, tn), tn)] = _dot(a_ref[...], wd_ref[...].astype(BF16))

    @pl.when(n == nn - 1)
    def _():
        for r in range(out_ref.shape[0] // EPILOGUE_ROWS):
            rows = slice(r * EPILOGUE_ROWS, (r + 1) * EPILOGUE_ROWS)
            h_new = res_ref[rows, :] + out_ref[rows, :]
            out_ref[rows, :] = h_new
            hn_ref[rows, :] = _rms(h_new, g_ref[...]).astype(BF16)


def ffn_down(a, wd, tile_expert, n_active, *, tm, res, gain, tn=256):
    r, ff = a.shape
    d = wd.shape[2]
    nn = d // tn
    single = pl.Buffered(1)
    return pl.pallas_call(
        functools.partial(_ffn_down_kernel, tn=tn, nn=nn),
        grid_spec=pltpu.PrefetchScalarGridSpec(
            num_scalar_prefetch=2,
            grid=(r // tm, nn),
            in_specs=[pl.BlockSpec((tm, ff), _tile_row, pipeline_mode=single),
                      pl.BlockSpec((None, ff, tn), lambda m, n, te, na: (te[m], 0, n)),
                      pl.BlockSpec((tm, d), _tile_row, pipeline_mode=single),
                      pl.BlockSpec((1, d), lambda m, n, te, na: (0, 0))],
            out_specs=[pl.BlockSpec((tm, d), _tile_row, pipeline_mode=single),
                       pl.BlockSpec((tm, d), _tile_row, pipeline_mode=single)]),
        out_shape=[jax.ShapeDtypeStruct((r, d), F32), jax.ShapeDtypeStruct((r, d), BF16)],
        compiler_params=_params("arbitrary", "arbitrary"),
        name="ffn_down_dense",
    )(tile_expert, n_active, a, wd, res, gain.reshape(1, d))


DOWN_ROWS = 512


def _down_block(m, hl, hs):
    per = HALF_MOE // DOWN_ROWS
    h = m // per
    return jnp.where(hl[h] > 0, m, hs[h] * per + per - 1)


def _ffn_down_routed_kernel(te_ref, hl_ref, hs_ref, a_ref, wd_ref, out_ref, wb_ref, *, tn):
    m = pl.program_id(1)
    per_tile = TM_MOE // DOWN_ROWS
    tile = m // per_tile

    @pl.when(hl_ref[m // (HALF_MOE // DOWN_ROWS)] > 0)
    def _():
        @pl.when((m == 0) | ((m % per_tile == 0) & (te_ref[tile] != te_ref[jnp.maximum(tile - 1, 0)])))
        def _():
            wb_ref[...] = wd_ref[...].astype(BF16)

        part = _dot(a_ref[...], wb_ref[...])
        for j in range(tn // LANES):
            out_ref[j] = part[:, j * LANES:(j + 1) * LANES]


def ffn_down_routed(a, wd, tile_expert, half_live, half_src, tn=512):
    r, ff = a.shape
    d = wd.shape[2]
    per_tile = TM_MOE // DOWN_ROWS
    rows = lambda n, m, te, hl, hs: (_down_block(m, hl, hs), 0)
    return pl.pallas_call(
        functools.partial(_ffn_down_routed_kernel, tn=tn),
        grid_spec=pltpu.PrefetchScalarGridSpec(
            num_scalar_prefetch=3,
            grid=(d // tn, r // DOWN_ROWS),
            in_specs=[pl.BlockSpec((DOWN_ROWS, ff), rows),
                      pl.BlockSpec((None, ff, tn), lambda n, m, te, hl, hs: (te[m // per_tile], 0, n))],
            out_specs=pl.BlockSpec((tn // LANES, DOWN_ROWS, LANES),
                                   lambda n, m, te, hl, hs: (n, _down_block(m, hl, hs), 0)),
            scratch_shapes=[pltpu.VMEM((ff, tn), BF16)]),
        out_shape=jax.ShapeDtypeStruct((ROW_TILES, r, LANES), F32),
        compiler_params=_params("arbitrary", "arbitrary"),
        name="ffn_down_routed",
    )(tile_expert, half_live, half_src, a, wd)


def _router_kernel(h_ref, g_ref, w_ref, o_ref, dense_ref, *xn_ref):
    xn = _rms(h_ref[...], g_ref[...])
    for ref in xn_ref:
        for s in range(ROW_TILES):
            ref[s] = xn[:, s * LANES:(s + 1) * LANES]
    logits = jnp.dot(xn, w_ref[...], preferred_element_type=F32, precision=lax.Precision.HIGHEST)
    lane = lax.broadcasted_iota(jnp.int32, logits.shape, 1)
    logits = jnp.where(lane < N_EXPERTS, logits, -jnp.inf)
    e = jnp.exp(logits - jnp.max(logits, axis=-1, keepdims=True))
    probs = e / jnp.sum(e, axis=-1, keepdims=True)
    p1 = jnp.max(probs, axis=-1, keepdims=True)
    e1 = jnp.min(jnp.where(probs == p1, lane, LANES), axis=-1, keepdims=True)
    rest = jnp.where(lane == e1, -1.0, probs)
    p2 = jnp.max(rest, axis=-1, keepdims=True)
    e2 = jnp.min(jnp.where(rest == p2, lane, LANES), axis=-1, keepdims=True)
    tot = p1 + p2
    g1 = p1 / tot
    g2 = p2 / tot
    o_ref[...] = jnp.where(lane == 0, e1.astype(F32),
                           jnp.where(lane == 1, e2.astype(F32), jnp.where(lane == 2, g1, g2)))
    dense_ref[...] = jnp.where(lane == e1, g1, jnp.where(lane == e2, g2, 0.0))


def route_top2(h, gain, w_router, tm, slabs=False):
    t, d = h.shape
    wpad = jnp.pad(w_router, ((0, 0), (0, LANES - N_EXPERTS)))
    out_specs = [pl.BlockSpec((tm, LANES), lambda i: (i, 0)), pl.BlockSpec((tm, LANES), lambda i: (i, 0))]
    out_shape = [jax.ShapeDtypeStruct((t, LANES), F32), jax.ShapeDtypeStruct((t, LANES), F32)]
    if slabs:
        out_specs.append(pl.BlockSpec((ROW_TILES, tm, LANES), lambda i: (0, i, 0)))
        out_shape.append(jax.ShapeDtypeStruct((ROW_TILES, t, LANES), F32))
    return pl.pallas_call(
        _router_kernel,
        grid=(t // tm,),
        in_specs=[pl.BlockSpec((tm, d), lambda i: (i, 0)),
                  pl.BlockSpec((1, d), lambda i: (0, 0)),
                  pl.BlockSpec((d, LANES), lambda i: (0, 0))],
        out_specs=out_specs,
        out_shape=out_shape,
        compiler_params=_params("arbitrary"),
        name="route_top2",
    )(h, gain.reshape(1, d), wpad)


GATHER_TILE = HALF_MOE


def _start_row_copies(idx_ref, first, count, src_ref, dst_ref, sem):
    def issue(j, carry):
        t = idx_ref[first + j]
        pltpu.make_async_copy(src_ref.at[:, pl.ds(t, 1), :], dst_ref.at[:, pl.ds(j, 1), :], sem).start()
        return carry

    lax.fori_loop(0, count, issue, 0, unroll=8)


def _wait_row_copies(src_ref, dst_ref, sem):
    pltpu.make_async_copy(src_ref.at[:, pl.ds(0, dst_ref.shape[1]), :], dst_ref, sem).wait()


def _gather_kernel(idx_ref, hl_ref, hs_ref, src_ref, o_ref, buf_ref, sem):
    i = pl.program_id(0)

    @pl.when(hl_ref[i] > 0)
    def _():
        _start_row_copies(idx_ref, i * GATHER_TILE, GATHER_TILE, src_ref, buf_ref, sem)
        _wait_row_copies(src_ref, buf_ref, sem)
        for s in range(ROW_TILES):
            o_ref[:, s * LANES:(s + 1) * LANES] = buf_ref[s].astype(o_ref.dtype)


def gather_rows(src, idx, tile_live, tile_src):
    r = idx.shape[0]
    return pl.pallas_call(
        _gather_kernel,
        grid_spec=pltpu.PrefetchScalarGridSpec(
            num_scalar_prefetch=3,
            grid=(r // GATHER_TILE,),
            in_specs=[pl.BlockSpec(memory_space=pl.ANY)],
            out_specs=pl.BlockSpec((GATHER_TILE, D_MODEL), lambda i, idx_ref, hl, hs: (hs[i], 0)),
            scratch_shapes=[pltpu.VMEM((ROW_TILES, GATHER_TILE, LANES), F32), pltpu.SemaphoreType.DMA(())]),
        out_shape=jax.ShapeDtypeStruct((r, D_MODEL), BF16),
        compiler_params=_params("arbitrary"),
        name="gather_rows",
    )(idx, tile_live, tile_src, src)


def _combine_kernel(pos_ref, h_ref, r_ref, o_ref, g_ref, h2_ref, hn_ref, o1_ref, o2_ref, sem1, sem2, *, tm, t):
    i = pl.program_id(0)
    _start_row_copies(pos_ref, i * tm, tm, o_ref, o1_ref, sem1)
    _start_row_copies(pos_ref, t + i * tm, tm, o_ref, o2_ref, sem2)
    _wait_row_copies(o_ref, o1_ref, sem1)
    _wait_row_copies(o_ref, o2_ref, sem2)
    g1 = r_ref[:, 2:3]
    g2 = r_ref[:, 3:4]
    for s in range(ROW_TILES):
        sl = slice(s * LANES, (s + 1) * LANES)
        h2_ref[:, sl] = h_ref[:, sl] + (g1 * o1_ref[s] + g2 * o2_ref[s])
    hn_ref[...] = _rms(h2_ref[...], g_ref[...]).astype(BF16)


def combine_rows(h, r, o, pos, gain, tm=TM_MID):
    t, d = h.shape
    tiled = pltpu.VMEM((ROW_TILES, tm, LANES), F32)
    return pl.pallas_call(
        functools.partial(_combine_kernel, tm=tm, t=t),
        grid_spec=pltpu.PrefetchScalarGridSpec(
            num_scalar_prefetch=1,
            grid=(t // tm,),
            in_specs=[pl.BlockSpec((tm, d), lambda i, pos_ref: (i, 0)),
                      pl.BlockSpec((tm, LANES), lambda i, pos_ref: (i, 0)),
                      pl.BlockSpec(memory_space=pl.ANY),
                      pl.BlockSpec((1, d), lambda i, pos_ref: (0, 0))],
            out_specs=[pl.BlockSpec((tm, d), lambda i, pos_ref: (i, 0)),
                       pl.BlockSpec((tm, d), lambda i, pos_ref: (i, 0))],
            scratch_shapes=[tiled, tiled, pltpu.SemaphoreType.DMA(()), pltpu.SemaphoreType.DMA(())]),
        out_shape=[jax.ShapeDtypeStruct((t, d), F32), jax.ShapeDtypeStruct((t, d), BF16)],
        compiler_params=_params("arbitrary"),
        name="combine_rows",
    )(pos, h, r, o, gain.reshape(1, d))


def moe_prompt(h, ffn_gain, w_router, wg, wu, wd, layer, ple_gain):
    r, _, xn = route_top2(h, ffn_gain, w_router, TM_BIG, slabs=True)
    experts = jnp.concatenate([r[:, 0], r[:, 1]]).astype(jnp.int32)
    onehot = (experts[:, None] == jnp.arange(N_EXPERTS)[None, :]).astype(jnp.int32)
    rank = jnp.sum(jnp.cumsum(onehot, axis=0) * onehot, axis=1) - 1
    counts = jnp.sum(onehot, axis=0)
    halves = (counts + HALF_MOE - 1) // HALF_MOE
    tiles = (halves + 1) // 2
    tile_end = jnp.cumsum(tiles)
    tile_start = tile_end - tiles
    dest = jnp.sum(onehot * tile_start[None, :], axis=1) * TM_MOE + rank
    token = jnp.tile(jnp.arange(T, dtype=jnp.int32), 2)
    src_token = jnp.zeros((R_PAD,), jnp.int32).at[dest].set(token)
    n_active = tile_end[-1]
    tile_idx = jnp.arange(MOE_TILES, dtype=jnp.int32)
    tile_src = jnp.minimum(tile_idx, n_active - 1)
    of_expert = (tile_src[:, None] >= tile_end[None, :]).astype(jnp.int32)
    tile_expert = jnp.sum(of_expert, axis=1)
    mine = (tile_expert[:, None] == jnp.arange(N_EXPERTS)[None, :]).astype(jnp.int32)
    left = jnp.sum(mine * (halves[None, :] - 2 * (tile_src[:, None] - tile_start[None, :])), axis=1)
    tile_live = jnp.where(tile_idx < n_active, jnp.clip(left, 1, 2), 0).astype(jnp.int32)
    half_idx = jnp.arange(2 * MOE_TILES, dtype=jnp.int32)
    half_live = (half_idx % 2 < tile_live[half_idx // 2]).astype(jnp.int32)
    half_src = lax.cummax(jnp.where(half_live > 0, half_idx, 0))
    xs = gather_rows(xn, src_token, half_live, half_src)
    flat = lambda w: w.reshape((-1,) + w.shape[2:])
    tile_w = (tile_expert + layer * N_EXPERTS).astype(jnp.int32)
    a = ffn_up(xs, flat(wg), flat(wu), tile_w, tile_live, tile_src, tm=TM_MOE)
    o = ffn_down_routed(a, flat(wd), tile_w, half_live, half_src)
    return combine_rows(h, r, o, dest, ple_gain)


def _headnorm_kernel(q_ref, k_ref, qg_ref, kg_ref, qn_ref, kn_ref):
    def norm_pairs(x_ref, g_ref, o_ref, scale):
        lo = lax.broadcasted_iota(jnp.int32, (x_ref.shape[0], LANES), 1) < HEAD_DIM
        for j in range(x_ref.shape[1] // LANES):
            x = x_ref[:, j * LANES:(j + 1) * LANES]
            s = x * x
            s_lo = jnp.sum(jnp.where(lo, s, 0.0), axis=-1, keepdims=True)
            s_hi = jnp.sum(jnp.where(lo, 0.0, s), axis=-1, keepdims=True)
            ms = jnp.where(lo, s_lo, s_hi) * (1.0 / HEAD_DIM)
            y = x * lax.rsqrt(ms + NORM_EPS) * g_ref[...]
            o_ref[:, j * LANES:(j + 1) * LANES] = (y * scale).astype(o_ref.dtype)

    norm_pairs(q_ref, qg_ref, qn_ref, HEAD_DIM ** -0.5)
    norm_pairs(k_ref, kg_ref, kn_ref, 1.0)


def head_norm(qkv, q_gain, k_gain, q_dtype, tm):
    t = qkv.shape[0]
    return pl.pallas_call(
        _headnorm_kernel,
        grid=(t // tm,),
        in_specs=[pl.BlockSpec((tm, NQ), lambda i: (i, 0)),
                  pl.BlockSpec((tm, NKV), lambda i: (i, NQ // NKV)),
                  pl.BlockSpec((1, LANES), lambda i: (0, 0)),
                  pl.BlockSpec((1, LANES), lambda i: (0, 0))],
        out_specs=[pl.BlockSpec((tm, NQ), lambda i: (i, 0)),
                   pl.BlockSpec((tm, NKV), lambda i: (i, 0))],
        out_shape=[jax.ShapeDtypeStruct((t, NQ), q_dtype), jax.ShapeDtypeStruct((t, NKV), F32)],
        compiler_params=_params("arbitrary"),
        name="head_norm",
    )(qkv, qkv, jnp.tile(q_gain, 2).reshape(1, LANES), jnp.tile(k_gain, 2).reshape(1, LANES))


def _attn_kernel(slope_ref, sink_ref, q_ref, kp_ref, kc_ref, vp_ref, vc_ref, o_ref):
    rows = WINDOW
    nkeys = 2 * WINDOW
    rep = ATTN_HEADS // ATTN_KV_HEADS
    qi = lax.broadcasted_iota(jnp.int32, (rows, nkeys), 0)
    kj = lax.broadcasted_iota(jnp.int32, (rows, nkeys), 1)
    delta = WINDOW + qi - kj
    mask = (delta >= 0) & (delta <= WINDOW) & ((kj >= WINDOW) | (pl.program_id(1) > 0))
    deltaf = delta.astype(F32)
    lo_q = lax.broadcasted_iota(jnp.int32, (rows, LANES), 1) < HEAD_DIM

    for jp in range(ATTN_KV_HEADS // 2):
        ksl = slice(jp * LANES, (jp + 1) * LANES)
        k2 = jnp.concatenate([kp_ref[:, ksl], kc_ref[:, ksl]], axis=0).astype(BF16)
        v2 = jnp.concatenate([vp_ref[:, ksl], vc_ref[:, ksl]], axis=0)
        v2r = pltpu.roll(v2, HEAD_DIM, 1).astype(BF16)
        v2 = v2.astype(BF16)
        for qp in range(rep):
            col = (jp * rep + qp) * LANES
            q2 = q_ref[:, col:col + LANES]
            q2r = pltpu.roll(q2, HEAD_DIM, 1)
            g_half = qp // (rep // 2)
            outs = []
            for hh in range(2):
                h = (jp * rep + qp) * 2 + hh
                qsel = q2 if hh == g_half else q2r
                qm = jnp.where(lo_q if g_half == 0 else ~lo_q, qsel, jnp.zeros_like(qsel))
                s = _dot_nt(qm, k2) - slope_ref[h] * deltaf
                s = jnp.where(mask, s, -jnp.inf)
                sink = sink_ref[h]
                m = jnp.maximum(jnp.max(s, axis=-1, keepdims=True), sink)
                e = jnp.exp(s - m)
                probs = e / (jnp.sum(e, axis=-1, keepdims=True) + jnp.exp(sink - m))
                outs.append(_dot(probs.astype(BF16), v2 if hh == g_half else v2r))
            o_ref[:, col:col + LANES] = jnp.where(lo_q, outs[0], outs[1]).astype(o_ref.dtype)


def attention_prompt(hn, h, w_qkv, layer, q_gain, k_gain, sinks, w_o, next_gain):
    qkv = matmul_cols(hn, w_qkv, layer, NQ + 2 * NKV)
    qn, kn = head_norm(qkv, q_gain, k_gain, BF16, TM_MID)
    nblk = SEQ // WINDOW
    vcol = NQ // NKV + 1
    cur = lambda c: (lambda b, n: (b * nblk + n, c))
    prev = lambda c: (lambda b, n: (jnp.maximum(b * nblk + n - 1, 0), c))
    smem = pl.BlockSpec(memory_space=pltpu.SMEM)
    o = pl.pallas_call(
        _attn_kernel,
        grid=(BATCH, nblk),
        in_specs=[smem, smem,
                  pl.BlockSpec((WINDOW, NQ), cur(0)),
                  pl.BlockSpec((WINDOW, NKV), prev(0)),
                  pl.BlockSpec((WINDOW, NKV), cur(0)),
                  pl.BlockSpec((WINDOW, NKV), prev(vcol)),
                  pl.BlockSpec((WINDOW, NKV), cur(vcol))],
        out_specs=pl.BlockSpec((WINDOW, NQ), cur(0)),
        out_shape=jax.ShapeDtypeStruct((T, NQ), BF16),
        compiler_params=_params("arbitrary", "arbitrary"),
        name="window_attention",
    )(_alibi_slopes(), sinks, qn, kn, kn, qkv, qkv)
    h2, hn2 = proj_residual(o, w_o, layer, h, next_gain)
    kv_shape = (BATCH, WINDOW, ATTN_KV_HEADS, HEAD_DIM)
    new_k = kn.reshape(BATCH, SEQ, NKV)[:, -WINDOW:].reshape(kv_shape)
    new_v = qkv[:, NQ + NKV:].reshape(BATCH, SEQ, NKV)[:, -WINDOW:].reshape(kv_shape)
    return h2, hn2, (new_k, new_v)


SCONV_TILE = 512
SCONV_TAIL = SUBLANES


def _sconv_kernel(gb_ref, gc_ref, u_ref, gcp_ref, up_ref, w_ref, y_ref, tail_ref):
    rows = gb_ref.shape[0]
    cu = gc_ref[...] * u_ref[...]
    prev = jnp.where(pl.program_id(1) > 0, gcp_ref[...] * up_ref[...], 0.0)
    row = lax.broadcasted_iota(jnp.int32, cu.shape, 0)
    s1 = jnp.where(row == 0, prev[SCONV_TAIL - 1:], pltpu.roll(cu, 1, 0))
    s2 = jnp.where(row == 0, prev[SCONV_TAIL - 2:SCONV_TAIL - 1],
                   jnp.where(row == 1, prev[SCONV_TAIL - 1:], pltpu.roll(cu, 2, 0)))
    conv = w_ref[0:1] * s2 + w_ref[1:2] * s1 + w_ref[2:3] * cu
    y_ref[...] = (gb_ref[...] * conv).astype(y_ref.dtype)
    tail_ref[...] = cu[rows - SCONV_TAIL:]


def sconv_prompt(hn, h, w_in, conv_w, layer, w_out, next_gain):
    d = D_MODEL
    proj = matmul_cols(hn, w_in, layer, 3 * d)
    nt = SEQ // SCONV_TILE
    tpb = SCONV_TILE // SCONV_TAIL
    cur = lambda c: (lambda b, i: (b * nt + i, c))
    prev = lambda c: (lambda b, i: (jnp.maximum((b * nt + i) * tpb - 1, 0), c))
    y, tails = pl.pallas_call(
        _sconv_kernel,
        grid=(BATCH, nt),
        in_specs=[pl.BlockSpec((SCONV_TILE, d), cur(0)),
                  pl.BlockSpec((SCONV_TILE, d), cur(1)),
                  pl.BlockSpec((SCONV_TILE, d), cur(2)),
                  pl.BlockSpec((SCONV_TAIL, d), prev(1)),
                  pl.BlockSpec((SCONV_TAIL, d), prev(2)),
                  pl.BlockSpec((None, SCONV_WIDTH, d), lambda b, i: (layer, 0, 0))],
        out_specs=[pl.BlockSpec((SCONV_TILE, d), cur(0)),
                   pl.BlockSpec((None, SCONV_TAIL, d), lambda b, i: (b * nt + i, 0, 0))],
        out_shape=[jax.ShapeDtypeStruct((T, d), BF16),
                   jax.ShapeDtypeStruct((BATCH * nt, SCONV_TAIL, d), F32)],
        compiler_params=_params("arbitrary", "arbitrary"),
        name="sconv_prompt",
    )(proj, proj, proj, proj, proj, conv_w)
    h2, hn2 = proj_residual(y, w_out, layer, h, next_gain)
    new_state = tails.reshape(BATCH, nt, SCONV_TAIL, d)[:, -1, -(SCONV_WIDTH - 1):]
    return h2, hn2, new_state


XBC_CHUNK = 512
GROUP_W = D_INNER // SSM_GROUPS
HEADS_PER_GROUP = SSM_HEADS // SSM_GROUPS


def _ssd_kernel(z_ref, xc_ref, xp_ref, dt_ref, cw_ref, cb_ref, dtb_ref, alog_ref, dsk_ref, nw_ref,
                y_ref, hfin_ref, state_ref, xact_ref, acst_ref, yg_ref):
    c = pl.program_id(1)
    q = SSM_CHUNK

    @pl.when(c == 0)
    def _():
        state_ref[...] = jnp.zeros_like(state_ref)

    has_prev = c > 0
    row = lax.broadcasted_iota(jnp.int32, (q, XBC_CHUNK), 0)
    for j in range(SSM_CONV_DIM // XBC_CHUNK):
        sl = slice(j * XBC_CHUNK, (j + 1) * XBC_CHUNK)
        cur = xc_ref[:, sl]
        prv = jnp.where(has_prev, xp_ref[:, sl], 0.0)
        acc = None
        for k in range(SSM_CONV_WIDTH):
            sh = SSM_CONV_WIDTH - 1 - k
            if sh == 0:
                term = cur
            else:
                term = jnp.where(row < sh, pltpu.roll(prv, sh, 0), pltpu.roll(cur, sh, 0))
            term = cw_ref[k:k + 1, sl] * term
            acc = term if acc is None else acc + term
        xact_ref[:, sl] = jax.nn.silu(acc + cb_ref[:, sl])

    dt = _softplus(dt_ref[...] + dtb_ref[...])
    da = dt * (-jnp.exp(alog_ref[...]))
    tri = (lax.broadcasted_iota(jnp.int32, (q, q), 0) >= lax.broadcasted_iota(jnp.int32, (q, q), 1))
    a_cs = jnp.dot(tri.astype(F32), da, preferred_element_type=F32, precision=lax.Precision.HIGHEST)
    acst_ref[...] = a_cs.T

    def group(g, carry):
        goff = pl.multiple_of(g * GROUP_W, GROUP_W)
        xs_g = xact_ref[:, pl.ds(goff, GROUP_W)]
        bm = xact_ref[:, pl.ds(pl.multiple_of(D_INNER + g * D_STATE, D_STATE), D_STATE)]
        cm = xact_ref[:, pl.ds(pl.multiple_of(D_INNER + SSM_GROUPS * D_STATE + g * D_STATE, D_STATE), D_STATE)]
        cm_b = cm.astype(BF16)
        bm_b = bm.astype(BF16)
        cb = _dot_nt(cm_b, bm_b)
        spread = (lax.broadcasted_iota(jnp.int32, (LANES, GROUP_W), 0) == g * HEADS_PER_GROUP
                  + lax.broadcasted_iota(jnp.int32, (LANES, GROUP_W), 1) // SSM_HEAD_DIM).astype(F32)
        acs_w = jnp.dot(a_cs, spread, preferred_element_type=F32, precision=lax.Precision.HIGHEST)
        dt_w = jnp.dot(dt, spread, preferred_element_type=F32, precision=lax.Precision.HIGHEST)
        acs_last_w = acs_w[q - 1:q]
        xdt = xs_g * dt_w
        xdt_b = xdt.astype(BF16)
        h_in = state_ref[g]
        h_in2 = h_in.reshape(GROUP_W, D_STATE)
        y_off = _dot_nt(cm_b, h_in2.astype(BF16)) * jnp.exp(acs_w)
        st = _dot_tn((xdt * jnp.exp(acs_last_w - acs_w)).astype(BF16), bm_b)
        chunk_decay = jnp.exp(acst_ref[pl.ds(pl.multiple_of(g * HEADS_PER_GROUP, HEADS_PER_GROUP),
                                             HEADS_PER_GROUP), :][:, q - 1:q])
        state_ref[g] = (chunk_decay.reshape(HEADS_PER_GROUP, 1, 1) * h_in
                        + st.reshape(HEADS_PER_GROUP, SSM_HEAD_DIM, D_STATE))
        for hh in range(HEADS_PER_GROUP):
            hsl = slice(hh * SSM_HEAD_DIM, (hh + 1) * SSM_HEAD_DIM)
            acs_col = acs_w[:, hh * SSM_HEAD_DIM:hh * SSM_HEAD_DIM + 1]
            acs_row = acst_ref[pl.ds(g * HEADS_PER_GROUP + hh, 1), :]
            seg = acs_col - acs_row
            decay_in = jnp.where(tri, jnp.exp(jnp.where(tri, seg, 0.0)), 0.0)
            yg_ref[:, hsl] = _dot((cb * decay_in).astype(BF16), xdt_b[:, hsl])
        y_g = yg_ref[...] + y_off + dsk_ref[:, pl.ds(goff, GROUP_W)] * xs_g
        gg = y_g * jax.nn.silu(z_ref[:, pl.ds(goff, GROUP_W)])
        gg = gg * lax.rsqrt(jnp.mean(gg * gg, axis=-1, keepdims=True) + NORM_EPS)
        y_ref[:, pl.ds(goff, GROUP_W)] = (gg * nw_ref[:, pl.ds(goff, GROUP_W)]).astype(y_ref.dtype)
        return carry

    lax.fori_loop(0, SSM_GROUPS, group, 0)

    @pl.when(c == pl.num_programs(1) - 1)
    def _():
        hfin_ref[...] = state_ref[...].reshape(hfin_ref.shape)


def _pad_heads(v):
    return jnp.pad(v, (0, LANES - SSM_HEADS)).reshape(1, LANES)


def ssd_prompt(hn, h, w_in, layer, conv_w, conv_b, dt_bias, a_log, d_skip, norm_w, w_out, next_gain):
    cd = SSM_CONV_DIM
    z = matmul_cols(hn, w_in, layer, D_INNER, col_off=0)
    xbc = matmul_cols(hn, w_in, layer, cd, col_off=D_INNER)
    w_dt = jnp.pad(w_in[layer, :, D_INNER + cd:], ((0, 0), (0, LANES - SSM_HEADS)))[None]
    dt_raw = matmul_cols(hn, w_dt, 0, LANES)
    dsk_full = jnp.repeat(d_skip, SSM_HEAD_DIM).reshape(1, D_INNER)
    nc = SEQ // SSM_CHUNK
    cur = lambda b, c: (b * nc + c, 0)
    whole = lambda b, c: (0, 0)
    y, h_fin = pl.pallas_call(
        _ssd_kernel,
        grid=(BATCH, nc),
        in_specs=[pl.BlockSpec((SSM_CHUNK, D_INNER), cur),
                  pl.BlockSpec((SSM_CHUNK, cd), cur),
                  pl.BlockSpec((SSM_CHUNK, cd), lambda b, c: (jnp.maximum(b * nc + c - 1, 0), 0)),
                  pl.BlockSpec((SSM_CHUNK, LANES), cur),
                  pl.BlockSpec((SSM_CONV_WIDTH, cd), whole),
                  pl.BlockSpec((1, cd), whole),
                  pl.BlockSpec((1, LANES), whole),
                  pl.BlockSpec((1, LANES), whole),
                  pl.BlockSpec((1, D_INNER), whole),
                  pl.BlockSpec((1, D_INNER), whole)],
        out_specs=[pl.BlockSpec((SSM_CHUNK, D_INNER), cur),
                   pl.BlockSpec((None, SSM_HEADS, SSM_HEAD_DIM, D_STATE), lambda b, c: (b, 0, 0, 0))],
        out_shape=[jax.ShapeDtypeStruct((T, D_INNER), BF16),
                   jax.ShapeDtypeStruct((BATCH, SSM_HEADS, SSM_HEAD_DIM, D_STATE), F32)],
        scratch_shapes=[pltpu.VMEM((SSM_GROUPS, HEADS_PER_GROUP, SSM_HEAD_DIM, D_STATE), F32),
                        pltpu.VMEM((SSM_CHUNK, cd), F32),
                        pltpu.VMEM((LANES, SSM_CHUNK), F32),
                        pltpu.VMEM((SSM_CHUNK, GROUP_W), F32)],
        compiler_params=_params("arbitrary", "arbitrary"),
        name="ssd_prompt",
    )(z, xbc, xbc, dt_raw, conv_w, conv_b.reshape(1, cd), _pad_heads(dt_bias), _pad_heads(a_log), dsk_full,
      norm_w.reshape(1, D_INNER))
    h2, hn2 = proj_residual(y, w_out, layer, h, next_gain, tm=TM_MID)
    new_conv = xbc.reshape(BATCH, SEQ, cd)[:, -(SSM_CONV_WIDTH - 1):]
    return h2, hn2, (h_fin, new_conv)


def _split(x):
    hi = x.astype(BF16)
    lo = (x - hi.astype(F32)).astype(BF16)
    return hi, lo


HP_K_CHUNK = 512


def _hp_kernel(*refs, nk, x_mode, epi):
    refs = list(refs)
    if x_mode == "norm":
        x_ref, g_ref = refs[:2]
        refs = refs[2:]
        x = _rms(x_ref[...], g_ref[...])
    elif x_mode == "swiglu":
        g_ref, u_ref = refs[:2]
        refs = refs[2:]
        x = jax.nn.silu(g_ref[...]) * u_ref[...]
    else:
        x = refs[0][...]
        refs = refs[1:]
    w_ref = refs[0]
    refs = refs[1:]
    if epi == "gated":
        res_ref, pp_ref, o_ref = refs
    elif epi == "res":
        res_ref, o_ref = refs
    else:
        (o_ref,) = refs
    k = pl.program_id(2)
    rows = x.shape[0]
    x_hi, x_lo = _split(x)
    x_both = jnp.concatenate([x_hi, x_lo], axis=0)
    part = None
    chunk = min(HP_K_CHUNK, w_ref.shape[0])
    for c in range(w_ref.shape[0] // chunk):
        ksl = slice(c * chunk, (c + 1) * chunk)
        w_hi, w_lo = _split(w_ref[ksl, :])
        both = _dot(x_both[:, ksl], w_hi)
        term = both[:rows] + both[rows:] + _dot(x_hi[:, ksl], w_lo)
        part = term if part is None else part + term

    @pl.when(k == 0)
    def _():
        o_ref[...] = part

    @pl.when(k > 0)
    def _():
        o_ref[...] += part

    if epi is not None:
        @pl.when(k == nk - 1)
        def _():
            acc = o_ref[...]
            if epi == "gated":
                acc = jax.nn.sigmoid(acc) * pp_ref[...]
            o_ref[...] = res_ref[...] + acc


def hp_linear(w, widx, n_out, *, x=None, norm=None, swiglu=None, col_off=0, n_e=1, tk, tn, res=None, pp=None):
    kdim = w.shape[1]
    nk = kdim // tk
    off = col_off // tn
    assert kdim % tk == 0 and n_out % tn == 0 and col_off % tn == 0
    if norm is not None:
        assert nk == 1
        x_mode = "norm"
        args = [norm[0], norm[1].reshape(1, kdim)]
        in_specs = [pl.BlockSpec((NS, tk), lambda e, n, k: (0, k)),
                    pl.BlockSpec((1, tk), lambda e, n, k: (0, k))]
    elif swiglu is not None:
        x_mode = "swiglu"
        args = list(swiglu)
        in_specs = [pl.BlockSpec((None, NS, tk), lambda e, n, k: (e, 0, k))] * 2
    else:
        x_mode = "plain"
        args = [x]
        in_specs = [pl.BlockSpec((NS, tk), lambda e, n, k: (0, k))]
    args.append(w)
    in_specs.append(pl.BlockSpec((None, tk, tn), lambda e, n, k: (widx + e, k, n + off)))
    epi = None
    if res is not None:
        epi = "res"
        args.append(res)
        in_specs.append(pl.BlockSpec((NS, tn), lambda e, n, k: (0, n)))
        if pp is not None:
            epi = "gated"
            args.append(pp)
            in_specs.append(pl.BlockSpec((NS, tn), lambda e, n, k: (0, n)))
    return pl.pallas_call(
        functools.partial(_hp_kernel, nk=nk, x_mode=x_mode, epi=epi),
        grid=(n_e, n_out // tn, nk),
        in_specs=in_specs,
        out_specs=pl.BlockSpec((None, NS, tn), lambda e, n, k: (e, 0, n)),
        out_shape=jax.ShapeDtypeStruct((n_e, NS, n_out), F32),
        compiler_params=_params("arbitrary", "arbitrary", "arbitrary"),
        name="hp_linear_" + x_mode,
    )(*args)


def _attn_step_kernel(slope_ref, sink_ref, q_ref, kc_ref, vc_ref, kn_ref, vn_ref, o_ref):
    rep = ATTN_HEADS // ATTN_KV_HEADS
    lo = lax.broadcasted_iota(jnp.int32, (1, LANES), 1) < HEAD_DIM
    delta = (WINDOW - lax.broadcasted_iota(jnp.int32, (WINDOW, 1), 0)).astype(F32)
    for jp in range(ATTN_KV_HEADS // 2):
        ksl = slice(jp * LANES, (jp + 1) * LANES)
        k2 = kc_ref[:, ksl]
        v2 = vc_ref[:, ksl]
        kn2 = kn_ref[:, ksl]
        vn2 = vn_ref[:, ksl]
        for qp in range(rep):
            col = (jp * rep + qp) * LANES
            q2 = q_ref[:, col:col + LANES]
            q2r = pltpu.roll(q2, HEAD_DIM, 1)
            g_half = qp // (rep // 2)
            kv_lanes = lo if g_half == 0 else ~lo
            out = jnp.zeros((1, LANES), F32)
            for hh in range(2):
                h = (jp * rep + qp) * 2 + hh
                qm = jnp.where(kv_lanes, q2 if hh == g_half else q2r, 0.0)
                s = jnp.sum(k2 * qm, axis=1, keepdims=True) - slope_ref[h] * delta
                s_new = jnp.sum(kn2 * qm, axis=1, keepdims=True)
                sink = sink_ref[h]
                m = jnp.maximum(jnp.maximum(jnp.max(s, axis=0, keepdims=True), s_new), sink)
                e = jnp.exp(s - m)
                e_new = jnp.exp(s_new - m)
                denom = jnp.sum(e, axis=0, keepdims=True) + e_new + jnp.exp(sink - m)
                o = jnp.sum((e / denom) * v2, axis=0, keepdims=True) + (e_new / denom) * vn2
                if hh != g_half:
                    o = pltpu.roll(o, HEAD_DIM, 1)
                out = jnp.where(lo if hh == 0 else ~lo, o, out)
            o_ref[:, col:col + LANES] = out


def _sconv_step_kernel(proj_ref, p_ref, w_ref, y_ref, cu_ref):
    d = D_MODEL
    cu = proj_ref[:, d:2 * d] * proj_ref[:, 2 * d:]
    conv = w_ref[0:1] * p_ref[0] + w_ref[1:2] * p_ref[1] + w_ref[2:3] * cu
    y_ref[...] = proj_ref[:, :d] * conv
    cu_ref[...] = cu


def _ssd_step_pre_kernel(xc_ref, p_ref, dt_ref, cw_ref, cb_ref, dtb_ref, alog_ref, xact_ref, dt_out_ref, dec_ref):
    acc = cw_ref[0:1] * p_ref[0] + cw_ref[1:2] * p_ref[1] + cw_ref[2:3] * p_ref[2] + cw_ref[3:4] * xc_ref[...]
    xact_ref[...] = jax.nn.silu(acc + cb_ref[...])
    dt = _softplus(dt_ref[...] + dtb_ref[...])
    dt_out_ref[...] = dt
    dec_ref[...] = jnp.exp(dt * (-jnp.exp(alog_ref[...])))


def _ssd_step_kernel(dt_ref, dec_ref, dsk_ref, xt_ref, bm_ref, cm_ref, h0_ref, yt_ref, h1_ref):
    b = pl.program_id(0)
    xt = xt_ref[...]
    lane = lax.broadcasted_iota(jnp.int32, xt.shape, 1)

    def head(h, yt):
        g = h // HEADS_PER_GROUP
        sel = lane == h
        x_col = jnp.sum(jnp.where(sel, xt, 0.0), axis=1, keepdims=True)
        bm = bm_ref[pl.ds(g, 1), :]
        cm = cm_ref[pl.ds(g, 1), :]
        h1 = dec_ref[b, h] * h0_ref[h] + (dt_ref[b, h] * x_col) * bm
        h1_ref[h] = h1
        y_col = jnp.sum(h1 * cm, axis=1, keepdims=True) + dsk_ref[h] * x_col
        return jnp.where(sel, y_col, yt)

    yt_ref[...] = lax.fori_loop(0, SSM_HEADS, head, jnp.zeros(xt.shape, F32), unroll=4)


def _ssd_gate_kernel(y_ref, z_ref, nw_ref, o_ref):
    for g in range(SSM_GROUPS):
        sl = slice(g * GROUP_W, (g + 1) * GROUP_W)
        gg = y_ref[:, sl] * jax.nn.silu(z_ref[:, sl])
        gg = gg * lax.rsqrt(jnp.mean(gg * gg, axis=-1, keepdims=True) + NORM_EPS)
        o_ref[:, sl] = gg * nw_ref[:, sl]


def _moe_mix_kernel(h_ref, gates_ref, d_ref, o_ref):
    lane = lax.broadcasted_iota(jnp.int32, gates_ref.shape, 1)
    y = jnp.zeros(h_ref.shape, F32)
    for e in range(N_EXPERTS):
        gate = jnp.sum(jnp.where(lane == e, gates_ref[...], 0.0), axis=1, keepdims=True)
        y = y + gate * d_ref[e]
    o_ref[...] = h_ref[...] + y


def _whole(shape):
    return pl.BlockSpec(shape, lambda *_: (0,) * len(shape))


def _small_call(body, out_shapes, *args, name):
    multi = isinstance(out_shapes, (list, tuple))
    outs = list(out_shapes) if multi else [out_shapes]
    res = pl.pallas_call(
        body,
        grid=(1,),
        in_specs=[_whole(a.shape) for a in args],
        out_specs=[_whole(o.shape) for o in outs],
        out_shape=outs,
        compiler_params=_params("arbitrary"),
        name=name,
    )(*args)
    return res if multi else res[0]


def _sds(*shape):
    return jax.ShapeDtypeStruct(shape, F32)


def attention_sample(h, norm_gain, cache_k, cache_v, w_qkv, layer, q_gain, k_gain, sinks, w_o):
    qkv = hp_linear(w_qkv, layer, NQ + 2 * NKV, norm=(h, norm_gain), tk=D_MODEL, tn=1024)[0]
    qn, kn = head_norm(qkv, q_gain, k_gain, F32, NS)
    v_new = qkv[:, NQ + NKV:]
    smem = pl.BlockSpec(memory_space=pltpu.SMEM)
    rowsq = pl.BlockSpec((None, 1, NQ), lambda b: (b, 0, 0))
    rowsk = pl.BlockSpec((None, 1, NKV), lambda b: (b, 0, 0))
    cache = pl.BlockSpec((None, WINDOW, NKV), lambda b: (b, 0, 0))
    o = pl.pallas_call(
        _attn_step_kernel,
        grid=(NS,),
        in_specs=[smem, smem, rowsq, cache, cache, rowsk, rowsk],
        out_specs=rowsq,
        out_shape=_sds(NS, 1, NQ),
        compiler_params=_params("arbitrary"),
        name="attention_sample",
    )(_alibi_slopes(), sinks, qn.reshape(NS, 1, NQ), cache_k.reshape(NS, WINDOW, NKV),
      cache_v.reshape(NS, WINDOW, NKV), kn.reshape(NS, 1, NKV), v_new.reshape(NS, 1, NKV))
    h2 = hp_linear(w_o, layer, D_MODEL, x=o.reshape(NS, NQ), tk=1024, tn=1024, res=h)[0]
    kv_row = (NS, 1, ATTN_KV_HEADS, HEAD_DIM)
    new_k = jnp.concatenate([cache_k[:, 1:], kn.reshape(kv_row)], axis=1)
    new_v = jnp.concatenate([cache_v[:, 1:], v_new.reshape(kv_row)], axis=1)
    return h2, (new_k, new_v)


def sconv_sample(h, norm_gain, past, w_in, conv_w, layer, w_out):
    d = D_MODEL
    proj = hp_linear(w_in, layer, 3 * d, norm=(h, norm_gain), tk=d, tn=512)[0]
    y, cu = _small_call(_sconv_step_kernel, [_sds(NS, d), _sds(NS, d)],
                        proj, past.transpose(1, 0, 2), conv_w[layer], name="sconv_sample")
    h2 = hp_linear(w_out, layer, d, x=y, tk=1024, tn=1024, res=h)[0]
    return h2, jnp.concatenate([past[:, 1:], cu[:, None]], axis=1)


def ssd_sample(h, norm_gain, conv_past, h0, w_in, layer, conv_w, conv_b, dt_bias, a_log, d_skip, norm_w, w_out):
    cd = SSM_CONV_DIM
    z = hp_linear(w_in, layer, D_INNER, norm=(h, norm_gain), tk=D_MODEL, tn=1024)[0]
    xbc = hp_linear(w_in, layer, cd, norm=(h, norm_gain), col_off=D_INNER, tk=D_MODEL, tn=1024)[0]
    w_dt = jnp.pad(w_in[layer, :, D_INNER + cd:], ((0, 0), (0, LANES - SSM_HEADS)))[None]
    dt_raw = hp_linear(w_dt, 0, LANES, norm=(h, norm_gain), tk=D_MODEL, tn=LANES)[0]
    xact, dt, dec = _small_call(
        _ssd_step_pre_kernel, [_sds(NS, cd), _sds(NS, LANES), _sds(NS, LANES)],
        xbc, conv_past.transpose(1, 0, 2), dt_raw, conv_w, conv_b.reshape(1, cd), _pad_heads(dt_bias),
        _pad_heads(a_log), name="ssd_sample_pre")
    xt = xact[:, :D_INNER].reshape(NS, SSM_HEADS, SSM_HEAD_DIM).transpose(0, 2, 1)
    bm = xact[:, D_INNER:D_INNER + SSM_GROUPS * D_STATE].reshape(NS, SSM_GROUPS, D_STATE)
    cm = xact[:, D_INNER + SSM_GROUPS * D_STATE:].reshape(NS, SSM_GROUPS, D_STATE)
    smem = pl.BlockSpec(memory_space=pltpu.SMEM)
    per_b = lambda *shape: pl.BlockSpec((None,) + shape, lambda b: (b,) + (0,) * len(shape))
    yt, h1 = pl.pallas_call(
        _ssd_step_kernel,
        grid=(NS,),
        in_specs=[smem, smem, smem,
                  per_b(SSM_HEAD_DIM, SSM_HEADS), per_b(SSM_GROUPS, D_STATE), per_b(SSM_GROUPS, D_STATE),
                  per_b(SSM_HEADS, SSM_HEAD_DIM, D_STATE)],
        out_specs=[per_b(SSM_HEAD_DIM, SSM_HEADS), per_b(SSM_HEADS, SSM_HEAD_DIM, D_STATE)],
        out_shape=[_sds(NS, SSM_HEAD_DIM, SSM_HEADS), _sds(NS, SSM_HEADS, SSM_HEAD_DIM, D_STATE)],
        compiler_params=_params("arbitrary"),
        name="ssd_sample_step",
    )(dt[:, :SSM_HEADS], dec[:, :SSM_HEADS], d_skip, xt, bm, cm, h0)
    y = yt.transpose(0, 2, 1).reshape(NS, D_INNER)
    yn = _small_call(_ssd_gate_kernel, _sds(NS, D_INNER), y, z, norm_w.reshape(1, D_INNER), name="ssd_sample_gate")
    h2 = hp_linear(w_out, layer, D_MODEL, x=yn, tk=1024, tn=1024, res=h)[0]
    return h2, (h1, jnp.concatenate([conv_past[:, 1:], xbc[:, None]], axis=1))


def ffn_sample(h, norm_gain, wg, wu, wd, widx, n_e):
    g = hp_linear(wg, widx, D_FF, norm=(h, norm_gain), n_e=n_e, tk=D_MODEL, tn=1024)
    u = hp_linear(wu, widx, D_FF, norm=(h, norm_gain), n_e=n_e, tk=D_MODEL, tn=1024)
    return g, u


def ple_sample(h, p, ple_gain, w_gate, w_proj, layer):
    pp = hp_linear(w_proj, layer, D_MODEL, x=p, tk=PLE_DIM, tn=D_MODEL)[0]
    return hp_linear(w_gate, layer, D_MODEL, norm=(h, ple_gain), tk=D_MODEL, tn=512, res=h, pp=pp)[0]


def kernel(x_prompt, x_sample, cache_k_win, cache_v_win, state_short_conv, state_ssm, state_ssm_conv, p_prompt, p_sample, norm_mixer, norm_ffn, attn_w_qkv, attn_q_norm, attn_k_norm, attn_sinks, attn_w_out, sconv_w_in, sconv_conv_w, sconv_w_out, ssm_w_in, ssm_conv_w, ssm_conv_b, ssm_dt_bias, ssm_a_log, ssm_d_skip, ssm_norm, ssm_w_out, ffn_w_gate, ffn_w_up, ffn_w_down, moe_w_router, moe_w_gate, moe_w_up, moe_w_down, ple_norm, ple_w_gate, ple_w_proj):
    flat_e = lambda w: w.reshape((-1,) + w.shape[2:])
    moe_wg, moe_wu, moe_wd = flat_e(moe_w_gate), flat_e(moe_w_up), flat_e(moe_w_down)

    def dense_tiles(tm, widx):
        return jnp.full((T // tm,), widx, jnp.int32), jnp.full((1,), T // tm, jnp.int32)

    h = x_prompt.reshape(T, D_MODEL)
    hn = norm_rows(h, norm_mixer[0], BF16)
    k_p, v_p, sc_p, ssm_p, ssmc_p = [], [], [], [], []
    for i in range(DEPTH):
        kind, j = i % 3, i // 3
        if kind == 0:
            h, hn, new = attention_prompt(hn, h, attn_w_qkv, j, attn_q_norm[j], attn_k_norm[j], attn_sinks[j],
                                          attn_w_out, norm_ffn[i])
            k_p.append(new[0]); v_p.append(new[1])
        elif kind == 1:
            h, hn, new = sconv_prompt(hn, h, sconv_w_in, sconv_conv_w, j, sconv_w_out, norm_ffn[i])
            sc_p.append(new)
        else:
            h, hn, new = ssd_prompt(hn, h, ssm_w_in, j, ssm_conv_w[j], ssm_conv_b[j], ssm_dt_bias[j],
                                    ssm_a_log[j], ssm_d_skip[j], ssm_norm[j], ssm_w_out, norm_ffn[i])
            ssm_p.append(new[0]); ssmc_p.append(new[1])
        c = i // 2
        if i % 2 == 0:
            n_tiles = T // TM_BIG
            a = ffn_up(hn, ffn_w_gate, ffn_w_up, jnp.full((n_tiles,), c, jnp.int32),
                       jnp.full((n_tiles,), 2, jnp.int32), jnp.arange(n_tiles, dtype=jnp.int32), tm=TM_BIG)
            h, hn = ffn_down(a, ffn_w_down, *dense_tiles(TM_BIG, c), tm=TM_BIG, res=h, gain=ple_norm[i])
        else:
            h, hn = moe_prompt(h, norm_ffn[i], moe_w_router[c], moe_w_gate, moe_w_up, moe_w_down, c, ple_norm[i])
        next_gain = norm_mixer[i + 1] if i + 1 < DEPTH else None
        h, hn = proj_residual(hn, ple_w_gate, i, h, next_gain, p=p_prompt[i].reshape(T, PLE_DIM), wp=ple_w_proj)
    y_p = h.reshape(BATCH, SEQ, D_MODEL)

    s = x_sample.reshape(NS, D_MODEL)
    k_s, v_s, sc_s, ssm_s, ssmc_s = [], [], [], [], []
    for i in range(DEPTH):
        kind, j = i % 3, i // 3
        if kind == 0:
            s, new = attention_sample(s, norm_mixer[i], cache_k_win[j], cache_v_win[j], attn_w_qkv, j,
                                      attn_q_norm[j], attn_k_norm[j], attn_sinks[j], attn_w_out)
            k_s.append(new[0]); v_s.append(new[1])
        elif kind == 1:
            s, new = sconv_sample(s, norm_mixer[i], state_short_conv[j], sconv_w_in, sconv_conv_w, j, sconv_w_out)
            sc_s.append(new)
        else:
            s, new = ssd_sample(s, norm_mixer[i], state_ssm_conv[j], state_ssm[j], ssm_w_in, j, ssm_conv_w[j],
                                ssm_conv_b[j], ssm_dt_bias[j], ssm_a_log[j], ssm_d_skip[j], ssm_norm[j], ssm_w_out)
            ssm_s.append(new[0]); ssmc_s.append(new[1])
        c = i // 2
        if i % 2 == 0:
            g, u = ffn_sample(s, norm_ffn[i], ffn_w_gate, ffn_w_up, ffn_w_down, c, 1)
            s = hp_linear(ffn_w_down, c, D_MODEL, swiglu=(g, u), tk=1024, tn=1024, res=s)[0]
        else:
            _, gates = route_top2(s, norm_ffn[i], moe_w_router[c], NS)
            g, u = ffn_sample(s, norm_ffn[i], moe_wg, moe_wu, moe_wd, c * N_EXPERTS, N_EXPERTS)
            d_e = hp_linear(moe_wd, c * N_EXPERTS, D_MODEL, swiglu=(g, u), n_e=N_EXPERTS, tk=1024, tn=1024)
            s = _small_call(_moe_mix_kernel, _sds(NS, D_MODEL), s, gates, d_e, name="moe_mix_sample")
        s = ple_sample(s, p_sample[i].reshape(NS, PLE_DIM), ple_norm[i], ple_w_gate, ple_w_proj, i)
    y_s = s.reshape(NS, 1, D_MODEL)

    return (y_p, y_s, jnp.stack(k_p), jnp.stack(v_p), jnp.stack(k_s), jnp.stack(v_s),
            jnp.stack(sc_p), jnp.stack(sc_s), jnp.stack(ssm_p), jnp.stack(ssm_s),
            jnp.stack(ssmc_p), jnp.stack(ssmc_s))
```
